```python
import math
import jax, jax.numpy as jnp
from jax import lax
import numpy as np

D_MODEL = 1024
BATCH = 8
SEQ = 8192
DEPTH = 4
DEC_BATCH = 8
DEC_SEQ = 32
PAST_LEN = 1024

CHUNK = 64
N_META = 16
Q_BLOCK = 128
SCAN_BLOCK = 64
EPS = 1e-6
L2_EPS = 1e-6

GROUP_W = D_MODEL // 4
MIX_W = 4 * GROUP_W
A_HEADS = 4
A_NOPE = 64
A_ROPE = 32
A_V = GROUP_W // A_HEADS
A_QRANK = 192
A_KVRANK = 128
A_SCALE = (A_NOPE + A_ROPE) ** -0.5
ROPE_BASE = 10000.0
B_HEADS = 4
B_HD = GROUP_W // B_HEADS
B_W_LORA = 64
B_A_LORA = 64
B_G_LORA = 128
B_GN_EPS = 64e-5
C_HEADS = 4
C_P = GROUP_W // C_HEADS
C_GROUPS = 2
C_N = 64
C_CONV = 4
C_CONV_CH = GROUP_W + 2 * C_GROUPS * C_N
D_HEADS = 4
D_DK = GROUP_W // D_HEADS
D_DV = GROUP_W // D_HEADS
D_CONV = 4
D_CONV_CH = 3 * GROUP_W
D_FF = 2816
FFN_CONV = 3
A_COLS = A_QRANK + A_KVRANK + A_ROPE
B_COLS = 3 * GROUP_W + B_W_LORA + B_A_LORA + B_G_LORA
C_COLS = GROUP_W + C_CONV_CH + C_HEADS
D_COLS = D_CONV_CH + GROUP_W + 2 * D_HEADS
IN_COLS = A_COLS + B_COLS + C_COLS + D_COLS

STATE_KEYS = ('ckv', 'krope', 'rwkv_S', 'rwkv_shift', 'ssd_S', 'ssd_conv', 'gdn_S', 'gdn_conv', 'ffn_conv')

kernel_name = 'hymba_style_streaming_hybrid_step'


def _split(u, sizes):
    return jnp.split(u, [int(i) for i in np.cumsum(sizes)[:-1]], axis=-1)


def _rmsnorm(x, g, eps=EPS):
    xf = x.astype(jnp.float32)
    y = xf * lax.rsqrt(jnp.mean(xf * xf, axis=-1, keepdims=True) + eps)
    return (y * g.astype(jnp.float32)).astype(x.dtype)


def _l2norm(x):
    xf = x.astype(jnp.float32)
    return xf * lax.rsqrt(jnp.sum(xf * xf, axis=-1, keepdims=True) + L2_EPS)


def _rope(x, pos):
    half = A_ROPE // 2
    inv = jnp.power(ROPE_BASE, -jnp.arange(half, dtype=jnp.float32) / half)
    ang = pos.astype(jnp.float32)[:, None] * inv
    shp = (pos.shape[0],) + (1,) * (x.ndim - 3) + (half,)
    cos, sin = jnp.cos(ang).reshape(shp), jnp.sin(ang).reshape(shp)
    xf = x.astype(jnp.float32)
    x1, x2 = xf[..., :half], xf[..., half:]
    return jnp.concatenate([x1 * cos - x2 * sin, x1 * sin + x2 * cos], axis=-1).astype(x.dtype)


def _dwconv(x, hist, w, b=None):
    k, t = w.shape[0], x.shape[1]
    xf = jnp.concatenate([hist.astype(x.dtype), x], axis=1)
    y = sum(xf[:, i:i + t] * w[i] for i in range(k))
    if b is not None:
        y = y + b
    return y, xf[:, t:]


def _to_blocks(u, nc):
    b, t = u.shape[:2]
    u = jnp.pad(u, [(0, 0), (0, nc * SCAN_BLOCK - t)] + [(0, 0)] * (u.ndim - 2))
    return jnp.moveaxis(u.reshape((b, nc, SCAN_BLOCK) + u.shape[2:]), 2, 3)


def _from_blocks(u, t):
    b, nc, h, q, d = u.shape
    return jnp.moveaxis(u, 3, 2).reshape(b, nc * q, h, d)[:, :t]


def _mla_attention(q_nope, q_rope, k_nope, k_rope, v, q_cid, k_cid):
    b, t = q_nope.shape[:2]
    nb = -(-t // Q_BLOCK)
    tp = nb * Q_BLOCK

    def blocks(u):
        u = jnp.pad(u, [(0, 0), (0, tp - t)] + [(0, 0)] * (u.ndim - 2))
        return jnp.moveaxis(u.reshape((b, nb, Q_BLOCK) + u.shape[2:]), 1, 0)

    qc = jnp.pad(q_cid, (0, tp - t), constant_values=2 ** 30).reshape(nb, Q_BLOCK)

    def one(args):
        qn, qr, cq = args
        s = (jnp.einsum('bqhd,bkhd->bhqk', qn, k_nope)
             + jnp.einsum('bqhr,bkr->bhqk', qr, k_rope)).astype(jnp.float32) * A_SCALE
        s = jnp.where(k_cid[None, :] <= cq[:, None], s, -jnp.inf)
        p = jax.nn.softmax(s, axis=-1).astype(v.dtype)
        return jnp.einsum('bhqk,bkhd->bqhd', p, v)

    o = lax.map(one, (blocks(q_nope), blocks(q_rope), qc))
    return jnp.moveaxis(o, 0, 1).reshape(b, tp, A_HEADS * A_V)[:, :t]


def _rwkv7(cols, prev, s0, mu, w0, w2, a0, a2, g2, k_k, k_a, r_k, gn_w, gn_b):
    b, t, _ = cols.shape
    cols = cols.astype(jnp.float32)
    shifted = jnp.concatenate([prev.astype(jnp.float32)[:, None], cols[:, :-1]], axis=1)
    xm = cols + (shifted - cols) * mu
    r, k, v, dw, da, dg = _split(xm, [GROUP_W, GROUP_W, GROUP_W, B_W_LORA, B_A_LORA, B_G_LORA])
    w_log = -jax.nn.softplus(-(w0 + jnp.tanh(dw) @ w2)) - 0.5
    a = jax.nn.sigmoid(a0 + da @ a2)
    g = jax.nn.sigmoid(dg) @ g2
    heads = lambda u: u.reshape(b, t, B_HEADS, B_HD)
    kk = _l2norm(heads(k * k_k))
    k = k * (1.0 + (a - 1.0) * k_a)
    rh, kh, vh, ah = heads(r), heads(k), heads(v), heads(a)
    decay = jnp.exp(-jnp.exp(heads(w_log)))

    def step(S, inp):
        r_t, d_t, k_t, v_t, kk_t, a_t = inp
        sa = jnp.einsum('bhvk,bhk->bhv', S, -kk_t)
        S = (S * d_t[:, :, None, :] + sa[..., None] * (kk_t * a_t)[:, :, None, :]
             + v_t[..., None] * k_t[:, :, None, :])
        return S, jnp.einsum('bhvk,bhk->bhv', S, r_t)

    tm = lambda u: jnp.moveaxis(u, 1, 0)
    s_fin, o = lax.scan(step, s0.astype(jnp.float32), tuple(map(tm, (rh, decay, kh, vh, kk, ah))))
    o = jnp.moveaxis(o, 0, 1)
    mean = jnp.mean(o, axis=-1, keepdims=True)
    var = jnp.mean(jnp.square(o - mean), axis=-1, keepdims=True)
    o = ((o - mean) * lax.rsqrt(var + B_GN_EPS)).reshape(b, t, GROUP_W) * gn_w + gn_b
    bonus = jnp.sum(rh * kh * r_k, axis=-1, keepdims=True) * vh
    return (o + bonus.reshape(b, t, GROUP_W)) * g, s_fin, cols[:, -1]


def _ssd_chunked(x, dt, a_head, bm, cm, s0):
    t = x.shape[1]
    nc = -(-t // SCAN_BLOCK)
    x, dt, bm, cm = (_to_blocks(u, nc) for u in (x, dt, bm, cm))
    a_cum = jnp.cumsum(dt * a_head[:, None], axis=-1)
    incl = jnp.tril(jnp.ones((SCAN_BLOCK, SCAN_BLOCK), bool))
    decay = jnp.exp(jnp.where(incl, a_cum[..., :, None] - a_cum[..., None, :], -jnp.inf))
    xdt = x * dt[..., None]
    y_diag = jnp.einsum('bchij,bchjp->bchip', jnp.einsum('bchin,bchjn->bchij', cm, bm) * decay, xdt)
    chunk_states = jnp.einsum('bchjn,bchjp->bchpn', bm * jnp.exp(a_cum[..., -1:] - a_cum)[..., None], xdt)
    chunk_decay = jnp.exp(a_cum[..., -1])

    def step(s, inp):
        cs, cd = inp
        return s * cd[..., None, None] + cs, s

    mv = lambda u: jnp.moveaxis(u, 1, 0)
    s_fin, s_prev = lax.scan(step, s0, (mv(chunk_states), mv(chunk_decay)))
    s_prev = jnp.moveaxis(s_prev, 0, 1)
    y_off = jnp.einsum('bchin,bchpn->bchip', cm, s_prev) * jnp.exp(a_cum)[..., None]
    return _from_blocks(y_diag + y_off, t), s_fin


def _mamba2(cols, hist, s0, conv_w, conv_b, dt_bias, a_log, d_skip, g_norm):
    b, t, _ = cols.shape
    z, xbc, dt = _split(cols, [GROUP_W, C_CONV_CH, C_HEADS])
    xbc, new_hist = _dwconv(xbc, hist, conv_w, conv_b)
    xbc = jax.nn.silu(xbc.astype(jnp.float32))
    xs, bm, cm = _split(xbc, [GROUP_W, C_GROUPS * C_N, C_GROUPS * C_N])
    grp = lambda u: jnp.repeat(u.reshape(b, t, C_GROUPS, C_N), C_HEADS // C_GROUPS, axis=2)
    xh = xs.reshape(b, t, C_HEADS, C_P)
    dt = jax.nn.softplus(dt.astype(jnp.float32) + dt_bias)
    y, s_fin = _ssd_chunked(xh, dt, -jnp.exp(a_log.astype(jnp.float32)), grp(bm), grp(cm), s0.astype(jnp.float32))
    y = (y + d_skip[:, None] * xh).reshape(b, t, GROUP_W)
    y = _rmsnorm(y * jax.nn.silu(z.astype(jnp.float32)), g_norm)
    return y, s_fin, new_hist


def _gdn_chunked(q, k, v, g, beta, s0):
    t = q.shape[1]
    nc = -(-t // SCAN_BLOCK)
    q, k, v, g, beta = (_to_blocks(u, nc) for u in (q, k, v, g, beta))
    g_cum = jnp.cumsum(g, axis=-1)
    kb, vb = k * beta[..., None], v * beta[..., None]
    incl = jnp.tril(jnp.ones((SCAN_BLOCK, SCAN_BLOCK), bool))
    strict = jnp.tril(jnp.ones((SCAN_BLOCK, SCAN_BLOCK), bool), -1)
    decay = jnp.exp(jnp.where(incl, g_cum[..., :, None] - g_cum[..., None, :], -jnp.inf))
    lower = jnp.where(strict, jnp.einsum('bchid,bchjd->bchij', kb, k) * decay, 0.0)
    eye = jnp.eye(SCAN_BLOCK, dtype=lower.dtype)
    tinv = lax.linalg.triangular_solve(eye + lower, jnp.broadcast_to(eye, lower.shape),
                                       left_side=True, lower=True, unit_diagonal=True)
    w = tinv @ (kb * jnp.exp(g_cum)[..., None])
    u = tinv @ vb
    a_qk = jnp.where(incl, jnp.einsum('bchid,bchjd->bchij', q, k) * decay, 0.0)
    q_dec = q * jnp.exp(g_cum)[..., None]
    k_dec = k * jnp.exp(g_cum[..., -1:] - g_cum)[..., None]
    g_last = jnp.exp(g_cum[..., -1])

    def step(S, inp):
        w_c, u_c, qd_c, a_c, kd_c, gl_c = inp
        v_new = u_c - jnp.einsum('bhik,bhkv->bhiv', w_c, S)
        o = jnp.einsum('bhik,bhkv->bhiv', qd_c, S) + jnp.einsum('bhij,bhjv->bhiv', a_c, v_new)
        S = S * gl_c[..., None, None] + jnp.einsum('bhik,bhiv->bhkv', kd_c, v_new)
        return S, o

    mv = lambda z: jnp.moveaxis(z, 1, 0)
    s_fin, o = lax.scan(step, s0, tuple(map(mv, (w, u, q_dec, a_qk, k_dec, g_last))))
    return _from_blocks(jnp.moveaxis(o, 0, 1), t), s_fin


def _gdn(cols, hist, s0, conv_w, a_log, dt_bias, g_norm):
    b, t, _ = cols.shape
    qkv, z, beta, alpha = _split(cols, [D_CONV_CH, GROUP_W, D_HEADS, D_HEADS])
    qkv, new_hist = _dwconv(qkv, hist, conv_w)
    q, k, v = _split(jax.nn.silu(qkv.astype(jnp.float32)), [GROUP_W] * 3)
    q = _l2norm(q.reshape(b, t, D_HEADS, D_DK)) * (D_DK ** -0.5)
    k = _l2norm(k.reshape(b, t, D_HEADS, D_DK))
    v = v.reshape(b, t, D_HEADS, D_DV)
    g = -jnp.exp(a_log.astype(jnp.float32)) * jax.nn.softplus(alpha.astype(jnp.float32) + dt_bias)
    o, s_fin = _gdn_chunked(q, k, v, g, jax.nn.sigmoid(beta.astype(jnp.float32)), s0.astype(jnp.float32))
    o = _rmsnorm(o, g_norm) * jax.nn.silu(z.astype(jnp.float32)).reshape(b, t, D_HEADS, D_DV)
    return o.reshape(b, t, GROUP_W), s_fin, new_hist


def _conv_ffn(h, hist, w_up, conv_w, w_down):
    u, new_hist = _dwconv(h @ w_up, hist, conv_w)
    gate, val = jnp.split(u, 2, axis=-1)
    return (jax.nn.silu(gate) * val) @ w_down, new_hist


def _zero_state(b, dtype):
    z = lambda *s: jnp.zeros((DEPTH, b) + s, dtype)
    return dict(ckv=z(0, A_KVRANK), krope=z(0, A_ROPE), rwkv_S=z(B_HEADS, B_HD, B_HD),
                rwkv_shift=z(B_COLS), ssd_S=z(C_HEADS, C_P, C_N), ssd_conv=z(C_CONV - 1, C_CONV_CH),
                gdn_S=z(D_HEADS, D_DK, D_DV), gdn_conv=z(D_CONV - 1, D_CONV_CH),
                ffn_conv=z(FFN_CONV - 1, 2 * D_FF))


def _trunk(x, pos, q_cid, st, P):
    b, t, _ = x.shape
    dtype = x.dtype
    k_cid = jnp.concatenate([jnp.full((st['ckv'].shape[2],), -1, jnp.int32), q_cid])
    new = {name: [] for name in STATE_KEYS}
    for l in range(DEPTH):
        h = _rmsnorm(x, P['norm1_g'][l])
        pa, pb, pc, pd = _split(h @ P['w_in'][l], [A_COLS, B_COLS, C_COLS, D_COLS])
        q_lat, c_raw, kr_raw = _split(pa, [A_QRANK, A_KVRANK, A_ROPE])
        q = (_rmsnorm(q_lat, P['a_gq'][l]) @ P['a_wuq'][l]).reshape(b, t, A_HEADS, A_NOPE + A_ROPE)
        c = _rmsnorm(c_raw, P['a_gkv'][l])
        kr = _rope(kr_raw, pos)
        c_all = jnp.concatenate([st['ckv'][l].astype(c.dtype), c], axis=1)
        kr_all = jnp.concatenate([st['krope'][l].astype(kr.dtype), kr], axis=1)
        n_k = c_all.shape[1]
        k_nope = (c_all @ P['a_wuk'][l]).reshape(b, n_k, A_HEADS, A_NOPE)
        v = (c_all @ P['a_wuv'][l]).reshape(b, n_k, A_HEADS, A_V)
        ya = _mla_attention(q[..., :A_NOPE], _rope(q[..., A_NOPE:], pos), k_nope, kr_all, v, q_cid, k_cid)
        ya = _rmsnorm(ya, P['a_gout'][l])
        yb, s_b, sh_b = _rwkv7(pb, st['rwkv_shift'][l], st['rwkv_S'][l], P['b_mu'][l], P['b_w0'][l],
                               P['b_w2'][l], P['b_a0'][l], P['b_a2'][l], P['b_g2'][l], P['b_kk'][l],
                               P['b_ka'][l], P['b_rk'][l], P['b_gnw'][l], P['b_gnb'][l])
        yc, s_c, cv_c = _mamba2(pc, st['ssd_conv'][l], st['ssd_S'][l], P['c_convw'][l], P['c_convb'][l],
                                P['c_dtb'][l], P['c_alog'][l], P['c_d'][l], P['c_gnorm'][l])
        yd, s_d, cv_d = _gdn(pd, st['gdn_conv'][l], st['gdn_S'][l], P['d_convw'][l], P['d_alog'][l],
                             P['d_dtb'][l], P['d_gnorm'][l])
        mix = jnp.concatenate([ya, yb.astype(dtype), yc.astype(dtype), yd.astype(dtype)], axis=-1)
        x = x + mix @ P['w_out'][l]
        f, cv_f = _conv_ffn(_rmsnorm(x, P['norm2_g'][l]), st['ffn_conv'][l], P['f_wup'][l],
                            P['f_convw'][l], P['f_wdown'][l])
        x = x + f
        for name, val in zip(STATE_KEYS, (c, kr, s_b, sh_b, s_c, cv_c, s_d, cv_d, cv_f)):
            new[name].append(val)
    return _rmsnorm(x, P['final_g']), {name: jnp.stack(vals) for name, vals in new.items()}


def setup_inputs(seed: int = 0) -> dict:
    key = jax.random.key(seed)
    ks = iter(jax.random.split(key, 64))
    nrm = lambda shape, scale: scale * jax.random.normal(next(ks), shape, jnp.float32)
    gain = lambda shape: 1.0 + 0.02 * jax.random.normal(next(ks), shape, jnp.float32)
    uni = lambda shape, lo, hi: jax.random.uniform(next(ks), shape, jnp.float32, lo, hi)

    def dt_bias(n):
        dt = jnp.exp(uni((DEPTH, n), math.log(1e-3), math.log(1e-1)))
        return dt + jnp.log(-jnp.expm1(-dt))

    return {
        'x_prompt': nrm((BATCH, SEQ, D_MODEL), 1.0),
        'x_sample': nrm((DEC_BATCH, DEC_SEQ, D_MODEL), 1.0),
        'cache_mla_ckv': nrm((DEPTH, DEC_BATCH, PAST_LEN, A_KVRANK), 1.0),
        'cache_mla_krope': nrm((DEPTH, DEC_BATCH, PAST_LEN, A_ROPE), 1.0),
        'state_rwkv': nrm((DEPTH, DEC_BATCH, B_HEADS, B_HD, B_HD), 0.3),
        'state_rwkv_shift': nrm((DEPTH, DEC_BATCH, B_COLS), 1.0),
        'state_ssd': nrm((DEPTH, DEC_BATCH, C_HEADS, C_P, C_N), 0.1),
        'state_ssd_conv': nrm((DEPTH, DEC_BATCH, C_CONV - 1, C_CONV_CH), 1.0),
        'state_gdn': nrm((DEPTH, DEC_BATCH, D_HEADS, D_DK, D_DV), 0.1),
        'state_gdn_conv': nrm((DEPTH, DEC_BATCH, D_CONV - 1, D_CONV_CH), 1.0),
        'state_ffn_conv': nrm((DEPTH, DEC_BATCH, FFN_CONV - 1, 2 * D_FF), 0.5),
        'meta_tokens': nrm((N_META, D_MODEL), 1.0),
        'norm1_g': gain((DEPTH, D_MODEL)),
        'w_in': nrm((DEPTH, D_MODEL, IN_COLS), D_MODEL ** -0.5),
        'a_gq': gain((DEPTH, A_QRANK)),
        'a_wuq': nrm((DEPTH, A_QRANK, A_HEADS * (A_NOPE + A_ROPE)), A_QRANK ** -0.5),
        'a_gkv': gain((DEPTH, A_KVRANK)),
        'a_wuk': nrm((DEPTH, A_KVRANK, A_HEADS * A_NOPE), A_KVRANK ** -0.5),
        'a_wuv': nrm((DEPTH, A_KVRANK, A_HEADS * A_V), A_KVRANK ** -0.5),
        'a_gout': gain((DEPTH, GROUP_W)),
        'b_mu': uni((DEPTH, B_COLS), 0.0, 1.0),
        'b_w0': uni((DEPTH, GROUP_W), -3.0, 1.0),
        'b_w2': nrm((DEPTH, B_W_LORA, GROUP_W), 0.1 * B_W_LORA ** -0.5),
        'b_a0': nrm((DEPTH, GROUP_W), 0.1),
        'b_a2': nrm((DEPTH, B_A_LORA, GROUP_W), 0.3 * B_A_LORA ** -0.5),
        'b_g2': nrm((DEPTH, B_G_LORA, GROUP_W), B_G_LORA ** -0.5),
        'b_kk': 1.0 + nrm((DEPTH, GROUP_W), 0.05),
        'b_ka': 1.0 + nrm((DEPTH, GROUP_W), 0.05),
        'b_rk': nrm((DEPTH, B_HEADS, B_HD), 0.1),
        'b_gnw': gain((DEPTH, GROUP_W)),
        'b_gnb': nrm((DEPTH, GROUP_W), 0.02),
        'c_convw': nrm((DEPTH, C_CONV, C_CONV_CH), C_CONV ** -0.5),
        'c_convb': nrm((DEPTH, C_CONV_CH), 0.02),
        'c_dtb': dt_bias(C_HEADS),
        'c_alog': jnp.log(uni((DEPTH, C_HEADS), 1.0, 16.0)),
        'c_d': 1.0 + nrm((DEPTH, C_HEADS), 0.1),
        'c_gnorm': gain((DEPTH, GROUP_W)),
        'd_convw': nrm((DEPTH, D_CONV, D_CONV_CH), D_CONV ** -0.5),
        'd_alog': jnp.log(uni((DEPTH, D_HEADS), 1.0, 16.0)),
        'd_dtb': dt_bias(D_HEADS),
        'd_gnorm': gain((DEPTH, D_DV)),
        'w_out': nrm((DEPTH, MIX_W, D_MODEL), MIX_W ** -0.5),
        'norm2_g': gain((DEPTH, D_MODEL)),
        'f_wup': nrm((DEPTH, D_MODEL, 2 * D_FF), D_MODEL ** -0.5),
        'f_convw': nrm((DEPTH, FFN_CONV, 2 * D_FF), FFN_CONV ** -0.5),
        'f_wdown': nrm((DEPTH, D_FF, D_MODEL), D_FF ** -0.5),
        'final_g': gain((D_MODEL,)),
    }


def reference(x_prompt, x_sample, cache_mla_ckv, cache_mla_krope, state_rwkv, state_rwkv_shift,
              state_ssd, state_ssd_conv, state_gdn, state_gdn_conv, state_ffn_conv,
              meta_tokens, norm1_g, w_in, a_gq, a_wuq, a_gkv, a_wuk, a_wuv, a_gout,
              b_mu, b_w0, b_w2, b_a0, b_a2, b_g2, b_kk, b_ka, b_rk, b_gnw, b_gnb,
              c_convw, c_convb, c_dtb, c_alog, c_d, c_gnorm,
              d_convw, d_alog, d_dtb, d_gnorm,
              w_out, norm2_g, f_wup, f_convw, f_wdown, final_g):
    P = dict(norm1_g=norm1_g, w_in=w_in, a_gq=a_gq, a_wuq=a_wuq, a_gkv=a_gkv, a_wuk=a_wuk,
             a_wuv=a_wuv, a_gout=a_gout, b_mu=b_mu, b_w0=b_w0, b_w2=b_w2, b_a0=b_a0, b_a2=b_a2,
             b_g2=b_g2, b_kk=b_kk, b_ka=b_ka, b_rk=b_rk, b_gnw=b_gnw, b_gnb=b_gnb,
             c_convw=c_convw, c_convb=c_convb, c_dtb=c_dtb, c_alog=c_alog, c_d=c_d, c_gnorm=c_gnorm,
             d_convw=d_convw, d_alog=d_alog, d_dtb=d_dtb, d_gnorm=d_gnorm,
             w_out=w_out, norm2_g=norm2_g, f_wup=f_wup, f_convw=f_convw, f_wdown=f_wdown,
             final_g=final_g)
    b_p = x_prompt.shape[0]
    meta = jnp.broadcast_to(meta_tokens[None].astype(x_prompt.dtype), (b_p, N_META, D_MODEL))
    x_ext = jnp.concatenate([meta, x_prompt], axis=1)
    idx = jnp.arange(x_ext.shape[1], dtype=jnp.int32)
    pos_p = idx - N_META
    cid_p = jnp.where(idx < N_META, 0, 1 + (idx - N_META) // CHUNK)
    y_p, ns_p = _trunk(x_ext, pos_p, cid_p, _zero_state(b_p, x_prompt.dtype), P)
    t_s = x_sample.shape[1]
    past = cache_mla_ckv.shape[2]
    ar = jnp.arange(t_s, dtype=jnp.int32)
    st_s = dict(ckv=cache_mla_ckv, krope=cache_mla_krope, rwkv_S=state_rwkv, rwkv_shift=state_rwkv_shift,
                ssd_S=state_ssd, ssd_conv=state_ssd_conv, gdn_S=state_gdn, gdn_conv=state_gdn_conv,
                ffn_conv=state_ffn_conv)
    y_s, ns_s = _trunk(x_sample, past + ar, ar // CHUNK, st_s, P)
    return (y_p[:, N_META:], y_s,
            ns_p['ckv'], ns_p['krope'], ns_p['rwkv_S'], ns_p['rwkv_shift'], ns_p['ssd_S'], ns_p['ssd_conv'],
            ns_p['gdn_S'], ns_p['gdn_conv'], ns_p['ffn_conv'],
            ns_s['ckv'], ns_s['krope'], ns_s['rwkv_S'], ns_s['rwkv_shift'], ns_s['ssd_S'], ns_s['ssd_conv'],
            ns_s['gdn_S'], ns_s['gdn_conv'], ns_s['ffn_conv'])
```

```python
import functools
import math

import numpy as np
import jax
import jax.numpy as jnp
from jax import lax
from jax.experimental import pallas as pl
from jax.experimental.pallas import tpu as pltpu

F32 = jnp.float32
BF16 = jnp.bfloat16

D_MODEL = 1024
DEPTH = 4
CHUNK = 64
N_META = 16
EPS = 1e-6
L2_EPS = 1e-6
GROUP_W = 256
N_HEADS = 4
HEAD_W = 64
A_NOPE = 64
A_ROPE = 32
A_QRANK = 192
A_KVRANK = 128
A_SCALE = (A_NOPE + A_ROPE) ** -0.5
ROPE_BASE = 10000.0
B_GN_EPS = 64e-5
B_COLS = 1024
C_CONV = 4
D_CONV = 4
D_FF = 2816
FFN_CONV = 3
A_COLS = 352
C_COLS = 772
D_COLS = 1032
IN_COLS = A_COLS + B_COLS + C_COLS + D_COLS

LANE = 128
SUBLANE = 8
VMEM_LIMIT = 56 * 1024 * 1024
NEG = -1e30
FFN_CW = 256
ATT_BLOCK = 256
ROW_TILE = 256


def _mm(a, b):
    return jnp.dot(a.astype(BF16), b.astype(BF16), preferred_element_type=F32)


def _mm_nt(a, b):
    return lax.dot_general(a.astype(BF16), b.astype(BF16), (((1,), (1,)), ((), ())),
                           preferred_element_type=F32)


def _mm_tn(a, b):
    return lax.dot_general(a.astype(BF16), b.astype(BF16), (((0,), (0,)), ((), ())),
                           preferred_element_type=F32)


def _split2(x):
    hi = x.astype(BF16)
    lo = (x - hi.astype(F32)).astype(BF16)
    return hi, lo


def _mm_x2(x, w):
    hi, lo = _split2(x)
    return (jnp.dot(hi, w, preferred_element_type=F32)
            + jnp.dot(lo, w, preferred_element_type=F32))


def _rms(x, g, n):
    ms = jnp.sum(x * x, axis=-1, keepdims=True) * (1.0 / n)
    return x * lax.rsqrt(ms + EPS) * g


def _softplus(x):
    return jnp.maximum(x, 0.0) + jnp.log1p(jnp.exp(-jnp.abs(x)))


def _sigmoid(x):
    return 1.0 / (1.0 + jnp.exp(-x))


def _silu(x):
    return x * _sigmoid(x)


def _iota(shape, axis):
    return lax.broadcasted_iota(jnp.int32, shape, axis)


def _norm_in(x_ref, g_ref, j, tm, front):
    h = _rms(x_ref[0], g_ref[...], D_MODEL)
    if front > 0:
        rows = j * tm + _iota((tm, 1), 0)
        h = jnp.where(rows >= front, h, 0.0)
    return h.astype(BF16)


def _head_ones():
    r = _iota((GROUP_W, GROUP_W), 0)
    c = _iota((GROUP_W, GROUP_W), 1)
    return jnp.where((r >> 6) == (c >> 6), 1.0, 0.0).astype(BF16)


def _hsum(x, bd):
    return _mm_x2(x, bd)


class _Stk:
    def __init__(self, c):
        self.c = c
        n = N_HEADS * c
        self.n = n
        lc = int(math.log2(c))
        assert 1 << lc == c
        self.steps = lc
        r = _iota((n, GROUP_W), 0)
        l = _iota((n, GROUP_W), 1)
        self.hm = (r >> lc) == (l >> 6)
        rr = _iota((n, n), 0)
        cc = _iota((n, n), 1)
        self.rr, self.cc = rr, cc
        same = (rr >> lc) == (cc >> lc)
        self.incl = same & (cc <= rr)
        self.strict = same & (cc < rr)
        tr = _iota((c, c), 0)
        tc = _iota((c, c), 1)
        self.tl = jnp.where(tc <= tr, 1.0, 0.0).astype(BF16)

    def tile(self, x):
        return jnp.concatenate([x] * N_HEADS, axis=0)

    def stack(self, x):
        return jnp.where(self.hm, self.tile(x), 0.0)

    def unstack(self, xs):
        c = self.c
        return xs[0:c] + xs[c:2 * c] + xs[2 * c:3 * c] + xs[3 * c:4 * c]

    def cumsum(self, x):
        hi, lo = _split2(x)
        return (jnp.dot(self.tl, hi, preferred_element_type=F32)
                + jnp.dot(self.tl, lo, preferred_element_type=F32))

    def decay(self, g_cum):
        n = self.n
        gs = self.stack(g_cum)
        hi, lo = _split2(gs)
        w = 1.0 / HEAD_W
        o1 = jnp.full((GROUP_W, n), w, BF16)
        o2 = jnp.full((n, GROUP_W), w, BF16)
        gc = jnp.dot(hi, o1, preferred_element_type=F32) + jnp.dot(lo, o1, preferred_element_type=F32)
        gr = _mm_nt(o2, hi) + _mm_nt(o2, lo)
        return jnp.exp(jnp.where(self.incl, gc - gr, NEG))

    def tri_inv(self, lower):
        minv = jnp.where(self.rr == self.cc, 1.0, 0.0)
        for lb in range(self.steps):
            join = (((self.rr >> (lb + 1)) == (self.cc >> (lb + 1)))
                    & (((self.rr >> lb) & 1) == 1) & (((self.cc >> lb) & 1) == 0))
            moff = jnp.where(join, lower, 0.0)
            minv = minv - _mm(_mm(minv, moff), minv)
        return minv


def _mla_prep_kernel(x_ref, g1_ref, wa_ref, gq_ref, wq1_ref, wq2_ref, gkv_ref, tab_ref,
                     q_out, ckv_out, kr_out, *, tm, front):
    j = pl.program_id(1)
    h = _norm_in(x_ref, g1_ref, j, tm, front)
    pa = jnp.dot(h, wa_ref[...], preferred_element_type=F32)
    qn = _rms(pa[:, 0:256], gq_ref[...], A_QRANK).astype(BF16)
    q1 = jnp.dot(qn, wq1_ref[...], preferred_element_type=F32)
    q2 = jnp.dot(qn, wq2_ref[...], preferred_element_type=F32)
    tab = tab_ref[...]
    cos4 = jnp.concatenate([tab[:, 0:128]] * N_HEADS, axis=1)
    sin4 = jnp.concatenate([tab[:, 128:256]] * N_HEADS, axis=1)
    q_out[0] = ((q1 * cos4 + q2 * sin4) * A_SCALE).astype(BF16)
    ckv_out[0] = _rms(pa[:, 256:384], gkv_ref[...], A_KVRANK)
    kr = pa[:, 384:512] * tab[:, 256:384] + pa[:, 512:640] * tab[:, 384:512]
    kr_out[0] = kr[:, 0:A_ROPE]


def _kv_up_kernel(c_ref, kr_ref, wk_ref, ek_ref, wv_ref, k_out, v_out):
    c = c_ref[0].astype(BF16)
    kr = kr_ref[0].astype(BF16)
    k = (jnp.dot(c, wk_ref[...], preferred_element_type=F32)
         + jnp.dot(kr, ek_ref[...], preferred_element_type=F32))
    k_out[0] = k.astype(BF16)
    v_out[0] = jnp.dot(c, wv_ref[...], preferred_element_type=F32).astype(BF16)


def _flash_kernel(q_ref, k_ref, v_ref, gout_ref, o_ref, m_sc, l_sc, acc_sc,
                  *, tq, tk, nkv, causal, klo, khi):
    i = pl.program_id(1)
    pair = []
    for h in range(N_HEADS):
        lanes = slice(LANE * h, LANE * (h + 1))
        q = q_ref[0, :, lanes]
        m_sc[...] = jnp.full((tq, 1), NEG, F32)
        l_sc[...] = jnp.zeros((tq, 1), F32)
        acc_sc[...] = jnp.zeros((tq, LANE), F32)

        def step(jb, masked, q=q, lanes=lanes):
            start = pl.multiple_of(jb * tk, tk)
            k = k_ref[0, pl.ds(start, tk), lanes]
            v = v_ref[0, pl.ds(start, tk), lanes]
            s = _mm_nt(q, k)
            if masked:
                qpos = i * tq + _iota((tq, 1), 0)
                kpos = jb * tk + _iota((1, tk), 1)
                vis = (kpos >= klo) & (kpos < khi)
                if causal:
                    vis = vis & ((kpos >> 6) <= (qpos >> 6))
                s = jnp.where(vis, s, NEG)
            m_prev = m_sc[...]
            m_new = jnp.maximum(m_prev, jnp.max(s, axis=-1, keepdims=True))
            alpha = jnp.exp(m_prev - m_new)
            p = jnp.exp(s - m_new)
            l_sc[...] = alpha * l_sc[...] + jnp.sum(p, axis=-1, keepdims=True)
            acc_sc[...] = alpha * acc_sc[...] + _mm(p, v)
            m_sc[...] = m_new

        if causal:
            step(0, True)

            def body(jb, carry):
                step(jb, False)
                return carry

            lax.fori_loop(1, i, body, 0)

            @pl.when(i > 0)
            def _():
                step(i, True)
        else:
            for jb in range(nkv):
                step(jb, True)
        pair.append(acc_sc[...] / l_sc[...])
    ya = jnp.concatenate([pair[0] + pair[1], pair[2] + pair[3]], axis=1)
    o_ref[0] = _rms(ya, gout_ref[...], GROUP_W)


def _rwkv_kernel(x_ref, g1_ref, wb_ref, shift_ref, s0_ref, mu_ref, w0_ref, wl_ref, a0_ref,
                 g2_ref, kk_ref, ka_ref, rk_ref, gnw_ref, gnb_ref,
                 y_out, s_out, shift_out, work, st, *, tm, c, front):
    j = pl.program_id(1)

    @pl.when(j == 0)
    def _():
        work[0:SUBLANE, :] = shift_ref[0]
        st[...] = s0_ref[0]

    h = _norm_in(x_ref, g1_ref, j, tm, front)
    cols = jnp.dot(h, wb_ref[...], preferred_element_type=F32)
    work[SUBLANE:SUBLANE + tm, :] = cols
    shifted = work[pl.ds(SUBLANE - 1, tm), :]
    tail = work[tm:tm + SUBLANE, :]
    work[0:SUBLANE, :] = tail
    shift_out[0] = tail
    xm = cols + (shifted - cols) * mu_ref[...]
    r = xm[:, 0:256]
    k = xm[:, 256:512]
    v = xm[:, 512:768]
    lora = xm[:, 768:896]
    dg = xm[:, 896:1024]
    lora = jnp.where(_iota((tm, LANE), 1) < 64, jnp.tanh(lora), lora)
    ll = _mm(lora, wl_ref[...])
    w_log = -_softplus(-(w0_ref[...] + ll[:, 0:256])) - 0.5
    logd = -jnp.exp(w_log)
    a = _sigmoid(a0_ref[...] + ll[:, 256:512])
    g = _mm(_sigmoid(dg), g2_ref[...])
    bd = _head_ones()
    kkr = k * kk_ref[...]
    kk = kkr * lax.rsqrt(_hsum(kkr * kkr, bd) + L2_EPS)
    k2 = k * (1.0 + (a - 1.0) * ka_ref[...])

    sk = _Stk(c)
    outs = []
    s = st[...]
    for ci in range(tm // c):
        rows = slice(ci * c, (ci + 1) * c)
        ld = logd[rows]
        gc = sk.cumsum(ld)
        eg = jnp.exp(gc)
        eng = jnp.exp(-gc)
        at = -kk[rows] * jnp.exp(gc - ld)
        bt = kk[rows] * a[rows] * eng
        kt = k2[rows] * eng
        rt = r[rows] * eg
        dc = eg[c - 1:c, :]
        at_s = sk.stack(at)
        rt_s = sk.stack(rt)
        v_s = sk.stack(v[rows])
        bt_t = sk.tile(bt)
        kt_t = sk.tile(kt)
        lab = jnp.where(sk.strict, _mm_nt(at_s, bt_t), 0.0)
        aak = jnp.where(sk.strict, _mm_nt(at_s, kt_t), 0.0)
        arb = jnp.where(sk.incl, _mm_nt(rt_s, bt_t), 0.0)
        ark = jnp.where(sk.incl, _mm_nt(rt_s, kt_t), 0.0)
        sb = s.astype(BF16)
        u = _mm(sk.tri_inv(-lab), _mm_nt(at_s, sb) + _mm(aak, v_s))
        o_s = _mm_nt(rt_s, sb) + _mm(arb, u) + _mm(ark, v_s)
        s = s * dc + _mm_tn(u, sk.stack(bt * dc)) + _mm_tn(v_s, sk.stack(kt * dc))
        outs.append(sk.unstack(o_s))
    st[...] = s
    s_out[0] = s
    o = jnp.concatenate(outs, axis=0) if len(outs) > 1 else outs[0]
    mean = _hsum(o, bd) * (1.0 / HEAD_W)
    d = o - mean
    var = _hsum(d * d, bd) * (1.0 / HEAD_W)
    o = d * lax.rsqrt(var + B_GN_EPS) * gnw_ref[...] + gnb_ref[...]
    bonus = _hsum(r * k2 * rk_ref[...], bd) * v
    y_out[0] = (o + bonus) * g


def _conv4(work, wv, tm):
    y = work[pl.ds(SUBLANE - 3, tm), :] * wv[0:1, :]
    y = y + work[pl.ds(SUBLANE - 2, tm), :] * wv[1:2, :]
    y = y + work[pl.ds(SUBLANE - 1, tm), :] * wv[2:3, :]
    return y + work[pl.ds(SUBLANE, tm), :] * wv[3:4, :]


def _ssd_kernel(x_ref, g1_ref, wc_ref, hist_ref, s0_ref, cw_ref, cb_ref, dtb_ref, alog_ref,
                dskip_ref, gn_ref, y_out, s_out, hist_out, work, st, *, tm, c, front):
    j = pl.program_id(1)

    @pl.when(j == 0)
    def _():
        work[0:SUBLANE, :] = hist_ref[0]
        st[...] = s0_ref[0]

    h = _norm_in(x_ref, g1_ref, j, tm, front)
    pc = jnp.dot(h, wc_ref[...], preferred_element_type=F32)
    z = pc[:, 0:256]
    work[SUBLANE:SUBLANE + tm, :] = pc[:, 256:1024]
    xbc = _silu(_conv4(work, cw_ref[...], tm) + cb_ref[...])
    tail = work[tm:tm + SUBLANE, :]
    work[0:SUBLANE, :] = tail
    hist_out[0] = tail
    xs = xbc[:, 0:256]
    bm = xbc[:, 256:512]
    cm = xbc[:, 512:768]
    dt = _softplus(pc[:, 1024:1280] + dtb_ref[...])
    if front > 0:
        rows_i = j * tm + _iota((tm, 1), 0)
        dt = jnp.where(rows_i >= front, dt, 0.0)
    a = dt * (-jnp.exp(alog_ref[...]))
    xdt = xs * dt

    sk = _Stk(c)
    outs = []
    s = st[...]
    for ci in range(tm // c):
        rows = slice(ci * c, (ci + 1) * c)
        ac = sk.cumsum(a[rows])
        al = ac[c - 1:c, :]
        dm = sk.decay(ac)
        amat = _mm_nt(sk.stack(cm[rows]), sk.tile(bm[rows])) * dm
        xdt_s = sk.stack(xdt[rows])
        y_s = _mm(amat, xdt_s) + _mm(sk.stack(cm[rows] * jnp.exp(ac)), s)
        s = s * jnp.exp(al) + _mm_tn(sk.stack(bm[rows] * jnp.exp(al - ac)), xdt_s)
        outs.append(sk.unstack(y_s))
    st[...] = s
    s_out[0] = s
    y = jnp.concatenate(outs, axis=0) if len(outs) > 1 else outs[0]
    y = y + dskip_ref[...] * xs
    y_out[0] = _rms(y * _silu(z), gn_ref[...], GROUP_W)


def _gdn_kernel(x_ref, g1_ref, wd_ref, hist_ref, s0_ref, cw_ref, alog_ref, dtb_ref, gn_ref,
                y_out, s_out, hist_out, work, st, *, tm, c, front):
    j = pl.program_id(1)

    @pl.when(j == 0)
    def _():
        work[0:SUBLANE, :] = hist_ref[0]
        st[...] = s0_ref[0]

    h = _norm_in(x_ref, g1_ref, j, tm, front)
    pd = jnp.dot(h, wd_ref[...], preferred_element_type=F32)
    work[SUBLANE:SUBLANE + tm, :] = pd[:, 0:768]
    qkv = _silu(_conv4(work, cw_ref[...], tm))
    tail = work[tm:tm + SUBLANE, :]
    work[0:SUBLANE, :] = tail
    hist_out[0] = tail
    z = pd[:, 768:1024]
    beta = _sigmoid(pd[:, 1024:1280])
    g = -jnp.exp(alog_ref[...]) * _softplus(pd[:, 1280:1536] + dtb_ref[...])
    bd = _head_ones()
    q = qkv[:, 0:256]
    k = qkv[:, 256:512]
    v = qkv[:, 512:768]
    q = q * lax.rsqrt(_hsum(q * q, bd) + L2_EPS) * (HEAD_W ** -0.5)
    k = k * lax.rsqrt(_hsum(k * k, bd) + L2_EPS)

    sk = _Stk(c)
    outs = []
    s = st[...]
    for ci in range(tm // c):
        rows = slice(ci * c, (ci + 1) * c)
        gc = sk.cumsum(g[rows])
        eg = jnp.exp(gc)
        gl = gc[c - 1:c, :]
        kc = k[rows]
        kb = kc * beta[rows]
        k_t = sk.tile(kc)
        dm = sk.decay(gc)
        lower = jnp.where(sk.strict, _mm_nt(sk.stack(kb), k_t) * dm, 0.0)
        aqk = _mm_nt(sk.stack(q[rows]), k_t) * dm
        sb = s.astype(BF16)
        rhs = sk.stack(v[rows] * beta[rows]) - _mm(sk.stack(kb * eg), sb)
        vn = _mm(sk.tri_inv(lower), rhs)
        o_s = _mm(sk.stack(q[rows] * eg), sb) + _mm(aqk, vn)
        s = s * jnp.exp(gl) + _mm_tn(sk.stack(kc * jnp.exp(gl - gc)), vn)
        outs.append(sk.unstack(o_s))
    st[...] = s
    s_out[0] = s
    o = jnp.concatenate(outs, axis=0) if len(outs) > 1 else outs[0]
    ms = _hsum(o * o, bd) * (1.0 / HEAD_W)
    y_out[0] = o * lax.rsqrt(ms + EPS) * gn_ref[...] * _silu(z)


def _ffn_kernel(x_ref, ya_ref, yb_ref, yc_ref, yd_ref, wo_ref, g2_ref, wup_ref, cw_ref, wdn_ref,
                hist_ref, xo_ref, hist_out, carry, work, *, tm, front):
    j = pl.program_id(1)

    @pl.when(j == 0)
    def _():
        carry[...] = hist_ref[0]

    x = x_ref[0]
    for idx, y_ref in enumerate((ya_ref, yb_ref, yc_ref, yd_ref)):
        x = x + jnp.dot(y_ref[0].astype(BF16), wo_ref[GROUP_W * idx:GROUP_W * (idx + 1), :],
                        preferred_element_type=F32)
    h2 = _rms(x, g2_ref[...], D_MODEL)
    if front > 0:
        rows = j * tm + _iota((tm, 1), 0)
        h2 = jnp.where(rows >= front, h2, 0.0)
    h2 = h2.astype(BF16)
    acc = jnp.zeros((tm, D_MODEL), F32)
    w2 = 2 * FFN_CW
    for f in range(D_FF // FFN_CW):
        cols = slice(f * w2, (f + 1) * w2)
        u = jnp.dot(h2, wup_ref[:, cols], preferred_element_type=F32)
        work[0:SUBLANE, :] = carry[:, cols]
        work[SUBLANE:SUBLANE + tm, :] = u
        cw = cw_ref[:, cols]
        y = (work[pl.ds(SUBLANE - 2, tm), :] * cw[0:1, :]
             + work[pl.ds(SUBLANE - 1, tm), :] * cw[1:2, :] + u * cw[2:3, :])
        carry[:, cols] = work[tm:tm + SUBLANE, :]
        act = _silu(y[:, 0:FFN_CW]) * y[:, FFN_CW:w2]
        acc = acc + jnp.dot(act.astype(BF16), wdn_ref[f * FFN_CW:(f + 1) * FFN_CW, :],
                            preferred_element_type=F32)
    xo_ref[0] = x + acc
    hist_out[0] = carry[...]


def _final_norm_kernel(x_ref, g_ref, o_ref):
    o_ref[0] = _rms(x_ref[0], g_ref[...], D_MODEL)


def _const_spec(arr):
    nd = arr.ndim
    return pl.BlockSpec(arr.shape, lambda b, j: (0,) * nd, pipeline_mode=pl.Buffered(1))


def _tile_spec(tm, width):
    return pl.BlockSpec((1, tm, width), lambda b, j: (b, j, 0))


def _batch_spec(rows, width):
    return pl.BlockSpec((1, rows, width), lambda b, j: (b, 0, 0))


def _params():
    return pltpu.CompilerParams(dimension_semantics=("arbitrary", "arbitrary"),
                                vmem_limit_bytes=VMEM_LIMIT)


def _mla_prep(x, g1, wa, gq, wq1, wq2, gkv, tab, tm, front):
    b, l, _ = x.shape
    consts = (g1, wa, gq, wq1, wq2, gkv)
    return pl.pallas_call(
        functools.partial(_mla_prep_kernel, tm=tm, front=front),
        grid=(b, l // tm),
        in_specs=[_tile_spec(tm, D_MODEL)] + [_const_spec(a) for a in consts]
        + [pl.BlockSpec((tm, 512), lambda bb, j: (j, 0))],
        out_specs=[_tile_spec(tm, 512), _tile_spec(tm, A_KVRANK), _tile_spec(tm, A_ROPE)],
        out_shape=[jax.ShapeDtypeStruct((b, l, 512), BF16),
                   jax.ShapeDtypeStruct((b, l, A_KVRANK), F32),
                   jax.ShapeDtypeStruct((b, l, A_ROPE), F32)],
        compiler_params=_params(), name="mla_prep",
    )(x, *consts, tab)


def _kv_up(c_all, kr_all, wk, ek, wv, tm):
    b, n, _ = c_all.shape
    consts = (wk, ek, wv)
    return pl.pallas_call(
        _kv_up_kernel,
        grid=(b, n // tm),
        in_specs=[_tile_spec(tm, A_KVRANK), _tile_spec(tm, A_ROPE)] + [_const_spec(a) for a in consts],
        out_specs=[_tile_spec(tm, 512), _tile_spec(tm, 512)],
        out_shape=[jax.ShapeDtypeStruct((b, n, 512), BF16)] * 2,
        compiler_params=_params(), name="kv_up",
    )(c_all, kr_all, *consts)


def _flash(q, k, v, gout, tq, tk, causal, klo, khi):
    b, l, _ = q.shape
    n = k.shape[1]
    return pl.pallas_call(
        functools.partial(_flash_kernel, tq=tq, tk=tk, nkv=n // tk, causal=causal, klo=klo, khi=khi),
        grid=(b, l // tq),
        in_specs=[_tile_spec(tq, 512),
                  pl.BlockSpec((1, n, 512), lambda bb, j: (bb, 0, 0), pipeline_mode=pl.Buffered(1)),
                  pl.BlockSpec((1, n, 512), lambda bb, j: (bb, 0, 0), pipeline_mode=pl.Buffered(1)),
                  _const_spec(gout)],
        out_specs=_tile_spec(tq, GROUP_W),
        out_shape=jax.ShapeDtypeStruct((b, l, GROUP_W), F32),
        scratch_shapes=[pltpu.VMEM((tq, 1), F32), pltpu.VMEM((tq, 1), F32), pltpu.VMEM((tq, LANE), F32)],
        compiler_params=_params(), name="mla_flash",
    )(q, k, v, gout)


def _scan_call(kernel, name, x, consts_a, hist, s0, consts_b, width_in, tm, c, front):
    b, l, _ = x.shape
    return pl.pallas_call(
        functools.partial(kernel, tm=tm, c=c, front=front),
        grid=(b, l // tm),
        in_specs=[_tile_spec(tm, D_MODEL)] + [_const_spec(a) for a in consts_a]
        + [_batch_spec(SUBLANE, width_in), _batch_spec(GROUP_W, GROUP_W)]
        + [_const_spec(a) for a in consts_b],
        out_specs=[_tile_spec(tm, GROUP_W), _batch_spec(GROUP_W, GROUP_W), _batch_spec(SUBLANE, width_in)],
        out_shape=[jax.ShapeDtypeStruct((b, l, GROUP_W), F32),
                   jax.ShapeDtypeStruct((b, GROUP_W, GROUP_W), F32),
                   jax.ShapeDtypeStruct((b, SUBLANE, width_in), F32)],
        scratch_shapes=[pltpu.VMEM((tm + SUBLANE, width_in), F32), pltpu.VMEM((GROUP_W, GROUP_W), F32)],
        compiler_params=_params(), name=name,
    )(x, *consts_a, hist, s0, *consts_b)


def _ffn(x, ya, yb, yc, yd, wo, g2, wup, cw, wdn, hist, tm, front):
    b, l, _ = x.shape
    return pl.pallas_call(
        functools.partial(_ffn_kernel, tm=tm, front=front),
        grid=(b, l // tm),
        in_specs=[_tile_spec(tm, D_MODEL)] + [_tile_spec(tm, GROUP_W)] * 4
        + [_const_spec(a) for a in (wo, g2, wup, cw, wdn)] + [_batch_spec(SUBLANE, 2 * D_FF)],
        out_specs=[_tile_spec(tm, D_MODEL), _batch_spec(SUBLANE, 2 * D_FF)],
        out_shape=[jax.ShapeDtypeStruct((b, l, D_MODEL), F32),
                   jax.ShapeDtypeStruct((b, SUBLANE, 2 * D_FF), F32)],
        scratch_shapes=[pltpu.VMEM((SUBLANE, 2 * D_FF), F32),
                        pltpu.VMEM((tm + SUBLANE, 2 * FFN_CW), F32)],
        compiler_params=_params(), name="out_ffn",
    )(x, ya, yb, yc, yd, wo, g2, wup, cw, wdn, hist)


def _final_norm(x, g, tm, skip_tiles, out_rows):
    b = x.shape[0]
    return pl.pallas_call(
        _final_norm_kernel,
        grid=(b, out_rows // tm),
        in_specs=[pl.BlockSpec((1, tm, D_MODEL), lambda bb, j: (bb, j + skip_tiles, 0)), _const_spec(g)],
        out_specs=_tile_spec(tm, D_MODEL),
        out_shape=jax.ShapeDtypeStruct((b, out_rows, D_MODEL), F32),
        compiler_params=_params(), name="final_norm",
    )(x, g)


def _np_idx():
    z = IN_COLS
    zpad = lambda n: [z] * n
    rep = lambda base: [base + i for i in range(N_HEADS) for _ in range(HEAD_W)]
    grp = lambda base: [base + g * HEAD_W + i for g in (0, 0, 1, 1) for i in range(HEAD_W)]
    a = (list(range(0, 192)) + zpad(64) + list(range(192, 320))
         + list(range(320, 352)) + zpad(96)
         + list(range(336, 352)) + list(range(320, 336)) + zpad(96))
    b0 = A_COLS
    bcols = list(range(b0, b0 + B_COLS))
    c0 = b0 + B_COLS
    ccols = (list(range(c0, c0 + 256)) + list(range(c0 + 256, c0 + 512))
             + grp(c0 + 512) + grp(c0 + 640) + rep(c0 + 768))
    d0 = c0 + C_COLS
    dcols = list(range(d0, d0 + 1024)) + rep(d0 + 1024) + rep(d0 + 1028)
    xbc_exp = list(range(256)) + grp(256) + grp(384)
    xbc_back = (list(range(256)) + list(range(256, 320)) + list(range(384, 448))
                + list(range(512, 576)) + list(range(640, 704)))
    ffn_perm = []
    for f in range(D_FF // FFN_CW):
        ffn_perm += list(range(f * FFN_CW, (f + 1) * FFN_CW))
        ffn_perm += list(range(D_FF + f * FFN_CW, D_FF + (f + 1) * FFN_CW))
    ffn_back = np.argsort(np.array(ffn_perm))
    as_i = lambda v: np.asarray(v, np.int32)
    return dict(a=as_i(a), b=as_i(bcols), c=as_i(ccols), d=as_i(dcols), xbc_exp=as_i(xbc_exp),
                xbc_back=as_i(xbc_back), ffn_perm=as_i(ffn_perm), ffn_back=as_i(ffn_back))


_IDX = _np_idx()


def _rep_heads(v):
    return jnp.repeat(v, HEAD_W, axis=-1)


def _pad_rows(w, rows=SUBLANE):
    return jnp.pad(w, [(0, rows - w.shape[0])] + [(0, 0)] * (w.ndim - 1))


def _layer_consts(P, l):
    row = lambda v: v.reshape(1, -1).astype(F32)
    w_in = jnp.concatenate([P['w_in'][l], jnp.zeros((D_MODEL, 1), F32)], axis=1)
    c = {}
    c['g1'] = row(P['norm1_g'][l])
    c['wa'] = w_in[:, _IDX['a']].astype(BF16)
    c['wb'] = w_in[:, _IDX['b']].astype(BF16)
    c['wc'] = w_in[:, _IDX['c']].astype(BF16)
    c['wd'] = w_in[:, _IDX['d']].astype(BF16)
    c['gq'] = row(jnp.pad(P['a_gq'][l], (0, 64)))
    wuq = P['a_wuq'][l].reshape(A_QRANK, N_HEADS, A_NOPE + A_ROPE)
    rope = wuq[:, :, A_NOPE:]
    swap = jnp.concatenate([rope[..., 16:], rope[..., :16]], axis=-1)
    zeros = lambda n: jnp.zeros((A_QRANK, N_HEADS, n), F32)
    wq1 = jnp.concatenate([wuq, zeros(32)], axis=-1).reshape(A_QRANK, 512)
    wq2 = jnp.concatenate([zeros(64), swap, zeros(32)], axis=-1).reshape(A_QRANK, 512)
    c['wq1'] = jnp.pad(wq1, ((0, 64), (0, 0))).astype(BF16)
    c['wq2'] = jnp.pad(wq2, ((0, 64), (0, 0))).astype(BF16)
    c['gkv'] = row(P['a_gkv'][l])
    wuk = P['a_wuk'][l].reshape(A_KVRANK, N_HEADS, A_NOPE)
    c['wk'] = jnp.concatenate([wuk, jnp.zeros((A_KVRANK, N_HEADS, 64), F32)], axis=-1
                              ).reshape(A_KVRANK, 512).astype(BF16)
    ek = np.zeros((A_ROPE, N_HEADS, LANE), np.float32)
    for hh in range(N_HEADS):
        ek[np.arange(A_ROPE), hh, A_NOPE + np.arange(A_ROPE)] = 1.0
    c['ek'] = jnp.asarray(ek.reshape(A_ROPE, 512), BF16)
    wuv = P['a_wuv'][l].reshape(A_KVRANK, N_HEADS, HEAD_W)
    zv = jnp.zeros((A_KVRANK, HEAD_W), F32)
    c['wv'] = jnp.concatenate(
        [jnp.concatenate([wuv[:, hh], zv] if hh % 2 == 0 else [zv, wuv[:, hh]], axis=-1)
         for hh in range(N_HEADS)], axis=-1).astype(BF16)
    c['gout'] = row(P['a_gout'][l])
    c['mu'] = row(P['b_mu'][l])
    c['w0'] = row(P['b_w0'][l])
    z64 = jnp.zeros((64, GROUP_W), F32)
    c['wl'] = jnp.concatenate([jnp.concatenate([P['b_w2'][l], z64], axis=1),
                               jnp.concatenate([z64, P['b_a2'][l]], axis=1)], axis=0).astype(BF16)
    c['a0'] = row(P['b_a0'][l])
    c['g2b'] = P['b_g2'][l].astype(BF16)
    c['kk'] = row(P['b_kk'][l])
    c['ka'] = row(P['b_ka'][l])
    c['rk'] = row(P['b_rk'][l])
    c['gnw'] = row(P['b_gnw'][l])
    c['gnb'] = row(P['b_gnb'][l])
    c['c_cw'] = _pad_rows(P['c_convw'][l][:, _IDX['xbc_exp']])
    c['c_cb'] = row(P['c_convb'][l][_IDX['xbc_exp']])
    c['c_dtb'] = row(_rep_heads(P['c_dtb'][l]))
    c['c_alog'] = row(_rep_heads(P['c_alog'][l]))
    c['c_d'] = row(_rep_heads(P['c_d'][l]))
    c['c_gn'] = row(P['c_gnorm'][l])
    c['d_cw'] = _pad_rows(P['d_convw'][l])
    c['d_alog'] = row(_rep_heads(P['d_alog'][l]))
    c['d_dtb'] = row(_rep_heads(P['d_dtb'][l]))
    c['d_gn'] = row(jnp.tile(P['d_gnorm'][l], N_HEADS))
    c['wo'] = P['w_out'][l].astype(BF16)
    c['g2'] = row(P['norm2_g'][l])
    c['wup'] = P['f_wup'][l][:, _IDX['ffn_perm']].astype(BF16)
    c['f_cw'] = _pad_rows(P['f_convw'][l][:, _IDX['ffn_perm']])
    c['wdn'] = P['f_wdown'][l].astype(BF16)
    return c


def _embed_bd(s):
    b = s.shape[0]
    eye = jnp.eye(N_HEADS, dtype=s.dtype)
    return jnp.einsum('bhij,hg->bhigj', s, eye).reshape(b, GROUP_W, GROUP_W)


def _extract_bd(s):
    b = s.shape[0]
    s5 = s.reshape(b, N_HEADS, HEAD_W, N_HEADS, HEAD_W)
    return jnp.stack([s5[:, hh, :, hh, :] for hh in range(N_HEADS)], axis=1)


def _hist8(hist):
    return jnp.pad(hist, ((0, 0), (SUBLANE - hist.shape[1], 0), (0, 0)))


def _rope_table(pos):
    half = A_ROPE // 2
    inv = jnp.power(ROPE_BASE, -jnp.arange(half, dtype=F32) / half)
    ang = pos.astype(F32)[:, None] * inv
    cos, sin = jnp.cos(ang), jnp.sin(ang)
    cos2 = jnp.concatenate([cos, cos], axis=-1)
    sin2 = jnp.concatenate([-sin, sin], axis=-1)
    n = pos.shape[0]
    one = jnp.ones((n, A_NOPE), F32)
    z = lambda w: jnp.zeros((n, w), F32)
    return jnp.concatenate([one, cos2, z(32), z(64), sin2, z(32), cos2, z(96), sin2, z(96)], axis=-1)


def _trunk(x, pos, front, st, P, *, tm, c, tq, causal, n_keys_pad):
    b, l, _ = x.shape
    tab = _rope_table(pos)
    new = {name: [] for name in ('ckv', 'krope', 'rwkv_S', 'rwkv_shift', 'ssd_S', 'ssd_conv',
                                 'gdn_S', 'gdn_conv', 'ffn_conv')}
    zeros_bd = jnp.zeros((b, GROUP_W, GROUP_W), F32)
    for li in range(DEPTH):
        c_ = P[li]
        q, ckv, krope = _mla_prep(x, c_['g1'], c_['wa'], c_['gq'], c_['wq1'], c_['wq2'], c_['gkv'],
                                  tab, tm, front)
        if st is None:
            c_all, kr_all, klo, khi = ckv, krope, front, l
            s_b = s_c = s_d = zeros_bd
            shift8 = jnp.zeros((b, SUBLANE, B_COLS), F32)
            chist = jnp.zeros((b, SUBLANE, 768), F32)
            dhist = jnp.zeros((b, SUBLANE, 768), F32)
            fhist = jnp.zeros((b, SUBLANE, 2 * D_FF), F32)
        else:
            past = st['ckv'].shape[2]
            padk = n_keys_pad - past - l
            c_all = jnp.concatenate([st['ckv'][li], ckv, jnp.zeros((b, padk, A_KVRANK), F32)], axis=1)
            kr_all = jnp.concatenate([st['krope'][li], krope, jnp.zeros((b, padk, A_ROPE), F32)], axis=1)
            klo, khi = 0, past + l
            s_b = _embed_bd(st['rwkv_S'][li])
            s_c = _embed_bd(jnp.swapaxes(st['ssd_S'][li], -1, -2))
            s_d = _embed_bd(st['gdn_S'][li])
            shift8 = _hist8(st['rwkv_shift'][li][:, None, :])
            chist = _hist8(st['ssd_conv'][li][:, :, _IDX['xbc_exp']])
            dhist = _hist8(st['gdn_conv'][li])
            fhist = _hist8(st['ffn_conv'][li][:, :, _IDX['ffn_perm']])
        nk = c_all.shape[1]
        tkv = tq if causal else nk
        kf, vf = _kv_up(c_all, kr_all, c_['wk'], c_['ek'], c_['wv'], tkv)
        ya = _flash(q, kf, vf, c_['gout'], tq, tkv, causal, klo, khi)
        yb, sb_new, shift_new = _scan_call(
            _rwkv_kernel, "rwkv7", x, (c_['g1'], c_['wb']), shift8, s_b,
            (c_['mu'], c_['w0'], c_['wl'], c_['a0'], c_['g2b'], c_['kk'], c_['ka'], c_['rk'],
             c_['gnw'], c_['gnb']), B_COLS, tm, c, front)
        yc, sc_new, chist_new = _scan_call(
            _ssd_kernel, "ssd", x, (c_['g1'], c_['wc']), chist, s_c,
            (c_['c_cw'], c_['c_cb'], c_['c_dtb'], c_['c_alog'], c_['c_d'], c_['c_gn']),
            768, tm, c, front)
        yd, sd_new, dhist_new = _scan_call(
            _gdn_kernel, "gdn", x, (c_['g1'], c_['wd']), dhist, s_d,
            (c_['d_cw'], c_['d_alog'], c_['d_dtb'], c_['d_gn']), 768, tm, c, front)
        x, fhist_new = _ffn(x, ya, yb, yc, yd, c_['wo'], c_['g2'], c_['wup'], c_['f_cw'], c_['wdn'],
                            fhist, tm, front)
        new['ckv'].append(ckv[:, front:])
        new['krope'].append(krope[:, front:])
        new['rwkv_S'].append(_extract_bd(sb_new))
        new['rwkv_shift'].append(shift_new[:, SUBLANE - 1])
        new['ssd_S'].append(jnp.swapaxes(_extract_bd(sc_new), -1, -2))
        new['ssd_conv'].append(chist_new[:, SUBLANE - (C_CONV - 1):][:, :, _IDX['xbc_back']])
        new['gdn_S'].append(_extract_bd(sd_new))
        new['gdn_conv'].append(dhist_new[:, SUBLANE - (D_CONV - 1):])
        new['ffn_conv'].append(fhist_new[:, SUBLANE - (FFN_CONV - 1):][:, :, _IDX['ffn_back']])
    return x, {name: jnp.stack(vals) for name, vals in new.items()}


def kernel(x_prompt, x_sample, cache_mla_ckv, cache_mla_krope, state_rwkv, state_rwkv_shift, state_ssd, state_ssd_conv, state_gdn, state_gdn_conv, state_ffn_conv, meta_tokens, norm1_g, w_in, a_gq, a_wuq, a_gkv, a_wuk, a_wuv, a_gout, b_mu, b_w0, b_w2, b_a0, b_a2, b_g2, b_kk, b_ka, b_rk, b_gnw, b_gnb, c_convw, c_convb, c_dtb, c_alog, c_d, c_gnorm, d_convw, d_alog, d_dtb, d_gnorm, w_out, norm2_g, f_wup, f_convw, f_wdown, final_g):
    P = dict(norm1_g=norm1_g, w_in=w_in, a_gq=a_gq, a_wuq=a_wuq, a_gkv=a_gkv, a_wuk=a_wuk,
             a_wuv=a_wuv, a_gout=a_gout, b_mu=b_mu, b_w0=b_w0, b_w2=b_w2, b_a0=b_a0, b_a2=b_a2,
             b_g2=b_g2, b_kk=b_kk, b_ka=b_ka, b_rk=b_rk, b_gnw=b_gnw, b_gnb=b_gnb,
             c_convw=c_convw, c_convb=c_convb, c_dtb=c_dtb, c_alog=c_alog, c_d=c_d, c_gnorm=c_gnorm,
             d_convw=d_convw, d_alog=d_alog, d_dtb=d_dtb, d_gnorm=d_gnorm,
             w_out=w_out, norm2_g=norm2_g, f_wup=f_wup, f_convw=f_convw, f_wdown=f_wdown)
    P = [_layer_consts(P, li) for li in range(DEPTH)]
    fin = final_g.reshape(1, -1).astype(F32)
    b_p, seq, _ = x_prompt.shape
    lx = N_META + seq
    assert seq % ATT_BLOCK == 0 and N_META <= ATT_BLOCK
    front = ATT_BLOCK - N_META
    l_pad = front + lx
    meta = jnp.broadcast_to(meta_tokens[None].astype(F32), (b_p, N_META, D_MODEL))
    x_ext = jnp.concatenate([jnp.zeros((b_p, front, D_MODEL), F32), meta, x_prompt], axis=1)
    pos_p = jnp.arange(l_pad, dtype=jnp.int32) - (front + N_META)
    y_p, ns_p = _trunk(x_ext, pos_p, front, None, P, tm=ROW_TILE, c=CHUNK, tq=ATT_BLOCK, causal=True,
                       n_keys_pad=l_pad)
    y_prompt = _final_norm(y_p, fin, ROW_TILE, (front + N_META) // ROW_TILE, seq)
    b_s, t_s, _ = x_sample.shape
    past = cache_mla_ckv.shape[2]
    assert t_s <= CHUNK and t_s % 16 == 0 and (t_s & (t_s - 1)) == 0
    st_s = dict(ckv=cache_mla_ckv, krope=cache_mla_krope, rwkv_S=state_rwkv, rwkv_shift=state_rwkv_shift,
                ssd_S=state_ssd, ssd_conv=state_ssd_conv, gdn_S=state_gdn, gdn_conv=state_gdn_conv,
                ffn_conv=state_ffn_conv)
    pos_s = past + jnp.arange(t_s, dtype=jnp.int32)
    n_keys_pad = -(-(past + t_s) // LANE) * LANE
    y_s, ns_s = _trunk(x_sample, pos_s, 0, st_s, P, tm=t_s, c=t_s, tq=t_s, causal=False,
                       n_keys_pad=n_keys_pad)
    y_sample = _final_norm(y_s, fin, t_s, 0, t_s)
    keys = ('ckv', 'krope', 'rwkv_S', 'rwkv_shift', 'ssd_S', 'ssd_conv', 'gdn_S', 'gdn_conv', 'ffn_conv')
    return (y_prompt, y_sample) + tuple(ns_p[k] for k in keys) + tuple(ns_s[k] for k in keys)
```

```python
import functools
import math

import numpy as np
import jax
import jax.numpy as jnp
from jax import lax
from jax.experimental import pallas as pl
from jax.experimental.pallas import tpu as pltpu

F32 = jnp.float32
BF16 = jnp.bfloat16

D_MODEL = 1024
DEPTH = 4
CHUNK = 64
N_META = 16
EPS = 1e-6
L2_EPS = 1e-6
GROUP_W = 256
N_HEADS = 4
HEAD_W = 64
A_NOPE = 64
A_ROPE = 32
A_QRANK = 192
A_KVRANK = 128
A_SCALE = (A_NOPE + A_ROPE) ** -0.5
ROPE_BASE = 10000.0
B_GN_EPS = 64e-5
B_COLS = 1024
C_CONV = 4
D_CONV = 4
D_FF = 2816
FFN_CONV = 3
A_COLS = 352
C_COLS = 772
D_COLS = 1032
IN_COLS = A_COLS + B_COLS + C_COLS + D_COLS

LANE = 128
SUBLANE = 8
VMEM_LIMIT = 56 * 1024 * 1024
NEG = -1e30
LOG2E = math.log2(math.e)
FFN_CW = 256
ATT_BLOCK = 256
ROW_TILE = 256


def _mm(a, b):
    return jnp.dot(a.astype(BF16), b.astype(BF16), preferred_element_type=F32)


def _mm_nt(a, b):
    return lax.dot_general(a.astype(BF16), b.astype(BF16), (((1,), (1,)), ((), ())),
                           preferred_element_type=F32)


def _mm_tn(a, b):
    return lax.dot_general(a.astype(BF16), b.astype(BF16), (((0,), (0,)), ((), ())),
                           preferred_element_type=F32)


def _split2(x):
    hi = x.astype(BF16)
    lo = (x - hi.astype(F32)).astype(BF16)
    return hi, lo


def _mm_x2(x, w):
    hi, lo = _split2(x)
    return (jnp.dot(hi, w, preferred_element_type=F32)
            + jnp.dot(lo, w, preferred_element_type=F32))


def _rms(x, g, n):
    ms = jnp.sum(x * x, axis=-1, keepdims=True) * (1.0 / n)
    return x * lax.rsqrt(ms + EPS) * g


def _softplus(x):
    return jnp.maximum(x, 0.0) + jnp.log1p(jnp.exp(-jnp.abs(x)))


def _sigmoid(x):
    return 1.0 / (1.0 + jnp.exp(-x))


def _silu(x):
    return x * _sigmoid(x)


def _iota(shape, axis):
    return lax.broadcasted_iota(jnp.int32, shape, axis)


def _norm_in(x_ref, g_ref, j, tm, front):
    h = _rms(x_ref[0], g_ref[...], D_MODEL)
    if front > 0:
        rows = j * tm + _iota((tm, 1), 0)
        h = jnp.where(rows >= front, h, 0.0)
    return h.astype(BF16)


def _head_ones():
    r = _iota((GROUP_W, GROUP_W), 0)
    c = _iota((GROUP_W, GROUP_W), 1)
    return jnp.where((r >> 6) == (c >> 6), 1.0, 0.0).astype(BF16)


def _hsum(x, bd):
    return _mm_x2(x, bd)


class _Stk:
    def __init__(self, c):
        self.c = c
        n = N_HEADS * c
        self.n = n
        lc = int(math.log2(c))
        assert 1 << lc == c
        self.steps = lc
        r = _iota((n, GROUP_W), 0)
        l = _iota((n, GROUP_W), 1)
        self.hm = (r >> lc) == (l >> 6)
        rr = _iota((n, n), 0)
        cc = _iota((n, n), 1)
        self.rr, self.cc = rr, cc
        same = (rr >> lc) == (cc >> lc)
        self.incl = same & (cc <= rr)
        self.strict = same & (cc < rr)
        tr = _iota((c, c), 0)
        tc = _iota((c, c), 1)
        self.tl = jnp.where(tc <= tr, 1.0, 0.0).astype(BF16)

    def tile(self, x):
        return jnp.concatenate([x] * N_HEADS, axis=0)

    def stack(self, x):
        return jnp.where(self.hm, self.tile(x), 0.0)

    def unstack(self, xs):
        c = self.c
        return xs[0:c] + xs[c:2 * c] + xs[2 * c:3 * c] + xs[3 * c:4 * c]

    def cumsum(self, x):
        hi, lo = _split2(x)
        return (jnp.dot(self.tl, hi, preferred_element_type=F32)
                + jnp.dot(self.tl, lo, preferred_element_type=F32))

    def decay(self, g_cum):
        n = self.n
        gs = self.stack(g_cum)
        hi, lo = _split2(gs)
        w = 1.0 / HEAD_W
        o1 = jnp.full((GROUP_W, n), w, BF16)
        o2 = jnp.full((n, GROUP_W), w, BF16)
        gc = jnp.dot(hi, o1, preferred_element_type=F32) + jnp.dot(lo, o1, preferred_element_type=F32)
        gr = _mm_nt(o2, hi) + _mm_nt(o2, lo)
        return jnp.exp(jnp.where(self.incl, gc - gr, NEG))

    def tri_inv(self, lower):
        minv = jnp.where(self.rr == self.cc, 1.0, 0.0)
        for lb in range(self.steps):
            join = (((self.rr >> (lb + 1)) == (self.cc >> (lb + 1)))
                    & (((self.rr >> lb) & 1) == 1) & (((self.cc >> lb) & 1) == 0))
            moff = jnp.where(join, lower, 0.0)
            minv = minv - _mm(_mm(minv, moff), minv)
        return minv


def _mla_prep_kernel(x_ref, g1_ref, wa_ref, gq_ref, wq1t_ref, wq2t_ref, gkv_ref, tabq_ref, tabk_ref,
                     qt_out, ckv_out, kr_out, *, tm, front):
    j = pl.program_id(1)
    h = _norm_in(x_ref, g1_ref, j, tm, front)
    pa = jnp.dot(h, wa_ref[...], preferred_element_type=F32)
    qn = _rms(pa[:, 0:256], gq_ref[...], A_QRANK)
    q1t = _mm_nt(wq1t_ref[...], qn)
    q2t = _mm_nt(wq2t_ref[...], qn)
    tabq = tabq_ref[...]
    cos4 = jnp.concatenate([tabq[0:128]] * N_HEADS, axis=0)
    sin4 = jnp.concatenate([tabq[128:256]] * N_HEADS, axis=0)
    qt_out[0] = ((q1t * cos4 + q2t * sin4) * (A_SCALE * LOG2E)).astype(BF16)
    ckv_out[0] = _rms(pa[:, 256:384], gkv_ref[...], A_KVRANK)
    tabk = tabk_ref[...]
    kr = pa[:, 384:512] * tabk[:, 0:128] + pa[:, 512:640] * tabk[:, 128:256]
    kr_out[0] = kr[:, 0:A_ROPE]


def _kv_up_kernel(c_ref, kr_ref, wk_ref, ek_ref, wvt_ref, ones_ref, k_out, vt_out):
    c = c_ref[0].astype(BF16)
    kr = kr_ref[0].astype(BF16)
    k = (jnp.dot(c, wk_ref[...], preferred_element_type=F32)
         + jnp.dot(kr, ek_ref[...], preferred_element_type=F32))
    k_out[0] = k.astype(BF16)
    vt_out[0, 0] = (_mm_nt(wvt_ref[...], c) + ones_ref[...]).astype(BF16)


def _flash_kernel(qt_ref, k_ref, vt_ref, gout_ref, o_ref, m_sc, acc_sc, sa_sc, sb_sc,
                  *, tq, tk, nkv, causal, klo, khi):
    i = pl.program_id(1)
    m_sc[...] = jnp.full((N_HEADS * SUBLANE, tq), NEG, F32)
    acc_sc[...] = jnp.zeros((N_HEADS * LANE, tq), F32)
    heads = [slice(LANE * h, LANE * (h + 1)) for h in range(N_HEADS)]

    s_bufs = (sa_sc, sb_sc)

    def produce(jb, slot):
        start = pl.multiple_of(jb * tk, tk)
        for h, rows in enumerate(heads):
            s_bufs[slot][h * tk:(h + 1) * tk, :] = jnp.dot(
                k_ref[0, pl.ds(start, tk), rows], qt_ref[0, rows, :], preferred_element_type=F32)

    def consume(jb, slot, masked, nxt):
        if nxt is not None:
            produce(*nxt)
        if masked:
            kpos = jb * tk + _iota((tk, 1), 0)
            qpos = i * tq + _iota((1, tq), 1)
            vis = (kpos >= klo) & (kpos < khi)
            if causal:
                vis = vis & ((kpos >> 6) <= (qpos >> 6))
        m_all = m_sc[...]
        acc_all = acc_sc[...]
        m_out, alphas, ps = [], [], []
        for h in range(N_HEADS):
            s = s_bufs[slot][h * tk:(h + 1) * tk, :]
            if masked:
                s = jnp.where(vis, s, NEG)
            m_prev = m_all[SUBLANE * h:SUBLANE * (h + 1)]
            m_new = jnp.maximum(m_prev, jnp.max(s, axis=0, keepdims=True))
            alphas.append(jnp.exp2(m_prev[0:1] - m_new[0:1]))
            ps.append(jnp.exp2(s - m_new[0:1]).astype(BF16))
            m_out.append(m_new)
        acc_out = [alphas[h] * acc_all[rows]
                   + jnp.dot(vt_ref[0, jb, rows, :], ps[h], preferred_element_type=F32)
                   for h, rows in enumerate(heads)]
        m_sc[...] = jnp.concatenate(m_out, axis=0)
        acc_sc[...] = jnp.concatenate(acc_out, axis=0)

    produce(0, 0)
    if causal:
        @pl.when(i == 0)
        def _():
            consume(0, 0, True, None)

        @pl.when(i > 0)
        def _():
            consume(0, 0, True, (1, 1))
            pairs = (i - 1) >> 1

            def body(t, carry):
                jb = 1 + 2 * t
                consume(jb, 1, False, (jb + 1, 0))
                consume(jb + 1, 0, False, (jb + 2, 1))
                return carry

            lax.fori_loop(0, pairs, body, 0)

            @pl.when((i & 1) == 0)
            def _():
                consume(i - 1, 1, False, (i, 0))
                consume(i, 0, True, None)

            @pl.when((i & 1) == 1)
            def _():
                consume(i, 1, True, None)
    else:
        for jb in range(nkv):
            consume(jb, jb & 1, True, (jb + 1, (jb + 1) & 1) if jb + 1 < nkv else None)
    outs = []
    for h in range(N_HEADS):
        a = acc_sc[LANE * h:LANE * (h + 1), :]
        outs.append(a[0:HEAD_W] / a[HEAD_W:HEAD_W + 1])
    yat = jnp.concatenate(outs, axis=0)
    ms = jnp.sum(yat * yat, axis=0, keepdims=True) * (1.0 / GROUP_W)
    o_ref[0] = yat * lax.rsqrt(ms + EPS) * gout_ref[...]


def _rwkv_kernel(x_ref, g1_ref, wb_ref, shift_ref, s0_ref, mu_ref, w0_ref, wl_ref, a0_ref,
                 g2_ref, kk_ref, ka_ref, rk_ref, gnw_ref, gnb_ref,
                 y_out, s_out, shift_out, work, st, *, tm, c, front):
    j = pl.program_id(1)

    @pl.when(j == 0)
    def _():
        work[0:SUBLANE, :] = shift_ref[0]
        st[...] = s0_ref[0]

    h = _norm_in(x_ref, g1_ref, j, tm, front)
    cols = jnp.dot(h, wb_ref[...], preferred_element_type=F32)
    work[SUBLANE:SUBLANE + tm, :] = cols
    shifted = work[pl.ds(SUBLANE - 1, tm), :]
    tail = work[tm:tm + SUBLANE, :]
    work[0:SUBLANE, :] = tail
    shift_out[0] = tail
    xm = cols + (shifted - cols) * mu_ref[...]
    r = xm[:, 0:256]
    k = xm[:, 256:512]
    v = xm[:, 512:768]
    lora = xm[:, 768:896]
    dg = xm[:, 896:1024]
    lora = jnp.where(_iota((tm, LANE), 1) < 64, jnp.tanh(lora), lora)
    ll = _mm(lora, wl_ref[...])
    w_log = -_softplus(-(w0_ref[...] + ll[:, 0:256])) - 0.5
    logd = -jnp.exp(w_log)
    a = _sigmoid(a0_ref[...] + ll[:, 256:512])
    g = _mm(_sigmoid(dg), g2_ref[...])
    bd = _head_ones()
    kkr = k * kk_ref[...]
    kk = kkr * lax.rsqrt(_hsum(kkr * kkr, bd) + L2_EPS)
    k2 = k * (1.0 + (a - 1.0) * ka_ref[...])

    sk = _Stk(c)
    outs = []
    s = st[...]
    for ci in range(tm // c):
        rows = slice(ci * c, (ci + 1) * c)
        ld = logd[rows]
        gc = sk.cumsum(ld)
        eg = jnp.exp(gc)
        eng = jnp.exp(-gc)
        at = -kk[rows] * jnp.exp(gc - ld)
        bt = kk[rows] * a[rows] * eng
        kt = k2[rows] * eng
        rt = r[rows] * eg
        dc = eg[c - 1:c, :]
        at_s = sk.stack(at)
        rt_s = sk.stack(rt)
        v_s = sk.stack(v[rows])
        bt_t = sk.tile(bt)
        kt_t = sk.tile(kt)
        lab = jnp.where(sk.strict, _mm_nt(at_s, bt_t), 0.0)
        aak = jnp.where(sk.strict, _mm_nt(at_s, kt_t), 0.0)
        arb = jnp.where(sk.incl, _mm_nt(rt_s, bt_t), 0.0)
        ark = jnp.where(sk.incl, _mm_nt(rt_s, kt_t), 0.0)
        sb = s.astype(BF16)
        u = _mm(sk.tri_inv(-lab), _mm_nt(at_s, sb) + _mm(aak, v_s))
        o_s = _mm_nt(rt_s, sb) + _mm(arb, u) + _mm(ark, v_s)
        s = s * dc + _mm_tn(u, sk.stack(bt * dc)) + _mm_tn(v_s, sk.stack(kt * dc))
        outs.append(sk.unstack(o_s))
    st[...] = s
    s_out[0] = s
    o = jnp.concatenate(outs, axis=0) if len(outs) > 1 else outs[0]
    mean = _hsum(o, bd) * (1.0 / HEAD_W)
    d = o - mean
    var = _hsum(d * d, bd) * (1.0 / HEAD_W)
    o = d * lax.rsqrt(var + B_GN_EPS) * gnw_ref[...] + gnb_ref[...]
    bonus = _hsum(r * k2 * rk_ref[...], bd) * v
    y_out[0] = (o + bonus) * g


def _conv4(work, wv, tm):
    y = work[pl.ds(SUBLANE - 3, tm), :] * wv[0:1, :]
    y = y + work[pl.ds(SUBLANE - 2, tm), :] * wv[1:2, :]
    y = y + work[pl.ds(SUBLANE - 1, tm), :] * wv[2:3, :]
    return y + work[pl.ds(SUBLANE, tm), :] * wv[3:4, :]


def _ssd_kernel(x_ref, g1_ref, wc_ref, hist_ref, s0_ref, cw_ref, cb_ref, dtb_ref, alog_ref,
                dskip_ref, gn_ref, y_out, s_out, hist_out, work, st, *, tm, c, front):
    j = pl.program_id(1)

    @pl.when(j == 0)
    def _():
        work[0:SUBLANE, :] = hist_ref[0]
        st[...] = s0_ref[0]

    h = _norm_in(x_ref, g1_ref, j, tm, front)
    pc = jnp.dot(h, wc_ref[...], preferred_element_type=F32)
    z = pc[:, 0:256]
    work[SUBLANE:SUBLANE + tm, :] = pc[:, 256:1024]
    xbc = _silu(_conv4(work, cw_ref[...], tm) + cb_ref[...])
    tail = work[tm:tm + SUBLANE, :]
    work[0:SUBLANE, :] = tail
    hist_out[0] = tail
    xs = xbc[:, 0:256]
    bm = xbc[:, 256:512]
    cm = xbc[:, 512:768]
    dt = _softplus(pc[:, 1024:1280] + dtb_ref[...])
    if front > 0:
        rows_i = j * tm + _iota((tm, 1), 0)
        dt = jnp.where(rows_i >= front, dt, 0.0)
    a = dt * (-jnp.exp(alog_ref[...]))
    xdt = xs * dt

    sk = _Stk(c)
    outs = []
    s = st[...]
    for ci in range(tm // c):
        rows = slice(ci * c, (ci + 1) * c)
        ac = sk.cumsum(a[rows])
        al = ac[c - 1:c, :]
        dm = sk.decay(ac)
        amat = _mm_nt(sk.stack(cm[rows]), sk.tile(bm[rows])) * dm
        xdt_s = sk.stack(xdt[rows])
        y_s = _mm(amat, xdt_s) + _mm(sk.stack(cm[rows] * jnp.exp(ac)), s)
        s = s * jnp.exp(al) + _mm_tn(sk.stack(bm[rows] * jnp.exp(al - ac)), xdt_s)
        outs.append(sk.unstack(y_s))
    st[...] = s
    s_out[0] = s
    y = jnp.concatenate(outs, axis=0) if len(outs) > 1 else outs[0]
    y = y + dskip_ref[...] * xs
    y_out[0] = _rms(y * _silu(z), gn_ref[...], GROUP_W)


def _gdn_kernel(x_ref, g1_ref, wd_ref, hist_ref, s0_ref, cw_ref, alog_ref, dtb_ref, gn_ref,
                y_out, s_out, hist_out, work, st, *, tm, c, front):
    j = pl.program_id(1)

    @pl.when(j == 0)
    def _():
        work[0:SUBLANE, :] = hist_ref[0]
        st[...] = s0_ref[0]

    h = _norm_in(x_ref, g1_ref, j, tm, front)
    pd = jnp.dot(h, wd_ref[...], preferred_element_type=F32)
    work[SUBLANE:SUBLANE + tm, :] = pd[:, 0:768]
    qkv = _silu(_conv4(work, cw_ref[...], tm))
    tail = work[tm:tm + SUBLANE, :]
    work[0:SUBLANE, :] = tail
    hist_out[0] = tail
    z = pd[:, 768:1024]
    beta = _sigmoid(pd[:, 1024:1280])
    g = -jnp.exp(alog_ref[...]) * _softplus(pd[:, 1280:1536] + dtb_ref[...])
    bd = _head_ones()
    q = qkv[:, 0:256]
    k = qkv[:, 256:512]
    v = qkv[:, 512:768]
    q = q * lax.rsqrt(_hsum(q * q, bd) + L2_EPS) * (HEAD_W ** -0.5)
    k = k * lax.rsqrt(_hsum(k * k, bd) + L2_EPS)

    sk = _Stk(c)
    outs = []
    s = st[...]
    for ci in range(tm // c):
        rows = slice(ci * c, (ci + 1) * c)
        gc = sk.cumsum(g[rows])
        eg = jnp.exp(gc)
        gl = gc[c - 1:c, :]
        kc = k[rows]
        kb = kc * beta[rows]
        k_t = sk.tile(kc)
        dm = sk.decay(gc)
        lower = jnp.where(sk.strict, _mm_nt(sk.stack(kb), k_t) * dm, 0.0)
        aqk = _mm_nt(sk.stack(q[rows]), k_t) * dm
        sb = s.astype(BF16)
        rhs = sk.stack(v[rows] * beta[rows]) - _mm(sk.stack(kb * eg), sb)
        vn = _mm(sk.tri_inv(lower), rhs)
        o_s = _mm(sk.stack(q[rows] * eg), sb) + _mm(aqk, vn)
        s = s * jnp.exp(gl) + _mm_tn(sk.stack(kc * jnp.exp(gl - gc)), vn)
        outs.append(sk.unstack(o_s))
    st[...] = s
    s_out[0] = s
    o = jnp.concatenate(outs, axis=0) if len(outs) > 1 else outs[0]
    ms = _hsum(o * o, bd) * (1.0 / HEAD_W)
    y_out[0] = o * lax.rsqrt(ms + EPS) * gn_ref[...] * _silu(z)


def _ffn_kernel(x_ref, ya_ref, yb_ref, yc_ref, yd_ref, wo_ref, g2_ref, wup_ref, cw_ref, wdn_ref,
                hist_ref, xo_ref, hist_out, carry, work, *, tm, front):
    j = pl.program_id(1)

    @pl.when(j == 0)
    def _():
        carry[...] = hist_ref[0]

    x = x_ref[0] + _mm_tn(ya_ref[0], wo_ref[0:GROUP_W, :])
    for idx, y_ref in ((1, yb_ref), (2, yc_ref), (3, yd_ref)):
        x = x + jnp.dot(y_ref[0].astype(BF16), wo_ref[GROUP_W * idx:GROUP_W * (idx + 1), :],
                        preferred_element_type=F32)
    h2 = _rms(x, g2_ref[...], D_MODEL)
    if front > 0:
        rows = j * tm + _iota((tm, 1), 0)
        h2 = jnp.where(rows >= front, h2, 0.0)
    h2 = h2.astype(BF16)
    acc = jnp.zeros((tm, D_MODEL), F32)
    w2 = 2 * FFN_CW
    for f in range(D_FF // FFN_CW):
        cols = slice(f * w2, (f + 1) * w2)
        u = jnp.dot(h2, wup_ref[:, cols], preferred_element_type=F32)
        work[0:SUBLANE, :] = carry[:, cols]
        work[SUBLANE:SUBLANE + tm, :] = u
        cw = cw_ref[:, cols]
        y = (work[pl.ds(SUBLANE - 2, tm), :] * cw[0:1, :]
             + work[pl.ds(SUBLANE - 1, tm), :] * cw[1:2, :] + u * cw[2:3, :])
        carry[:, cols] = work[tm:tm + SUBLANE, :]
        act = _silu(y[:, 0:FFN_CW]) * y[:, FFN_CW:w2]
        acc = acc + jnp.dot(act.astype(BF16), wdn_ref[f * FFN_CW:(f + 1) * FFN_CW, :],
                            preferred_element_type=F32)
    xo_ref[0] = x + acc
    hist_out[0] = carry[...]


def _final_norm_kernel(x_ref, g_ref, o_ref):
    o_ref[0] = _rms(x_ref[0], g_ref[...], D_MODEL)


def _const_spec(arr):
    nd = arr.ndim
    return pl.BlockSpec(arr.shape, lambda b, j: (0,) * nd, pipeline_mode=pl.Buffered(1))


def _tile_spec(tm, width):
    return pl.BlockSpec((1, tm, width), lambda b, j: (b, j, 0))


def _batch_spec(rows, width):
    return pl.BlockSpec((1, rows, width), lambda b, j: (b, 0, 0))


def _params():
    return pltpu.CompilerParams(dimension_semantics=("arbitrary", "arbitrary"),
                                vmem_limit_bytes=VMEM_LIMIT)


def _cols_spec(rows, tm):
    return pl.BlockSpec((1, rows, tm), lambda b, j: (b, 0, j))


def _mla_prep(x, g1, wa, gq, wq1t, wq2t, gkv, tabq, tabk, tm, front):
    b, l, _ = x.shape
    consts = (g1, wa, gq, wq1t, wq2t, gkv)
    return pl.pallas_call(
        functools.partial(_mla_prep_kernel, tm=tm, front=front),
        grid=(b, l // tm),
        in_specs=[_tile_spec(tm, D_MODEL)] + [_const_spec(a) for a in consts]
        + [pl.BlockSpec((2 * LANE, tm), lambda bb, j: (0, j)),
           pl.BlockSpec((tm, 2 * LANE), lambda bb, j: (j, 0))],
        out_specs=[_cols_spec(512, tm), _tile_spec(tm, A_KVRANK), _tile_spec(tm, A_ROPE)],
        out_shape=[jax.ShapeDtypeStruct((b, 512, l), BF16),
                   jax.ShapeDtypeStruct((b, l, A_KVRANK), F32),
                   jax.ShapeDtypeStruct((b, l, A_ROPE), F32)],
        compiler_params=_params(), name="mla_prep",
    )(x, *consts, tabq, tabk)


def _kv_up(c_all, kr_all, wk, ek, wvt, ones_col, tm):
    b, n, _ = c_all.shape
    consts = (wk, ek, wvt, ones_col)
    return pl.pallas_call(
        _kv_up_kernel,
        grid=(b, n // tm),
        in_specs=[_tile_spec(tm, A_KVRANK), _tile_spec(tm, A_ROPE)] + [_const_spec(a) for a in consts],
        out_specs=[_tile_spec(tm, 512), pl.BlockSpec((1, 1, 512, tm), lambda bb, j: (bb, j, 0, 0))],
        out_shape=[jax.ShapeDtypeStruct((b, n, 512), BF16),
                   jax.ShapeDtypeStruct((b, n // tm, 512, tm), BF16)],
        compiler_params=_params(), name="kv_up",
    )(c_all, kr_all, *consts)


def _flash(qt, k, vt, gout_col, tq, tk, causal, klo, khi):
    b, _, l = qt.shape
    n = k.shape[1]
    nkv = n // tk
    return pl.pallas_call(
        functools.partial(_flash_kernel, tq=tq, tk=tk, nkv=nkv, causal=causal, klo=klo, khi=khi),
        grid=(b, l // tq),
        in_specs=[_cols_spec(512, tq),
                  pl.BlockSpec((1, n, 512), lambda bb, j: (bb, 0, 0), pipeline_mode=pl.Buffered(1)),
                  pl.BlockSpec((1, nkv, 512, tk), lambda bb, j: (bb, 0, 0, 0), pipeline_mode=pl.Buffered(1)),
                  _const_spec(gout_col)],
        out_specs=_cols_spec(GROUP_W, tq),
        out_shape=jax.ShapeDtypeStruct((b, GROUP_W, l), F32),
        scratch_shapes=[pltpu.VMEM((N_HEADS * SUBLANE, tq), F32), pltpu.VMEM((N_HEADS * LANE, tq), F32),
                        pltpu.VMEM((N_HEADS * tk, tq), F32), pltpu.VMEM((N_HEADS * tk, tq), F32)],
        compiler_params=_params(), name="mla_flash",
    )(qt, k, vt, gout_col)


def _scan_call(kernel, name, x, consts_a, hist, s0, consts_b, width_in, tm, c, front):
    b, l, _ = x.shape
    return pl.pallas_call(
        functools.partial(kernel, tm=tm, c=c, front=front),
        grid=(b, l // tm),
        in_specs=[_tile_spec(tm, D_MODEL)] + [_const_spec(a) for a in consts_a]
        + [_batch_spec(SUBLANE, width_in), _batch_spec(GROUP_W, GROUP_W)]
        + [_const_spec(a) for a in consts_b],
        out_specs=[_tile_spec(tm, GROUP_W), _batch_spec(GROUP_W, GROUP_W), _batch_spec(SUBLANE, width_in)],
        out_shape=[jax.ShapeDtypeStruct((b, l, GROUP_W), F32),
                   jax.ShapeDtypeStruct((b, GROUP_W, GROUP_W), F32),
                   jax.ShapeDtypeStruct((b, SUBLANE, width_in), F32)],
        scratch_shapes=[pltpu.VMEM((tm + SUBLANE, width_in), F32), pltpu.VMEM((GROUP_W, GROUP_W), F32)],
        compiler_params=_params(), name=name,
    )(x, *consts_a, hist, s0, *consts_b)


def _ffn(x, ya, yb, yc, yd, wo, g2, wup, cw, wdn, hist, tm, front):
    b, l, _ = x.shape
    return pl.pallas_call(
        functools.partial(_ffn_kernel, tm=tm, front=front),
        grid=(b, l // tm),
        in_specs=[_tile_spec(tm, D_MODEL), _cols_spec(GROUP_W, tm)] + [_tile_spec(tm, GROUP_W)] * 3
        + [_const_spec(a) for a in (wo, g2, wup, cw, wdn)] + [_batch_spec(SUBLANE, 2 * D_FF)],
        out_specs=[_tile_spec(tm, D_MODEL), _batch_spec(SUBLANE, 2 * D_FF)],
        out_shape=[jax.ShapeDtypeStruct((b, l, D_MODEL), F32),
                   jax.ShapeDtypeStruct((b, SUBLANE, 2 * D_FF), F32)],
        scratch_shapes=[pltpu.VMEM((SUBLANE, 2 * D_FF), F32),
                        pltpu.VMEM((tm + SUBLANE, 2 * FFN_CW), F32)],
        compiler_params=_params(), name="out_ffn",
    )(x, ya, yb, yc, yd, wo, g2, wup, cw, wdn, hist)


def _final_norm(x, g, tm, skip_tiles, out_rows):
    b = x.shape[0]
    return pl.pallas_call(
        _final_norm_kernel,
        grid=(b, out_rows // tm),
        in_specs=[pl.BlockSpec((1, tm, D_MODEL), lambda bb, j: (bb, j + skip_tiles, 0)), _const_spec(g)],
        out_specs=_tile_spec(tm, D_MODEL),
        out_shape=jax.ShapeDtypeStruct((b, out_rows, D_MODEL), F32),
        compiler_params=_params(), name="final_norm",
    )(x, g)


def _np_idx():
    z = IN_COLS
    zpad = lambda n: [z] * n
    rep = lambda base: [base + i for i in range(N_HEADS) for _ in range(HEAD_W)]
    grp = lambda base: [base + g * HEAD_W + i for g in (0, 0, 1, 1) for i in range(HEAD_W)]
    a = (list(range(0, 192)) + zpad(64) + list(range(192, 320))
         + list(range(320, 352)) + zpad(96)
         + list(range(336, 352)) + list(range(320, 336)) + zpad(96))
    b0 = A_COLS
    bcols = list(range(b0, b0 + B_COLS))
    c0 = b0 + B_COLS
    ccols = (list(range(c0, c0 + 256)) + list(range(c0 + 256, c0 + 512))
             + grp(c0 + 512) + grp(c0 + 640) + rep(c0 + 768))
    d0 = c0 + C_COLS
    dcols = list(range(d0, d0 + 1024)) + rep(d0 + 1024) + rep(d0 + 1028)
    xbc_exp = list(range(256)) + grp(256) + grp(384)
    xbc_back = (list(range(256)) + list(range(256, 320)) + list(range(384, 448))
                + list(range(512, 576)) + list(range(640, 704)))
    ffn_perm = []
    for f in range(D_FF // FFN_CW):
        ffn_perm += list(range(f * FFN_CW, (f + 1) * FFN_CW))
        ffn_perm += list(range(D_FF + f * FFN_CW, D_FF + (f + 1) * FFN_CW))
    ffn_back = np.argsort(np.array(ffn_perm))
    as_i = lambda v: np.asarray(v, np.int32)
    return dict(a=as_i(a), b=as_i(bcols), c=as_i(ccols), d=as_i(dcols), xbc_exp=as_i(xbc_exp),
                xbc_back=as_i(xbc_back), ffn_perm=as_i(ffn_perm), ffn_back=as_i(ffn_back))


_IDX = _np_idx()


def _rep_heads(v):
    return jnp.repeat(v, HEAD_W, axis=-1)


def _pad_rows(w, rows=SUBLANE):
    return jnp.pad(w, [(0, rows - w.shape[0])] + [(0, 0)] * (w.ndim - 1))


def _layer_consts(P, l):
    row = lambda v: v.reshape(1, -1).astype(F32)
    w_in = jnp.concatenate([P['w_in'][l], jnp.zeros((D_MODEL, 1), F32)], axis=1)
    c = {}
    c['g1'] = row(P['norm1_g'][l])
    c['wa'] = w_in[:, _IDX['a']].astype(BF16)
    c['wb'] = w_in[:, _IDX['b']].astype(BF16)
    c['wc'] = w_in[:, _IDX['c']].astype(BF16)
    c['wd'] = w_in[:, _IDX['d']].astype(BF16)
    c['gq'] = row(jnp.pad(P['a_gq'][l], (0, 64)))
    wuq = P['a_wuq'][l].reshape(A_QRANK, N_HEADS, A_NOPE + A_ROPE)
    rope = wuq[:, :, A_NOPE:]
    swap = jnp.concatenate([rope[..., 16:], rope[..., :16]], axis=-1)
    zeros = lambda n: jnp.zeros((A_QRANK, N_HEADS, n), F32)
    wq1 = jnp.concatenate([wuq, zeros(32)], axis=-1).reshape(A_QRANK, 512)
    wq2 = jnp.concatenate([zeros(64), swap, zeros(32)], axis=-1).reshape(A_QRANK, 512)
    c['wq1t'] = jnp.pad(wq1, ((0, 64), (0, 0))).T.astype(BF16)
    c['wq2t'] = jnp.pad(wq2, ((0, 64), (0, 0))).T.astype(BF16)
    c['gkv'] = row(P['a_gkv'][l])
    wuk = P['a_wuk'][l].reshape(A_KVRANK, N_HEADS, A_NOPE)
    c['wk'] = jnp.concatenate([wuk, jnp.zeros((A_KVRANK, N_HEADS, 64), F32)], axis=-1
                              ).reshape(A_KVRANK, 512).astype(BF16)
    ek = np.zeros((A_ROPE, N_HEADS, LANE), np.float32)
    for hh in range(N_HEADS):
        ek[np.arange(A_ROPE), hh, A_NOPE + np.arange(A_ROPE)] = 1.0
    c['ek'] = jnp.asarray(ek.reshape(A_ROPE, 512), BF16)
    wuv = P['a_wuv'][l].reshape(A_KVRANK, N_HEADS, HEAD_W)
    c['wvt'] = jnp.concatenate([wuv, jnp.zeros((A_KVRANK, N_HEADS, HEAD_W), F32)], axis=-1
                               ).reshape(A_KVRANK, 512).T.astype(BF16)
    ones = np.zeros((N_HEADS, LANE, 1), np.float32)
    ones[:, HEAD_W, 0] = 1.0
    c['ones_col'] = jnp.asarray(ones.reshape(512, 1))
    c['gout_col'] = P['a_gout'][l].reshape(GROUP_W, 1).astype(F32)
    c['mu'] = row(P['b_mu'][l])
    c['w0'] = row(P['b_w0'][l])
    z64 = jnp.zeros((64, GROUP_W), F32)
    c['wl'] = jnp.concatenate([jnp.concatenate([P['b_w2'][l], z64], axis=1),
                               jnp.concatenate([z64, P['b_a2'][l]], axis=1)], axis=0).astype(BF16)
    c['a0'] = row(P['b_a0'][l])
    c['g2b'] = P['b_g2'][l].astype(BF16)
    c['kk'] = row(P['b_kk'][l])
    c['ka'] = row(P['b_ka'][l])
    c['rk'] = row(P['b_rk'][l])
    c['gnw'] = row(P['b_gnw'][l])
    c['gnb'] = row(P['b_gnb'][l])
    c['c_cw'] = _pad_rows(P['c_convw'][l][:, _IDX['xbc_exp']])
    c['c_cb'] = row(P['c_convb'][l][_IDX['xbc_exp']])
    c['c_dtb'] = row(_rep_heads(P['c_dtb'][l]))
    c['c_alog'] = row(_rep_heads(P['c_alog'][l]))
    c['c_d'] = row(_rep_heads(P['c_d'][l]))
    c['c_gn'] = row(P['c_gnorm'][l])
    c['d_cw'] = _pad_rows(P['d_convw'][l])
    c['d_alog'] = row(_rep_heads(P['d_alog'][l]))
    c['d_dtb'] = row(_rep_heads(P['d_dtb'][l]))
    c['d_gn'] = row(jnp.tile(P['d_gnorm'][l], N_HEADS))
    c['wo'] = P['w_out'][l].astype(BF16)
    c['g2'] = row(P['norm2_g'][l])
    c['wup'] = P['f_wup'][l][:, _IDX['ffn_perm']].astype(BF16)
    c['f_cw'] = _pad_rows(P['f_convw'][l][:, _IDX['ffn_perm']])
    c['wdn'] = P['f_wdown'][l].astype(BF16)
    return c


def _embed_bd(s):
    b = s.shape[0]
    eye = jnp.eye(N_HEADS, dtype=s.dtype)
    return jnp.einsum('bhij,hg->bhigj', s, eye).reshape(b, GROUP_W, GROUP_W)


def _extract_bd(s):
    b = s.shape[0]
    s5 = s.reshape(b, N_HEADS, HEAD_W, N_HEADS, HEAD_W)
    return jnp.stack([s5[:, hh, :, hh, :] for hh in range(N_HEADS)], axis=1)


def _hist8(hist):
    return jnp.pad(hist, ((0, 0), (SUBLANE - hist.shape[1], 0), (0, 0)))


def _rope_table(pos):
    half = A_ROPE // 2
    inv = jnp.power(ROPE_BASE, -jnp.arange(half, dtype=F32) / half)
    ang = pos.astype(F32)[:, None] * inv
    cos, sin = jnp.cos(ang), jnp.sin(ang)
    cos2 = jnp.concatenate([cos, cos], axis=-1)
    sin2 = jnp.concatenate([-sin, sin], axis=-1)
    n = pos.shape[0]
    one = jnp.ones((n, A_NOPE), F32)
    z = lambda w: jnp.zeros((n, w), F32)
    tabq = jnp.concatenate([one, cos2, z(32), z(64), sin2, z(32)], axis=-1).T
    tabk = jnp.concatenate([cos2, z(96), sin2, z(96)], axis=-1)
    return tabq, tabk


def _trunk(x, pos, front, st, P, *, tm, c, tq, causal, n_keys_pad):
    b, l, _ = x.shape
    tabq, tabk = _rope_table(pos)
    new = {name: [] for name in ('ckv', 'krope', 'rwkv_S', 'rwkv_shift', 'ssd_S', 'ssd_conv',
                                 'gdn_S', 'gdn_conv', 'ffn_conv')}
    zeros_bd = jnp.zeros((b, GROUP_W, GROUP_W), F32)
    for li in range(DEPTH):
        c_ = P[li]
        qt, ckv, krope = _mla_prep(x, c_['g1'], c_['wa'], c_['gq'], c_['wq1t'], c_['wq2t'], c_['gkv'],
                                   tabq, tabk, tm, front)
        if st is None:
            c_all, kr_all, klo, khi = ckv, krope, front, l
            s_b = s_c = s_d = zeros_bd
            shift8 = jnp.zeros((b, SUBLANE, B_COLS), F32)
            chist = jnp.zeros((b, SUBLANE, 768), F32)
            dhist = jnp.zeros((b, SUBLANE, 768), F32)
            fhist = jnp.zeros((b, SUBLANE, 2 * D_FF), F32)
        else:
            past = st['ckv'].shape[2]
            padk = n_keys_pad - past - l
            c_all = jnp.concatenate([st['ckv'][li], ckv, jnp.zeros((b, padk, A_KVRANK), F32)], axis=1)
            kr_all = jnp.concatenate([st['krope'][li], krope, jnp.zeros((b, padk, A_ROPE), F32)], axis=1)
            klo, khi = 0, past + l
            s_b = _embed_bd(st['rwkv_S'][li])
            s_c = _embed_bd(jnp.swapaxes(st['ssd_S'][li], -1, -2))
            s_d = _embed_bd(st['gdn_S'][li])
            shift8 = _hist8(st['rwkv_shift'][li][:, None, :])
            chist = _hist8(st['ssd_conv'][li][:, :, _IDX['xbc_exp']])
            dhist = _hist8(st['gdn_conv'][li])
            fhist = _hist8(st['ffn_conv'][li][:, :, _IDX['ffn_perm']])
        nk = c_all.shape[1]
        tkv = tq if causal else nk
        kf, vt = _kv_up(c_all, kr_all, c_['wk'], c_['ek'], c_['wvt'], c_['ones_col'], tkv)
        ya = _flash(qt, kf, vt, c_['gout_col'], tq, tkv, causal, klo, khi)
        yb, sb_new, shift_new = _scan_call(
            _rwkv_kernel, "rwkv7", x, (c_['g1'], c_['wb']), shift8, s_b,
            (c_['mu'], c_['w0'], c_['wl'], c_['a0'], c_['g2b'], c_['kk'], c_['ka'], c_['rk'],
             c_['gnw'], c_['gnb']), B_COLS, tm, c, front)
        yc, sc_new, chist_new = _scan_call(
            _ssd_kernel, "ssd", x, (c_['g1'], c_['wc']), chist, s_c,
            (c_['c_cw'], c_['c_cb'], c_['c_dtb'], c_['c_alog'], c_['c_d'], c_['c_gn']),
            768, tm, c, front)
        yd, sd_new, dhist_new = _scan_call(
            _gdn_kernel, "gdn", x, (c_['g1'], c_['wd']), dhist, s_d,
            (c_['d_cw'], c_['d_alog'], c_['d_dtb'], c_['d_gn']), 768, tm, c, front)
        x, fhist_new = _ffn(x, ya, yb, yc, yd, c_['wo'], c_['g2'], c_['wup'], c_['f_cw'], c_['wdn'],
                            fhist, tm, front)
        new['ckv'].append(ckv[:, front:])
        new['krope'].append(krope[:, front:])
        new['rwkv_S'].append(_extract_bd(sb_new))
        new['rwkv_shift'].append(shift_new[:, SUBLANE - 1])
        new['ssd_S'].append(jnp.swapaxes(_extract_bd(sc_new), -1, -2))
        new['ssd_conv'].append(chist_new[:, SUBLANE - (C_CONV - 1):][:, :, _IDX['xbc_back']])
        new['gdn_S'].append(_extract_bd(sd_new))
        new['gdn_conv'].append(dhist_new[:, SUBLANE - (D_CONV - 1):])
        new['ffn_conv'].append(fhist_new[:, SUBLANE - (FFN_CONV - 1):][:, :, _IDX['ffn_back']])
    return x, {name: jnp.stack(vals) for name, vals in new.items()}


def kernel(x_prompt, x_sample, cache_mla_ckv, cache_mla_krope, state_rwkv, state_rwkv_shift, state_ssd, state_ssd_conv, state_gdn, state_gdn_conv, state_ffn_conv, meta_tokens, norm1_g, w_in, a_gq, a_wuq, a_gkv, a_wuk, a_wuv, a_gout, b_mu, b_w0, b_w2, b_a0, b_a2, b_g2, b_kk, b_ka, b_rk, b_gnw, b_gnb, c_convw, c_convb, c_dtb, c_alog, c_d, c_gnorm, d_convw, d_alog, d_dtb, d_gnorm, w_out, norm2_g, f_wup, f_convw, f_wdown, final_g):
    P = dict(norm1_g=norm1_g, w_in=w_in, a_gq=a_gq, a_wuq=a_wuq, a_gkv=a_gkv, a_wuk=a_wuk,
             a_wuv=a_wuv, a_gout=a_gout, b_mu=b_mu, b_w0=b_w0, b_w2=b_w2, b_a0=b_a0, b_a2=b_a2,
             b_g2=b_g2, b_kk=b_kk, b_ka=b_ka, b_rk=b_rk, b_gnw=b_gnw, b_gnb=b_gnb,
             c_convw=c_convw, c_convb=c_convb, c_dtb=c_dtb, c_alog=c_alog, c_d=c_d, c_gnorm=c_gnorm,
             d_convw=d_convw, d_alog=d_alog, d_dtb=d_dtb, d_gnorm=d_gnorm,
             w_out=w_out, norm2_g=norm2_g, f_wup=f_wup, f_convw=f_convw, f_wdown=f_wdown)
    P = [_layer_consts(P, li) for li in range(DEPTH)]
    fin = final_g.reshape(1, -1).astype(F32)
    b_p, seq, _ = x_prompt.shape
    lx = N_META + seq
    assert seq % ATT_BLOCK == 0 and N_META <= ATT_BLOCK
    front = ATT_BLOCK - N_META
    l_pad = front + lx
    meta = jnp.broadcast_to(meta_tokens[None].astype(F32), (b_p, N_META, D_MODEL))
    x_ext = jnp.concatenate([jnp.zeros((b_p, front, D_MODEL), F32), meta, x_prompt], axis=1)
    pos_p = jnp.arange(l_pad, dtype=jnp.int32) - (front + N_META)
    y_p, ns_p = _trunk(x_ext, pos_p, front, None, P, tm=ROW_TILE, c=CHUNK, tq=ATT_BLOCK, causal=True,
                       n_keys_pad=l_pad)
    y_prompt = _final_norm(y_p, fin, ROW_TILE, (front + N_META) // ROW_TILE, seq)
    b_s, t_s, _ = x_sample.shape
    past = cache_mla_ckv.shape[2]
    assert t_s <= CHUNK and t_s % 16 == 0 and (t_s & (t_s - 1)) == 0
    st_s = dict(ckv=cache_mla_ckv, krope=cache_mla_krope, rwkv_S=state_rwkv, rwkv_shift=state_rwkv_shift,
                ssd_S=state_ssd, ssd_conv=state_ssd_conv, gdn_S=state_gdn, gdn_conv=state_gdn_conv,
                ffn_conv=state_ffn_conv)
    pos_s = past + jnp.arange(t_s, dtype=jnp.int32)
    n_keys_pad = -(-(past + t_s) // LANE) * LANE
    y_s, ns_s = _trunk(x_sample, pos_s, 0, st_s, P, tm=t_s, c=t_s, tq=t_s, causal=False,
                       n_keys_pad=n_keys_pad)
    y_sample = _final_norm(y_s, fin, t_s, 0, t_s)
    keys = ('ckv', 'krope', 'rwkv_S', 'rwkv_shift', 'ssd_S', 'ssd_conv', 'gdn_S', 'gdn_conv', 'ffn_conv')
    return (y_prompt, y_sample) + tuple(ns_p[k] for k in keys) + tuple(ns_s[k] for k in keys)
```

```python
import functools
import math

import numpy as np
import jax
import jax.numpy as jnp
from jax import lax
from jax.experimental import pallas as pl
from jax.experimental.pallas import tpu as pltpu

F32 = jnp.float32
BF16 = jnp.bfloat16

D_MODEL = 1024
DEPTH = 4
CHUNK = 64
N_META = 16
EPS = 1e-6
L2_EPS = 1e-6
GROUP_W = 256
N_HEADS = 4
HEAD_W = 64
A_NOPE = 64
A_ROPE = 32
A_QRANK = 192
A_KVRANK = 128
A_SCALE = (A_NOPE + A_ROPE) ** -0.5
ROPE_BASE = 10000.0
B_GN_EPS = 64e-5
B_COLS = 1024
C_CONV = 4
D_CONV = 4
D_FF = 2816
FFN_CONV = 3
A_COLS = 352
C_COLS = 772
D_COLS = 1032
IN_COLS = A_COLS + B_COLS + C_COLS + D_COLS

LANE = 128
SUBLANE = 8
VMEM_LIMIT = 56 * 1024 * 1024
NEG = -1e30
LOG2E = math.log2(math.e)
FFN_CW = 256
ATT_BLOCK = 256
ROW_TILE = 256
SCAN_ROWS = 2


def _mm(a, b):
    return jnp.dot(a.astype(BF16), b.astype(BF16), preferred_element_type=F32)


def _mm_nt(a, b):
    return lax.dot_general(a.astype(BF16), b.astype(BF16), (((1,), (1,)), ((), ())),
                           preferred_element_type=F32)


def _mm_tn(a, b):
    return lax.dot_general(a.astype(BF16), b.astype(BF16), (((0,), (0,)), ((), ())),
                           preferred_element_type=F32)


def _split2(x):
    hi = x.astype(BF16)
    lo = (x - hi.astype(F32)).astype(BF16)
    return hi, lo


def _mm_x2(x, w):
    hi, lo = _split2(x)
    return (jnp.dot(hi, w, preferred_element_type=F32)
            + jnp.dot(lo, w, preferred_element_type=F32))


def _rms(x, g, n):
    ms = jnp.sum(x * x, axis=-1, keepdims=True) * (1.0 / n)
    return x * lax.rsqrt(ms + EPS) * g


def _softplus(x):
    return jnp.maximum(x, 0.0) + jnp.log1p(jnp.exp(-jnp.abs(x)))


def _sigmoid(x):
    return 1.0 / (1.0 + jnp.exp(-x))


def _silu(x):
    return x * _sigmoid(x)


def _iota(shape, axis):
    return lax.broadcasted_iota(jnp.int32, shape, axis)


def _norm_in(x_ref, g_ref, j, tm, front):
    h = _rms(x_ref[0], g_ref[...], D_MODEL)
    if front > 0:
        rows = j * tm + _iota((tm, 1), 0)
        h = jnp.where(rows >= front, h, 0.0)
    return h.astype(BF16)


def _norm_in_rows(x_ref, g_ref, j, tm, front):
    x = x_ref[...]
    h = _rms(x, g_ref[...], D_MODEL)
    if front > 0:
        pos = j * tm + _iota((1, tm, 1), 1)
        h = jnp.where(pos >= front, h, 0.0)
    return h.reshape(x.shape[0] * tm, D_MODEL).astype(BF16)


def _hsum(x, bd):
    return _mm_x2(x, bd)


TRI_INCL, TRI_STRICT, TRI_EYE, TRI_JOIN = 0, 1, 2, 3


class _Stk:
    def __init__(self, c, hm_ref, tri_ref, tl_ref):
        self.c = c
        self.n = N_HEADS * c
        self.steps = int(math.log2(c))
        assert 1 << self.steps == c
        self.hm_ref, self.tri_ref, self.tl_ref = hm_ref, tri_ref, tl_ref

    def mask(self, which):
        return self.tri_ref[which]

    def tile(self, x):
        return jnp.concatenate([x] * N_HEADS, axis=0)

    def stack(self, x):
        return self.tile(x) * self.hm_ref[...]

    def unstack(self, xs):
        c = self.c
        return xs[0:c] + xs[c:2 * c] + xs[2 * c:3 * c] + xs[3 * c:4 * c]

    def cumsum(self, x):
        hi, lo = _split2(x)
        tl = self.tl_ref[...]
        return jnp.dot(tl, hi, preferred_element_type=F32) + jnp.dot(tl, lo, preferred_element_type=F32)

    def decay(self, g_cum):
        gs = self.stack(g_cum)
        gcol = jnp.min(gs, axis=-1, keepdims=True)
        hi, lo = _split2(gs)
        o2 = jnp.full((self.n, GROUP_W), 1.0 / HEAD_W, BF16)
        grow = _mm_nt(o2, hi) + _mm_nt(o2, lo)
        return jnp.exp(jnp.minimum(gcol - grow, 0.0)) * self.mask(TRI_INCL)

    def tri_inv_many(self, lowers):
        eye = self.mask(TRI_EYE)
        minvs = [eye - lw * self.mask(TRI_JOIN) for lw in lowers]
        for lb in range(1, self.steps):
            ts = [_mm(mi, lw * self.mask(TRI_JOIN + lb)) for mi, lw in zip(minvs, lowers)]
            minvs = [mi - _mm(t, mi) for t, mi in zip(ts, minvs)]
        return minvs


def _mla_prep_kernel(x_ref, g1_ref, wa_ref, gq_ref, wq1t_ref, wq2t_ref, gkv_ref, tabq_ref, tabk_ref,
                     qt_out, ckv_out, kr_out, *, tm, front):
    j = pl.program_id(1)
    h = _norm_in(x_ref, g1_ref, j, tm, front)
    pa = jnp.dot(h, wa_ref[...], preferred_element_type=F32)
    qn = _rms(pa[:, 0:256], gq_ref[...], A_QRANK)
    q1t = _mm_nt(wq1t_ref[...], qn)
    q2t = _mm_nt(wq2t_ref[...], qn)
    tabq = tabq_ref[...]
    cos4 = jnp.concatenate([tabq[0:128]] * N_HEADS, axis=0)
    sin4 = jnp.concatenate([tabq[128:256]] * N_HEADS, axis=0)
    qt_out[0] = ((q1t * cos4 + q2t * sin4) * (A_SCALE * LOG2E)).astype(BF16)
    ckv_out[0] = _rms(pa[:, 256:384], gkv_ref[...], A_KVRANK)
    tabk = tabk_ref[...]
    kr = pa[:, 384:512] * tabk[:, 0:128] + pa[:, 512:640] * tabk[:, 128:256]
    kr_out[0] = kr[:, 0:A_ROPE]


def _kv_up_kernel(c_ref, kr_ref, wk_ref, ek_ref, wvt_ref, ones_ref, k_out, vt_out):
    c = c_ref[0].astype(BF16)
    kr = kr_ref[0].astype(BF16)
    k = (jnp.dot(c, wk_ref[...], preferred_element_type=F32)
         + jnp.dot(kr, ek_ref[...], preferred_element_type=F32))
    k_out[0] = k.astype(BF16)
    vt_out[0, 0] = (_mm_nt(wvt_ref[...], c) + ones_ref[...]).astype(BF16)


def _flash_kernel(qt_ref, k_ref, vt_ref, gout_ref, o_ref, m_sc, acc_sc, sa_sc, sb_sc,
                  *, tq, tk, nkv, causal, klo, khi):
    i = pl.program_id(1)
    m_sc[...] = jnp.full((N_HEADS * SUBLANE, tq), NEG, F32)
    acc_sc[...] = jnp.zeros((N_HEADS * LANE, tq), F32)
    heads = [slice(LANE * h, LANE * (h + 1)) for h in range(N_HEADS)]

    s_bufs = (sa_sc, sb_sc)

    def produce(jb, slot):
        start = pl.multiple_of(jb * tk, tk)
        for h, rows in enumerate(heads):
            s_bufs[slot][h * tk:(h + 1) * tk, :] = jnp.dot(
                k_ref[0, pl.ds(start, tk), rows], qt_ref[0, rows, :], preferred_element_type=F32)

    def consume(jb, slot, masked, nxt):
        if nxt is not None:
            produce(*nxt)
        if masked:
            kpos = jb * tk + _iota((tk, 1), 0)
            qpos = i * tq + _iota((1, tq), 1)
            vis = (kpos >= klo) & (kpos < khi)
            if causal:
                vis = vis & ((kpos >> 6) <= (qpos >> 6))
        m_all = m_sc[...]
        acc_all = acc_sc[...]
        m_out, alphas, ps = [], [], []
        for h in range(N_HEADS):
            s = s_bufs[slot][h * tk:(h + 1) * tk, :]
            if masked:
                s = jnp.where(vis, s, NEG)
            m_prev = m_all[SUBLANE * h:SUBLANE * (h + 1)]
            m_new = jnp.maximum(m_prev, jnp.max(s, axis=0, keepdims=True))
            alphas.append(jnp.exp2(m_prev[0:1] - m_new[0:1]))
            ps.append(jnp.exp2(s - m_new[0:1]).astype(BF16))
            m_out.append(m_new)
        acc_out = [alphas[h] * acc_all[rows]
                   + jnp.dot(vt_ref[0, jb, rows, :], ps[h], preferred_element_type=F32)
                   for h, rows in enumerate(heads)]
        m_sc[...] = jnp.concatenate(m_out, axis=0)
        acc_sc[...] = jnp.concatenate(acc_out, axis=0)

    produce(0, 0)
    if causal:
        @pl.when(i == 0)
        def _():
            consume(0, 0, True, None)

        @pl.when(i > 0)
        def _():
            consume(0, 0, True, (1, 1))
            pairs = (i - 1) >> 1

            def body(t, carry):
                jb = 1 + 2 * t
                consume(jb, 1, False, (jb + 1, 0))
                consume(jb + 1, 0, False, (jb + 2, 1))
                return carry

            lax.fori_loop(0, pairs, body, 0)

            @pl.when((i & 1) == 0)
            def _():
                consume(i - 1, 1, False, (i, 0))
                consume(i, 0, True, None)

            @pl.when((i & 1) == 1)
            def _():
                consume(i, 1, True, None)
    else:
        for jb in range(nkv):
            consume(jb, jb & 1, True, (jb + 1, (jb + 1) & 1) if jb + 1 < nkv else None)
    outs = []
    for h in range(N_HEADS):
        a = acc_sc[LANE * h:LANE * (h + 1), :]
        outs.append(a[0:HEAD_W] / a[HEAD_W:HEAD_W + 1])
    yat = jnp.concatenate(outs, axis=0)
    ms = jnp.sum(yat * yat, axis=0, keepdims=True) * (1.0 / GROUP_W)
    o_ref[0] = yat * lax.rsqrt(ms + EPS) * gout_ref[...]


def _chunk_ids(grp, tm, c):
    ids = [(g, ci) for ci in range(tm // c) for g in range(grp)]
    return ids, {(g, ci): slice(g * tm + ci * c, g * tm + (ci + 1) * c) for g, ci in ids}


def _carry_rows(work, g, new_rows, tm):
    work[g, SUBLANE:SUBLANE + tm, :] = new_rows
    return work[g, tm:tm + SUBLANE, :]


def _rwkv_kernel(x_ref, g1_ref, wb_ref, shift_ref, s0_ref, mu_ref, w0_ref, wl_ref, a0_ref,
                 g2_ref, kk_ref, ka_ref, rk_ref, gnw_ref, gnb_ref, hm_ref, tri_ref, tl_ref, bd_ref,
                 y_out, s_out, shift_out, work, st, *, tm, c, front, grp):
    j = pl.program_id(1)

    @pl.when(j == 0)
    def _():
        work[:, 0:SUBLANE, :] = shift_ref[...]
        st[...] = s0_ref[...]

    h = _norm_in_rows(x_ref, g1_ref, j, tm, front)
    cols = jnp.dot(h, wb_ref[...], preferred_element_type=F32)
    shifted = []
    for g in range(grp):
        tail = _carry_rows(work, g, cols[g * tm:(g + 1) * tm], tm)
        shifted.append(work[g, pl.ds(SUBLANE - 1, tm), :])
        work[g, 0:SUBLANE, :] = tail
        shift_out[g] = tail
    shifted = jnp.concatenate(shifted, axis=0)
    xm = cols + (shifted - cols) * mu_ref[...]
    r = xm[:, 0:256]
    k = xm[:, 256:512]
    v = xm[:, 512:768]
    lora = xm[:, 768:896]
    dg = xm[:, 896:1024]
    lora = jnp.where(_iota((grp * tm, LANE), 1) < 64, jnp.tanh(lora), lora)
    ll = _mm(lora, wl_ref[...])
    w_log = -_softplus(-(w0_ref[...] + ll[:, 0:256])) - 0.5
    logd = -jnp.exp(w_log)
    a = _sigmoid(a0_ref[...] + ll[:, 256:512])
    g_gate = _mm(_sigmoid(dg), g2_ref[...])
    bd = bd_ref[...]
    kkr = k * kk_ref[...]
    kk = kkr * lax.rsqrt(_hsum(kkr * kkr, bd) + L2_EPS)
    k2 = k * (1.0 + (a - 1.0) * ka_ref[...])

    sk = _Stk(c, hm_ref, tri_ref, tl_ref)
    ids, rows = _chunk_ids(grp, tm, c)
    strict, incl = sk.mask(TRI_STRICT), sk.mask(TRI_INCL)
    gcs = {i: sk.cumsum(logd[rows[i]]) for i in ids}
    pre = {}
    for i in ids:
        gc, ld = gcs[i], logd[rows[i]]
        eg, eng = jnp.exp(gc), jnp.exp(-gc)
        bt = kk[rows[i]] * a[rows[i]] * eng
        kt = k2[rows[i]] * eng
        dc = eg[c - 1:c, :]
        pre[i] = dict(at_s=sk.stack(-kk[rows[i]] * jnp.exp(gc - ld)), rt_s=sk.stack(r[rows[i]] * eg),
                      v_s=sk.stack(v[rows[i]]), bt_t=sk.tile(bt), kt_t=sk.tile(kt), dc=dc,
                      bd_s=sk.stack(bt * dc), kd_s=sk.stack(kt * dc))
    lab = {i: _mm_nt(pre[i]['at_s'], pre[i]['bt_t']) * strict for i in ids}
    aak = {i: _mm_nt(pre[i]['at_s'], pre[i]['kt_t']) * strict for i in ids}
    arb = {i: _mm_nt(pre[i]['rt_s'], pre[i]['bt_t']) * incl for i in ids}
    ark = {i: _mm_nt(pre[i]['rt_s'], pre[i]['kt_t']) * incl for i in ids}
    minv = dict(zip(ids, sk.tri_inv_many([-lab[i] for i in ids])))
    a2 = {i: _mm(aak[i], pre[i]['v_s']) for i in ids}
    u0 = {i: _mm(minv[i], a2[i]) for i in ids}
    m1 = {i: _mm(minv[i], pre[i]['at_s']) for i in ids}
    ork = {i: _mm(ark[i], pre[i]['v_s']) for i in ids}
    skv = {i: _mm_tn(pre[i]['v_s'], pre[i]['kd_s']) for i in ids}

    s = [st[g] for g in range(grp)]
    outs = {}
    for ci in range(tm // c):
        sb = [s[g].astype(BF16) for g in range(grp)]
        us = [u0[(g, ci)] + _mm_nt(m1[(g, ci)], sb[g]) for g in range(grp)]
        oq = [_mm_nt(pre[(g, ci)]['rt_s'], sb[g]) for g in range(grp)]
        for g in range(grp):
            i = (g, ci)
            s[g] = s[g] * pre[i]['dc'] + _mm_tn(us[g], pre[i]['bd_s']) + skv[i]
            outs[i] = sk.unstack(oq[g] + _mm(arb[i], us[g]) + ork[i])
    for g in range(grp):
        st[g] = s[g]
        s_out[g] = s[g]
    o = jnp.concatenate([outs[(g, ci)] for g in range(grp) for ci in range(tm // c)], axis=0)
    mean = _hsum(o, bd) * (1.0 / HEAD_W)
    d = o - mean
    var = _hsum(d * d, bd) * (1.0 / HEAD_W)
    o = d * lax.rsqrt(var + B_GN_EPS) * gnw_ref[...] + gnb_ref[...]
    bonus = _hsum(r * k2 * rk_ref[...], bd) * v
    y_out[...] = ((o + bonus) * g_gate).reshape(grp, tm, GROUP_W)


def _conv4_rows(work, new, wv, hist_out, grp, tm):
    ys = []
    for g in range(grp):
        tail = _carry_rows(work, g, new[g * tm:(g + 1) * tm], tm)
        y = work[g, pl.ds(SUBLANE - 3, tm), :] * wv[0:1, :]
        y = y + work[g, pl.ds(SUBLANE - 2, tm), :] * wv[1:2, :]
        y = y + work[g, pl.ds(SUBLANE - 1, tm), :] * wv[2:3, :]
        ys.append(y + work[g, pl.ds(SUBLANE, tm), :] * wv[3:4, :])
        work[g, 0:SUBLANE, :] = tail
        hist_out[g] = tail
    return jnp.concatenate(ys, axis=0)


def _ssd_kernel(x_ref, g1_ref, wc_ref, hist_ref, s0_ref, cw_ref, cb_ref, dtb_ref, alog_ref,
                dskip_ref, gn_ref, hm_ref, tri_ref, tl_ref, bd_ref,
                y_out, s_out, hist_out, work, st, *, tm, c, front, grp):
    j = pl.program_id(1)

    @pl.when(j == 0)
    def _():
        work[:, 0:SUBLANE, :] = hist_ref[...]
        st[...] = s0_ref[...]

    h = _norm_in_rows(x_ref, g1_ref, j, tm, front)
    pc = jnp.dot(h, wc_ref[...], preferred_element_type=F32)
    z = pc[:, 0:256]
    xbc = _silu(_conv4_rows(work, pc[:, 256:1024], cw_ref[...], hist_out, grp, tm) + cb_ref[...])
    xs = xbc[:, 0:256]
    bm = xbc[:, 256:512]
    cm = xbc[:, 512:768]
    dt = _softplus(pc[:, 1024:1280] + dtb_ref[...])
    if front > 0:
        pos = j * tm + _iota((1, tm, 1), 1)
        dt = jnp.where(pos >= front, dt.reshape(grp, tm, GROUP_W), 0.0).reshape(grp * tm, GROUP_W)
    a = dt * (-jnp.exp(alog_ref[...]))
    xdt = xs * dt

    sk = _Stk(c, hm_ref, tri_ref, tl_ref)
    ids, rows = _chunk_ids(grp, tm, c)
    acs = {i: sk.cumsum(a[rows[i]]) for i in ids}
    dms = {i: sk.decay(acs[i]) for i in ids}
    xdt_s = {i: sk.stack(xdt[rows[i]]) for i in ids}
    amat = {i: _mm_nt(sk.stack(cm[rows[i]]), sk.tile(bm[rows[i]])) * dms[i] for i in ids}
    ydiag = {i: _mm(amat[i], xdt_s[i]) for i in ids}
    sx = {i: _mm_tn(sk.stack(bm[rows[i]] * jnp.exp(acs[i][c - 1:c, :] - acs[i])), xdt_s[i]) for i in ids}
    ce_s = {i: sk.stack(cm[rows[i]] * jnp.exp(acs[i])) for i in ids}

    s = [st[g] for g in range(grp)]
    outs = {}
    for ci in range(tm // c):
        for g in range(grp):
            i = (g, ci)
            outs[i] = sk.unstack(ydiag[i] + _mm(ce_s[i], s[g]))
            s[g] = s[g] * jnp.exp(acs[i][c - 1:c, :]) + sx[i]
    for g in range(grp):
        st[g] = s[g]
        s_out[g] = s[g]
    y = jnp.concatenate([outs[(g, ci)] for g in range(grp) for ci in range(tm // c)], axis=0)
    y = y + dskip_ref[...] * xs
    y_out[...] = _rms(y * _silu(z), gn_ref[...], GROUP_W).reshape(grp, tm, GROUP_W)


def _gdn_kernel(x_ref, g1_ref, wd_ref, hist_ref, s0_ref, cw_ref, alog_ref, dtb_ref, gn_ref,
                hm_ref, tri_ref, tl_ref, bd_ref,
                y_out, s_out, hist_out, work, st, *, tm, c, front, grp):
    j = pl.program_id(1)

    @pl.when(j == 0)
    def _():
        work[:, 0:SUBLANE, :] = hist_ref[...]
        st[...] = s0_ref[...]

    h = _norm_in_rows(x_ref, g1_ref, j, tm, front)
    pd = jnp.dot(h, wd_ref[...], preferred_element_type=F32)
    qkv = _silu(_conv4_rows(work, pd[:, 0:768], cw_ref[...], hist_out, grp, tm))
    z = pd[:, 768:1024]
    beta = _sigmoid(pd[:, 1024:1280])
    g_log = -jnp.exp(alog_ref[...]) * _softplus(pd[:, 1280:1536] + dtb_ref[...])
    bd = bd_ref[...]
    q = qkv[:, 0:256]
    k = qkv[:, 256:512]
    v = qkv[:, 512:768]
    q = q * lax.rsqrt(_hsum(q * q, bd) + L2_EPS) * (HEAD_W ** -0.5)
    k = k * lax.rsqrt(_hsum(k * k, bd) + L2_EPS)

    sk = _Stk(c, hm_ref, tri_ref, tl_ref)
    ids, rows = _chunk_ids(grp, tm, c)
    strict = sk.mask(TRI_STRICT)
    gcs = {i: sk.cumsum(g_log[rows[i]]) for i in ids}
    dms = {i: sk.decay(gcs[i]) for i in ids}
    kb = {i: k[rows[i]] * beta[rows[i]] for i in ids}
    k_t = {i: sk.tile(k[rows[i]]) for i in ids}
    lower = {i: _mm_nt(sk.stack(kb[i]), k_t[i]) * dms[i] * strict for i in ids}
    aqk = {i: _mm_nt(sk.stack(q[rows[i]]), k_t[i]) * dms[i] for i in ids}
    tinv = dict(zip(ids, sk.tri_inv_many([lower[i] for i in ids])))
    u = {i: _mm(tinv[i], sk.stack(v[rows[i]] * beta[rows[i]])) for i in ids}
    w = {i: _mm(tinv[i], sk.stack(kb[i] * jnp.exp(gcs[i]))) for i in ids}
    qe_s = {i: sk.stack(q[rows[i]] * jnp.exp(gcs[i])) for i in ids}
    kd_s = {i: sk.stack(k[rows[i]] * jnp.exp(gcs[i][c - 1:c, :] - gcs[i])) for i in ids}

    s = [st[g] for g in range(grp)]
    outs = {}
    for ci in range(tm // c):
        sb = [s[g].astype(BF16) for g in range(grp)]
        vn = [u[(g, ci)] - _mm(w[(g, ci)], sb[g]) for g in range(grp)]
        oq = [_mm(qe_s[(g, ci)], sb[g]) for g in range(grp)]
        for g in range(grp):
            i = (g, ci)
            s[g] = s[g] * jnp.exp(gcs[i][c - 1:c, :]) + _mm_tn(kd_s[i], vn[g])
            outs[i] = sk.unstack(oq[g] + _mm(aqk[i], vn[g]))
    for g in range(grp):
        st[g] = s[g]
        s_out[g] = s[g]
    o = jnp.concatenate([outs[(g, ci)] for g in range(grp) for ci in range(tm // c)], axis=0)
    ms = _hsum(o * o, bd) * (1.0 / HEAD_W)
    y_out[...] = (o * lax.rsqrt(ms + EPS) * gn_ref[...] * _silu(z)).reshape(grp, tm, GROUP_W)


def _ffn_kernel(x_ref, ya_ref, yb_ref, yc_ref, yd_ref, wo_ref, g2_ref, wup_ref, cw_ref, wdn_ref,
                hist_ref, xo_ref, hist_out, carry, work, *, tm, front):
    j = pl.program_id(1)

    @pl.when(j == 0)
    def _():
        carry[...] = hist_ref[0]

    x = x_ref[0] + _mm_tn(ya_ref[0], wo_ref[0:GROUP_W, :])
    for idx, y_ref in ((1, yb_ref), (2, yc_ref), (3, yd_ref)):
        x = x + jnp.dot(y_ref[0].astype(BF16), wo_ref[GROUP_W * idx:GROUP_W * (idx + 1), :],
                        preferred_element_type=F32)
    h2 = _rms(x, g2_ref[...], D_MODEL)
    if front > 0:
        rows = j * tm + _iota((tm, 1), 0)
        h2 = jnp.where(rows >= front, h2, 0.0)
    h2 = h2.astype(BF16)
    acc = jnp.zeros((tm, D_MODEL), F32)
    w2 = 2 * FFN_CW
    for f in range(D_FF // FFN_CW):
        cols = slice(f * w2, (f + 1) * w2)
        u = jnp.dot(h2, wup_ref[:, cols], preferred_element_type=F32)
        work[0:SUBLANE, :] = carry[:, cols]
        work[SUBLANE:SUBLANE + tm, :] = u
        cw = cw_ref[:, cols]
        y = (work[pl.ds(SUBLANE - 2, tm), :] * cw[0:1, :]
             + work[pl.ds(SUBLANE - 1, tm), :] * cw[1:2, :] + u * cw[2:3, :])
        carry[:, cols] = work[tm:tm + SUBLANE, :]
        act = _silu(y[:, 0:FFN_CW]) * y[:, FFN_CW:w2]
        acc = acc + jnp.dot(act.astype(BF16), wdn_ref[f * FFN_CW:(f + 1) * FFN_CW, :],
                            preferred_element_type=F32)
    xo_ref[0] = x + acc
    hist_out[0] = carry[...]


def _final_norm_kernel(x_ref, g_ref, o_ref):
    o_ref[0] = _rms(x_ref[0], g_ref[...], D_MODEL)


def _const_spec(arr):
    nd = arr.ndim
    return pl.BlockSpec(arr.shape, lambda b, j: (0,) * nd, pipeline_mode=pl.Buffered(1))


def _tile_spec(tm, width):
    return pl.BlockSpec((1, tm, width), lambda b, j: (b, j, 0))


def _batch_spec(rows, width):
    return pl.BlockSpec((1, rows, width), lambda b, j: (b, 0, 0))


def _params():
    return pltpu.CompilerParams(dimension_semantics=("arbitrary", "arbitrary"),
                                vmem_limit_bytes=VMEM_LIMIT)


def _cols_spec(rows, tm):
    return pl.BlockSpec((1, rows, tm), lambda b, j: (b, 0, j))


def _mla_prep(x, g1, wa, gq, wq1t, wq2t, gkv, tabq, tabk, tm, front):
    b, l, _ = x.shape
    consts = (g1, wa, gq, wq1t, wq2t, gkv)
    return pl.pallas_call(
        functools.partial(_mla_prep_kernel, tm=tm, front=front),
        grid=(b, l // tm),
        in_specs=[_tile_spec(tm, D_MODEL)] + [_const_spec(a) for a in consts]
        + [pl.BlockSpec((2 * LANE, tm), lambda bb, j: (0, j)),
           pl.BlockSpec((tm, 2 * LANE), lambda bb, j: (j, 0))],
        out_specs=[_cols_spec(512, tm), _tile_spec(tm, A_KVRANK), _tile_spec(tm, A_ROPE)],
        out_shape=[jax.ShapeDtypeStruct((b, 512, l), BF16),
                   jax.ShapeDtypeStruct((b, l, A_KVRANK), F32),
                   jax.ShapeDtypeStruct((b, l, A_ROPE), F32)],
        compiler_params=_params(), name="mla_prep",
    )(x, *consts, tabq, tabk)


def _kv_up(c_all, kr_all, wk, ek, wvt, ones_col, tm):
    b, n, _ = c_all.shape
    consts = (wk, ek, wvt, ones_col)
    return pl.pallas_call(
        _kv_up_kernel,
        grid=(b, n // tm),
        in_specs=[_tile_spec(tm, A_KVRANK), _tile_spec(tm, A_ROPE)] + [_const_spec(a) for a in consts],
        out_specs=[_tile_spec(tm, 512), pl.BlockSpec((1, 1, 512, tm), lambda bb, j: (bb, j, 0, 0))],
        out_shape=[jax.ShapeDtypeStruct((b, n, 512), BF16),
                   jax.ShapeDtypeStruct((b, n // tm, 512, tm), BF16)],
        compiler_params=_params(), name="kv_up",
    )(c_all, kr_all, *consts)


def _flash(qt, k, vt, gout_col, tq, tk, causal, klo, khi):
    b, _, l = qt.shape
    n = k.shape[1]
    nkv = n // tk
    return pl.pallas_call(
        functools.partial(_flash_kernel, tq=tq, tk=tk, nkv=nkv, causal=causal, klo=klo, khi=khi),
        grid=(b, l // tq),
        in_specs=[_cols_spec(512, tq),
                  pl.BlockSpec((1, n, 512), lambda bb, j: (bb, 0, 0), pipeline_mode=pl.Buffered(1)),
                  pl.BlockSpec((1, nkv, 512, tk), lambda bb, j: (bb, 0, 0, 0), pipeline_mode=pl.Buffered(1)),
                  _const_spec(gout_col)],
        out_specs=_cols_spec(GROUP_W, tq),
        out_shape=jax.ShapeDtypeStruct((b, GROUP_W, l), F32),
        scratch_shapes=[pltpu.VMEM((N_HEADS * SUBLANE, tq), F32), pltpu.VMEM((N_HEADS * LANE, tq), F32),
                        pltpu.VMEM((N_HEADS * tk, tq), F32), pltpu.VMEM((N_HEADS * tk, tq), F32)],
        compiler_params=_params(), name="mla_flash",
    )(qt, k, vt, gout_col)


def _stack_consts(c):
    n = N_HEADS * c
    lc = int(math.log2(c))
    r, l = np.arange(n)[:, None], np.arange(GROUP_W)[None, :]
    hm = ((r >> lc) == (l >> 6)).astype(np.float32)
    rr, cc = np.arange(n)[:, None], np.arange(n)[None, :]
    same = (rr >> lc) == (cc >> lc)
    tri = [same & (cc <= rr), same & (cc < rr), rr == cc]
    for lb in range(lc):
        tri.append(((rr >> (lb + 1)) == (cc >> (lb + 1))) & (((rr >> lb) & 1) == 1) & (((cc >> lb) & 1) == 0))
    tl = np.arange(c)[None, :] <= np.arange(c)[:, None]
    hh = np.arange(GROUP_W)
    bd = (hh[:, None] >> 6) == (hh[None, :] >> 6)
    return (jnp.asarray(hm), jnp.asarray(np.stack(tri).astype(np.float32)),
            jnp.asarray(tl, BF16), jnp.asarray(bd, BF16))


def _scan_call(kernel, name, x, consts_a, hist, s0, consts_b, width_in, tm, c, front, grp):
    b, l, _ = x.shape
    assert b % grp == 0
    consts_b = tuple(consts_b) + _stack_consts(c)
    rows_spec = lambda rows, width: pl.BlockSpec((grp, rows, width), lambda bb, j: (bb, 0, 0))
    return pl.pallas_call(
        functools.partial(kernel, tm=tm, c=c, front=front, grp=grp),
        grid=(b // grp, l // tm),
        in_specs=[pl.BlockSpec((grp, tm, D_MODEL), lambda bb, j: (bb, j, 0))]
        + [_const_spec(a) for a in consts_a]
        + [rows_spec(SUBLANE, width_in), rows_spec(GROUP_W, GROUP_W)]
        + [_const_spec(a) for a in consts_b],
        out_specs=[pl.BlockSpec((grp, tm, GROUP_W), lambda bb, j: (bb, j, 0)),
                   rows_spec(GROUP_W, GROUP_W), rows_spec(SUBLANE, width_in)],
        out_shape=[jax.ShapeDtypeStruct((b, l, GROUP_W), F32),
                   jax.ShapeDtypeStruct((b, GROUP_W, GROUP_W), F32),
                   jax.ShapeDtypeStruct((b, SUBLANE, width_in), F32)],
        scratch_shapes=[pltpu.VMEM((grp, tm + SUBLANE, width_in), F32),
                        pltpu.VMEM((grp, GROUP_W, GROUP_W), F32)],
        compiler_params=_params(), name=name,
    )(x, *consts_a, hist, s0, *consts_b)


def _ffn(x, ya, yb, yc, yd, wo, g2, wup, cw, wdn, hist, tm, front):
    b, l, _ = x.shape
    return pl.pallas_call(
        functools.partial(_ffn_kernel, tm=tm, front=front),
        grid=(b, l // tm),
        in_specs=[_tile_spec(tm, D_MODEL), _cols_spec(GROUP_W, tm)] + [_tile_spec(tm, GROUP_W)] * 3
        + [_const_spec(a) for a in (wo, g2, wup, cw, wdn)] + [_batch_spec(SUBLANE, 2 * D_FF)],
        out_specs=[_tile_spec(tm, D_MODEL), _batch_spec(SUBLANE, 2 * D_FF)],
        out_shape=[jax.ShapeDtypeStruct((b, l, D_MODEL), F32),
                   jax.ShapeDtypeStruct((b, SUBLANE, 2 * D_FF), F32)],
        scratch_shapes=[pltpu.VMEM((SUBLANE, 2 * D_FF), F32),
                        pltpu.VMEM((tm + SUBLANE, 2 * FFN_CW), F32)],
        compiler_params=_params(), name="out_ffn",
    )(x, ya, yb, yc, yd, wo, g2, wup, cw, wdn, hist)


def _final_norm(x, g, tm, skip_tiles, out_rows):
    b = x.shape[0]
    return pl.pallas_call(
        _final_norm_kernel,
        grid=(b, out_rows // tm),
        in_specs=[pl.BlockSpec((1, tm, D_MODEL), lambda bb, j: (bb, j + skip_tiles, 0)), _const_spec(g)],
        out_specs=_tile_spec(tm, D_MODEL),
        out_shape=jax.ShapeDtypeStruct((b, out_rows, D_MODEL), F32),
        compiler_params=_params(), name="final_norm",
    )(x, g)


def _np_idx():
    z = IN_COLS
    zpad = lambda n: [z] * n
    rep = lambda base: [base + i for i in range(N_HEADS) for _ in range(HEAD_W)]
    grp = lambda base: [base + g * HEAD_W + i for g in (0, 0, 1, 1) for i in range(HEAD_W)]
    a = (list(range(0, 192)) + zpad(64) + list(range(192, 320))
         + list(range(320, 352)) + zpad(96)
         + list(range(336, 352)) + list(range(320, 336)) + zpad(96))
    b0 = A_COLS
    bcols = list(range(b0, b0 + B_COLS))
    c0 = b0 + B_COLS
    ccols = (list(range(c0, c0 + 256)) + list(range(c0 + 256, c0 + 512))
             + grp(c0 + 512) + grp(c0 + 640) + rep(c0 + 768))
    d0 = c0 + C_COLS
    dcols = list(range(d0, d0 + 1024)) + rep(d0 + 1024) + rep(d0 + 1028)
    xbc_exp = list(range(256)) + grp(256) + grp(384)
    xbc_back = (list(range(256)) + list(range(256, 320)) + list(range(384, 448))
                + list(range(512, 576)) + list(range(640, 704)))
    ffn_perm = []
    for f in range(D_FF // FFN_CW):
        ffn_perm += list(range(f * FFN_CW, (f + 1) * FFN_CW))
        ffn_perm += list(range(D_FF + f * FFN_CW, D_FF + (f + 1) * FFN_CW))
    ffn_back = np.argsort(np.array(ffn_perm))
    as_i = lambda v: np.asarray(v, np.int32)
    return dict(a=as_i(a), b=as_i(bcols), c=as_i(ccols), d=as_i(dcols), xbc_exp=as_i(xbc_exp),
                xbc_back=as_i(xbc_back), ffn_perm=as_i(ffn_perm), ffn_back=as_i(ffn_back))


_IDX = _np_idx()


def _rep_heads(v):
    return jnp.repeat(v, HEAD_W, axis=-1)


def _pad_rows(w, rows=SUBLANE):
    return jnp.pad(w, [(0, rows - w.shape[0])] + [(0, 0)] * (w.ndim - 1))


def _layer_consts(P, l):
    row = lambda v: v.reshape(1, -1).astype(F32)
    w_in = jnp.concatenate([P['w_in'][l], jnp.zeros((D_MODEL, 1), F32)], axis=1)
    c = {}
    c['g1'] = row(P['norm1_g'][l])
    c['wa'] = w_in[:, _IDX['a']].astype(BF16)
    c['wb'] = w_in[:, _IDX['b']].astype(BF16)
    c['wc'] = w_in[:, _IDX['c']].astype(BF16)
    c['wd'] = w_in[:, _IDX['d']].astype(BF16)
    c['gq'] = row(jnp.pad(P['a_gq'][l], (0, 64)))
    wuq = P['a_wuq'][l].reshape(A_QRANK, N_HEADS, A_NOPE + A_ROPE)
    rope = wuq[:, :, A_NOPE:]
    swap = jnp.concatenate([rope[..., 16:], rope[..., :16]], axis=-1)
    zeros = lambda n: jnp.zeros((A_QRANK, N_HEADS, n), F32)
    wq1 = jnp.concatenate([wuq, zeros(32)], axis=-1).reshape(A_QRANK, 512)
    wq2 = jnp.concatenate([zeros(64), swap, zeros(32)], axis=-1).reshape(A_QRANK, 512)
    c['wq1t'] = jnp.pad(wq1, ((0, 64), (0, 0))).T.astype(BF16)
    c['wq2t'] = jnp.pad(wq2, ((0, 64), (0, 0))).T.astype(BF16)
    c['gkv'] = row(P['a_gkv'][l])
    wuk = P['a_wuk'][l].reshape(A_KVRANK, N_HEADS, A_NOPE)
    c['wk'] = jnp.concatenate([wuk, jnp.zeros((A_KVRANK, N_HEADS, 64), F32)], axis=-1
                              ).reshape(A_KVRANK, 512).astype(BF16)
    ek = np.zeros((A_ROPE, N_HEADS, LANE), np.float32)
    for hh in range(N_HEADS):
        ek[np.arange(A_ROPE), hh, A_NOPE + np.arange(A_ROPE)] = 1.0
    c['ek'] = jnp.asarray(ek.reshape(A_ROPE, 512), BF16)
    wuv = P['a_wuv'][l].reshape(A_KVRANK, N_HEADS, HEAD_W)
    c['wvt'] = jnp.concatenate([wuv, jnp.zeros((A_KVRANK, N_HEADS, HEAD_W), F32)], axis=-1
                               ).reshape(A_KVRANK, 512).T.astype(BF16)
    ones = np.zeros((N_HEADS, LANE, 1), np.float32)
    ones[:, HEAD_W, 0] = 1.0
    c['ones_col'] = jnp.asarray(ones.reshape(512, 1))
    c['gout_col'] = P['a_gout'][l].reshape(GROUP_W, 1).astype(F32)
    c['mu'] = row(P['b_mu'][l])
    c['w0'] = row(P['b_w0'][l])
    z64 = jnp.zeros((64, GROUP_W), F32)
    c['wl'] = jnp.concatenate([jnp.concatenate([P['b_w2'][l], z64], axis=1),
                               jnp.concatenate([z64, P['b_a2'][l]], axis=1)], axis=0).astype(BF16)
    c['a0'] = row(P['b_a0'][l])
    c['g2b'] = P['b_g2'][l].astype(BF16)
    c['kk'] = row(P['b_kk'][l])
    c['ka'] = row(P['b_ka'][l])
    c['rk'] = row(P['b_rk'][l])
    c['gnw'] = row(P['b_gnw'][l])
    c['gnb'] = row(P['b_gnb'][l])
    c['c_cw'] = _pad_rows(P['c_convw'][l][:, _IDX['xbc_exp']])
    c['c_cb'] = row(P['c_convb'][l][_IDX['xbc_exp']])
    c['c_dtb'] = row(_rep_heads(P['c_dtb'][l]))
    c['c_alog'] = row(_rep_heads(P['c_alog'][l]))
    c['c_d'] = row(_rep_heads(P['c_d'][l]))
    c['c_gn'] = row(P['c_gnorm'][l])
    c['d_cw'] = _pad_rows(P['d_convw'][l])
    c['d_alog'] = row(_rep_heads(P['d_alog'][l]))
    c['d_dtb'] = row(_rep_heads(P['d_dtb'][l]))
    c['d_gn'] = row(jnp.tile(P['d_gnorm'][l], N_HEADS))
    c['wo'] = P['w_out'][l].astype(BF16)
    c['g2'] = row(P['norm2_g'][l])
    c['wup'] = P['f_wup'][l][:, _IDX['ffn_perm']].astype(BF16)
    c['f_cw'] = _pad_rows(P['f_convw'][l][:, _IDX['ffn_perm']])
    c['wdn'] = P['f_wdown'][l].astype(BF16)
    return c


def _embed_bd(s):
    b = s.shape[0]
    eye = jnp.eye(N_HEADS, dtype=s.dtype)
    return jnp.einsum('bhij,hg->bhigj', s, eye).reshape(b, GROUP_W, GROUP_W)


def _extract_bd(s):
    b = s.shape[0]
    s5 = s.reshape(b, N_HEADS, HEAD_W, N_HEADS, HEAD_W)
    return jnp.stack([s5[:, hh, :, hh, :] for hh in range(N_HEADS)], axis=1)


def _hist8(hist):
    return jnp.pad(hist, ((0, 0), (SUBLANE - hist.shape[1], 0), (0, 0)))


def _rope_table(pos):
    half = A_ROPE // 2
    inv = jnp.power(ROPE_BASE, -jnp.arange(half, dtype=F32) / half)
    ang = pos.astype(F32)[:, None] * inv
    cos, sin = jnp.cos(ang), jnp.sin(ang)
    cos2 = jnp.concatenate([cos, cos], axis=-1)
    sin2 = jnp.concatenate([-sin, sin], axis=-1)
    n = pos.shape[0]
    one = jnp.ones((n, A_NOPE), F32)
    z = lambda w: jnp.zeros((n, w), F32)
    tabq = jnp.concatenate([one, cos2, z(32), z(64), sin2, z(32)], axis=-1).T
    tabk = jnp.concatenate([cos2, z(96), sin2, z(96)], axis=-1)
    return tabq, tabk


def _trunk(x, pos, front, st, P, *, tm, tm_ffn, c, tq, causal, n_keys_pad):
    b, l, _ = x.shape
    tabq, tabk = _rope_table(pos)
    new = {name: [] for name in ('ckv', 'krope', 'rwkv_S', 'rwkv_shift', 'ssd_S', 'ssd_conv',
                                 'gdn_S', 'gdn_conv', 'ffn_conv')}
    zeros_bd = jnp.zeros((b, GROUP_W, GROUP_W), F32)
    for li in range(DEPTH):
        c_ = P[li]
        qt, ckv, krope = _mla_prep(x, c_['g1'], c_['wa'], c_['gq'], c_['wq1t'], c_['wq2t'], c_['gkv'],
                                   tabq, tabk, tm, front)
        if st is None:
            c_all, kr_all, klo, khi = ckv, krope, front, l
            s_b = s_c = s_d = zeros_bd
            shift8 = jnp.zeros((b, SUBLANE, B_COLS), F32)
            chist = jnp.zeros((b, SUBLANE, 768), F32)
            dhist = jnp.zeros((b, SUBLANE, 768), F32)
            fhist = jnp.zeros((b, SUBLANE, 2 * D_FF), F32)
        else:
            past = st['ckv'].shape[2]
            padk = n_keys_pad - past - l
            c_all = jnp.concatenate([st['ckv'][li], ckv, jnp.zeros((b, padk, A_KVRANK), F32)], axis=1)
            kr_all = jnp.concatenate([st['krope'][li], krope, jnp.zeros((b, padk, A_ROPE), F32)], axis=1)
            klo, khi = 0, past + l
            s_b = _embed_bd(st['rwkv_S'][li])
            s_c = _embed_bd(jnp.swapaxes(st['ssd_S'][li], -1, -2))
            s_d = _embed_bd(st['gdn_S'][li])
            shift8 = _hist8(st['rwkv_shift'][li][:, None, :])
            chist = _hist8(st['ssd_conv'][li][:, :, _IDX['xbc_exp']])
            dhist = _hist8(st['gdn_conv'][li])
            fhist = _hist8(st['ffn_conv'][li][:, :, _IDX['ffn_perm']])
        nk = c_all.shape[1]
        tkv = tq if causal else nk
        kf, vt = _kv_up(c_all, kr_all, c_['wk'], c_['ek'], c_['wvt'], c_['ones_col'], tkv)
        ya = _flash(qt, kf, vt, c_['gout_col'], tq, tkv, causal, klo, khi)
        yb, sb_new, shift_new = _scan_call(
            _rwkv_kernel, "rwkv7", x, (c_['g1'], c_['wb']), shift8, s_b,
            (c_['mu'], c_['w0'], c_['wl'], c_['a0'], c_['g2b'], c_['kk'], c_['ka'], c_['rk'],
             c_['gnw'], c_['gnb']), B_COLS, tm, c, front, SCAN_ROWS)
        yc, sc_new, chist_new = _scan_call(
            _ssd_kernel, "ssd", x, (c_['g1'], c_['wc']), chist, s_c,
            (c_['c_cw'], c_['c_cb'], c_['c_dtb'], c_['c_alog'], c_['c_d'], c_['c_gn']),
            768, tm, c, front, SCAN_ROWS)
        yd, sd_new, dhist_new = _scan_call(
            _gdn_kernel, "gdn", x, (c_['g1'], c_['wd']), dhist, s_d,
            (c_['d_cw'], c_['d_alog'], c_['d_dtb'], c_['d_gn']), 768, tm, c, front, SCAN_ROWS)
        x, fhist_new = _ffn(x, ya, yb, yc, yd, c_['wo'], c_['g2'], c_['wup'], c_['f_cw'], c_['wdn'],
                            fhist, tm_ffn, front)
        new['ckv'].append(ckv[:, front:])
        new['krope'].append(krope[:, front:])
        new['rwkv_S'].append(_extract_bd(sb_new))
        new['rwkv_shift'].append(shift_new[:, SUBLANE - 1])
        new['ssd_S'].append(jnp.swapaxes(_extract_bd(sc_new), -1, -2))
        new['ssd_conv'].append(chist_new[:, SUBLANE - (C_CONV - 1):][:, :, _IDX['xbc_back']])
        new['gdn_S'].append(_extract_bd(sd_new))
        new['gdn_conv'].append(dhist_new[:, SUBLANE - (D_CONV - 1):])
        new['ffn_conv'].append(fhist_new[:, SUBLANE - (FFN_CONV - 1):][:, :, _IDX['ffn_back']])
    return x, {name: jnp.stack(vals) for name, vals in new.items()}


def kernel(x_prompt, x_sample, cache_mla_ckv, cache_mla_krope, state_rwkv, state_rwkv_shift, state_ssd, state_ssd_conv, state_gdn, state_gdn_conv, state_ffn_conv, meta_tokens, norm1_g, w_in, a_gq, a_wuq, a_gkv, a_wuk, a_wuv, a_gout, b_mu, b_w0, b_w2, b_a0, b_a2, b_g2, b_kk, b_ka, b_rk, b_gnw, b_gnb, c_convw, c_convb, c_dtb, c_alog, c_d, c_gnorm, d_convw, d_alog, d_dtb, d_gnorm, w_out, norm2_g, f_wup, f_convw, f_wdown, final_g):
    P = dict(norm1_g=norm1_g, w_in=w_in, a_gq=a_gq, a_wuq=a_wuq, a_gkv=a_gkv, a_wuk=a_wuk,
             a_wuv=a_wuv, a_gout=a_gout, b_mu=b_mu, b_w0=b_w0, b_w2=b_w2, b_a0=b_a0, b_a2=b_a2,
             b_g2=b_g2, b_kk=b_kk, b_ka=b_ka, b_rk=b_rk, b_gnw=b_gnw, b_gnb=b_gnb,
             c_convw=c_convw, c_convb=c_convb, c_dtb=c_dtb, c_alog=c_alog, c_d=c_d, c_gnorm=c_gnorm,
             d_convw=d_convw, d_alog=d_alog, d_dtb=d_dtb, d_gnorm=d_gnorm,
             w_out=w_out, norm2_g=norm2_g, f_wup=f_wup, f_convw=f_convw, f_wdown=f_wdown)
    P = [_layer_consts(P, li) for li in range(DEPTH)]
    fin = final_g.reshape(1, -1).astype(F32)
    b_p, seq, _ = x_prompt.shape
    lx = N_META + seq
    assert seq % ATT_BLOCK == 0 and N_META <= ATT_BLOCK
    front = ATT_BLOCK - N_META
    l_pad = front + lx
    meta = jnp.broadcast_to(meta_tokens[None].astype(F32), (b_p, N_META, D_MODEL))
    x_ext = jnp.concatenate([jnp.zeros((b_p, front, D_MODEL), F32), meta, x_prompt], axis=1)
    pos_p = jnp.arange(l_pad, dtype=jnp.int32) - (front + N_META)
    tm_ffn = max(t for t in (ROW_TILE, 2 * ROW_TILE, 3 * ROW_TILE) if l_pad % t == 0)
    y_p, ns_p = _trunk(x_ext, pos_p, front, None, P, tm=ROW_TILE, tm_ffn=tm_ffn, c=CHUNK, tq=ATT_BLOCK,
                       causal=True, n_keys_pad=l_pad)
    y_prompt = _final_norm(y_p, fin, ROW_TILE, (front + N_META) // ROW_TILE, seq)
    b_s, t_s, _ = x_sample.shape
    past = cache_mla_ckv.shape[2]
    assert t_s <= CHUNK and t_s % 16 == 0 and (t_s & (t_s - 1)) == 0
    st_s = dict(ckv=cache_mla_ckv, krope=cache_mla_krope, rwkv_S=state_rwkv, rwkv_shift=state_rwkv_shift,
                ssd_S=state_ssd, ssd_conv=state_ssd_conv, gdn_S=state_gdn, gdn_conv=state_gdn_conv,
                ffn_conv=state_ffn_conv)
    pos_s = past + jnp.arange(t_s, dtype=jnp.int32)
    n_keys_pad = -(-(past + t_s) // LANE) * LANE
    y_s, ns_s = _trunk(x_sample, pos_s, 0, st_s, P, tm=t_s, tm_ffn=t_s, c=t_s, tq=t_s, causal=False,
                       n_keys_pad=n_keys_pad)
    y_sample = _final_norm(y_s, fin, t_s, 0, t_s)
    keys = ('ckv', 'krope', 'rwkv_S', 'rwkv_shift', 'ssd_S', 'ssd_conv', 'gdn_S', 'gdn_conv', 'ffn_conv')
    return (y_prompt, y_sample) + tuple(ns_p[k] for k in keys) + tuple(ns_s[k] for k in keys)
```

```python
import functools
import math

import numpy as np
import jax
import jax.numpy as jnp
from jax import lax
from jax.experimental import pallas as pl
from jax.experimental.pallas import tpu as pltpu

F32 = jnp.float32
BF16 = jnp.bfloat16

D_MODEL = 1024
DEPTH = 4
CHUNK = 64
N_META = 16
EPS = 1e-6
L2_EPS = 1e-6
GROUP_W = 256
N_HEADS = 4
HEAD_W = 64
A_NOPE = 64
A_ROPE = 32
A_QRANK = 192
A_KVRANK = 128
A_SCALE = (A_NOPE + A_ROPE) ** -0.5
ROPE_BASE = 10000.0
B_GN_EPS = 64e-5
B_COLS = 1024
C_CONV = 4
D_CONV = 4
D_FF = 2816
FFN_CONV = 3
A_COLS = 352
C_COLS = 772
D_COLS = 1032
IN_COLS = A_COLS + B_COLS + C_COLS + D_COLS

LANE = 128
SUBLANE = 8
VMEM_LIMIT = 56 * 1024 * 1024
NEG = -1e30
LOG2E = math.log2(math.e)
FFN_CW = 256
ATT_BLOCK = 256
ROW_TILE = 256
SCAN_ROWS = 2


def _mm(a, b):
    return jnp.dot(a.astype(BF16), b.astype(BF16), preferred_element_type=F32)


def _mm_nt(a, b):
    return lax.dot_general(a.astype(BF16), b.astype(BF16), (((1,), (1,)), ((), ())),
                           preferred_element_type=F32)


def _mm_tn(a, b):
    return lax.dot_general(a.astype(BF16), b.astype(BF16), (((0,), (0,)), ((), ())),
                           preferred_element_type=F32)


def _split2(x):
    hi = x.astype(BF16)
    lo = (x - hi.astype(F32)).astype(BF16)
    return hi, lo


def _mm_x2(x, w):
    hi, lo = _split2(x)
    return (jnp.dot(hi, w, preferred_element_type=F32)
            + jnp.dot(lo, w, preferred_element_type=F32))


def _rms(x, g, n):
    ms = jnp.sum(x * x, axis=-1, keepdims=True) * (1.0 / n)
    return x * lax.rsqrt(ms + EPS) * g


def _softplus(x):
    return jnp.maximum(x, 0.0) + jnp.log1p(jnp.exp(-jnp.abs(x)))


def _sigmoid(x):
    return 1.0 / (1.0 + jnp.exp(-x))


def _silu(x):
    return x * _sigmoid(x)


def _iota(shape, axis):
    return lax.broadcasted_iota(jnp.int32, shape, axis)


def _norm_in(x_ref, g_ref, j, tm, front):
    h = _rms(x_ref[0], g_ref[...], D_MODEL)
    if front > 0:
        rows = j * tm + _iota((tm, 1), 0)
        h = jnp.where(rows >= front, h, 0.0)
    return h.astype(BF16)


def _norm_in_rows(x_ref, g_ref, j, tm, front):
    x = x_ref[...]
    h = _rms(x, g_ref[...], D_MODEL)
    if front > 0:
        pos = j * tm + _iota((1, tm, 1), 1)
        h = jnp.where(pos >= front, h, 0.0)
    return h.reshape(x.shape[0] * tm, D_MODEL).astype(BF16)


def _hsum(x, bd):
    return _mm_x2(x, bd)


TRI_INCL, TRI_STRICT, TRI_EYE, TRI_JOIN = 0, 1, 2, 3


class _Stk:
    def __init__(self, c, hm_ref, tri_ref, tl_ref):
        self.c = c
        self.n = N_HEADS * c
        self.steps = int(math.log2(c))
        assert 1 << self.steps == c
        self.hm_ref, self.tri_ref, self.tl_ref = hm_ref, tri_ref, tl_ref

    def mask(self, which):
        return self.tri_ref[which]

    def tile(self, x):
        return jnp.concatenate([x] * N_HEADS, axis=0)

    def stack(self, x):
        return self.tile(x) * self.hm_ref[...]

    def unstack(self, xs):
        c = self.c
        return xs[0:c] + xs[c:2 * c] + xs[2 * c:3 * c] + xs[3 * c:4 * c]

    def cumsum(self, x):
        hi, lo = _split2(x)
        tl = self.tl_ref[...]
        return jnp.dot(tl, hi, preferred_element_type=F32) + jnp.dot(tl, lo, preferred_element_type=F32)

    def decay(self, g_cum):
        gs = self.stack(g_cum)
        gcol = jnp.min(gs, axis=-1, keepdims=True)
        hi, lo = _split2(gs)
        o2 = jnp.full((self.n, GROUP_W), 1.0 / HEAD_W, BF16)
        grow = _mm_nt(o2, hi) + _mm_nt(o2, lo)
        return jnp.exp(jnp.minimum(gcol - grow, 0.0)) * self.mask(TRI_INCL)

    def tri_inv_many(self, lowers):
        eye = self.mask(TRI_EYE)
        minvs = [eye - lw * self.mask(TRI_JOIN) for lw in lowers]
        for lb in range(1, self.steps):
            ts = [_mm(mi, lw * self.mask(TRI_JOIN + lb)) for mi, lw in zip(minvs, lowers)]
            minvs = [mi - _mm(t, mi) for t, mi in zip(ts, minvs)]
        return minvs


def _mla_prep_kernel(x_ref, g1_ref, wa_ref, gq_ref, wq1t_ref, wq2t_ref, gkv_ref, tabq_ref, tabk_ref,
                     qt_out, ckv_out, kr_out, *, tm, front):
    j = pl.program_id(1)
    h = _norm_in(x_ref, g1_ref, j, tm, front)
    pa = jnp.dot(h, wa_ref[...], preferred_element_type=F32)
    qn = _rms(pa[:, 0:256], gq_ref[...], A_QRANK)
    q1t = _mm_nt(wq1t_ref[...], qn)
    q2t = _mm_nt(wq2t_ref[...], qn)
    tabq = tabq_ref[...]
    cos4 = jnp.concatenate([tabq[0:128]] * N_HEADS, axis=0)
    sin4 = jnp.concatenate([tabq[128:256]] * N_HEADS, axis=0)
    qt_out[0] = ((q1t * cos4 + q2t * sin4) * (A_SCALE * LOG2E)).astype(BF16)
    ckv_out[0] = _rms(pa[:, 256:384], gkv_ref[...], A_KVRANK)
    tabk = tabk_ref[...]
    kr = pa[:, 384:512] * tabk[:, 0:128] + pa[:, 512:640] * tabk[:, 128:256]
    kr_out[0] = kr[:, 0:A_ROPE]


def _kv_up_kernel(c_ref, kr_ref, wk_ref, ek_ref, wvt_ref, ones_ref, k_out, vt_out):
    c = c_ref[0].astype(BF16)
    kr = kr_ref[0].astype(BF16)
    k = (jnp.dot(c, wk_ref[...], preferred_element_type=F32)
         + jnp.dot(kr, ek_ref[...], preferred_element_type=F32))
    k_out[0] = k.astype(BF16)
    vt_out[0, 0] = (_mm_nt(wvt_ref[...], c) + ones_ref[...]).astype(BF16)


def _flash_kernel(qt_ref, k_ref, vt_ref, gout_ref, o_ref, m_sc, acc_sc, sa_sc, sb_sc,
                  *, tq, tk, nkv, causal, klo, khi):
    i = pl.program_id(1)
    m_sc[...] = jnp.full((N_HEADS * SUBLANE, tq), NEG, F32)
    acc_sc[...] = jnp.zeros((N_HEADS * LANE, tq), F32)
    heads = [slice(LANE * h, LANE * (h + 1)) for h in range(N_HEADS)]
    s_bufs = (sa_sc, sb_sc)

    def produce(jb, slot):
        start = pl.multiple_of(jb * tk, tk)
        for h, rows in enumerate(heads):
            s_bufs[slot][h * tk:(h + 1) * tk, :] = jnp.dot(
                k_ref[0, pl.ds(start, tk), rows], qt_ref[0, rows, :], preferred_element_type=F32)

    def consume(jb, slot, masked, nxt):
        if nxt is not None:
            produce(*nxt)
        if masked:
            kpos = jb * tk + _iota((tk, 1), 0)
            qpos = i * tq + _iota((1, tq), 1)
            vis = (kpos >= klo) & (kpos < khi)
            if causal:
                vis = vis & ((kpos >> 6) <= (qpos >> 6))
        m_all = m_sc[...]
        acc_all = acc_sc[...]
        m_out, alphas, ps = [], [], []
        for h in range(N_HEADS):
            s = s_bufs[slot][h * tk:(h + 1) * tk, :]
            if masked:
                s = jnp.where(vis, s, NEG)
            m_prev = m_all[SUBLANE * h:SUBLANE * (h + 1)]
            m_new = jnp.maximum(m_prev, jnp.max(s, axis=0, keepdims=True))
            alphas.append(jnp.exp2(m_prev[0:1] - m_new[0:1]))
            ps.append(jnp.exp2(s - m_new[0:1]).astype(BF16))
            m_out.append(m_new)
        acc_out = [alphas[h] * acc_all[rows]
                   + jnp.dot(vt_ref[0, jb, rows, :], ps[h], preferred_element_type=F32)
                   for h, rows in enumerate(heads)]
        m_sc[...] = jnp.concatenate(m_out, axis=0)
        acc_sc[...] = jnp.concatenate(acc_out, axis=0)

    produce(0, 0)
    if causal:
        @pl.when(i == 0)
        def _():
            consume(0, 0, True, None)

        @pl.when(i > 0)
        def _():
            consume(0, 0, True, (1, 1))
            quads = (i - 1) >> 2

            def pair(jb):
                consume(jb, 1, False, (jb + 1, 0))
                consume(jb + 1, 0, False, (jb + 2, 1))

            def body(t, carry):
                pair(1 + 4 * t)
                pair(3 + 4 * t)
                return carry

            lax.fori_loop(0, quads, body, 0)

            @pl.when(((i - 1) & 2) != 0)
            def _():
                pair(1 + 4 * quads)

            @pl.when((i & 1) == 0)
            def _():
                consume(i - 1, 1, False, (i, 0))
                consume(i, 0, True, None)

            @pl.when((i & 1) == 1)
            def _():
                consume(i, 1, True, None)
    else:
        for jb in range(nkv):
            consume(jb, jb & 1, True, (jb + 1, (jb + 1) & 1) if jb + 1 < nkv else None)
    outs = []
    for h in range(N_HEADS):
        a = acc_sc[LANE * h:LANE * (h + 1), :]
        outs.append(a[0:HEAD_W] / a[HEAD_W:HEAD_W + 1])
    yat = jnp.concatenate(outs, axis=0)
    ms = jnp.sum(yat * yat, axis=0, keepdims=True) * (1.0 / GROUP_W)
    o_ref[0] = yat * lax.rsqrt(ms + EPS) * gout_ref[...]


def _chunk_ids(grp, tm, c):
    ids = [(g, ci) for ci in range(tm // c) for g in range(grp)]
    return ids, {(g, ci): slice(g * tm + ci * c, g * tm + (ci + 1) * c) for g, ci in ids}


def _carry_rows(work, g, new_rows, tm):
    work[g, SUBLANE:SUBLANE + tm, :] = new_rows
    return work[g, tm:tm + SUBLANE, :]


def _rwkv_kernel(x_ref, g1_ref, wb_ref, shift_ref, s0_ref, mu_ref, w0_ref, wl_ref, a0_ref,
                 g2_ref, kk_ref, ka_ref, rk_ref, gnw_ref, gnb_ref, hm_ref, tri_ref, tl_ref, bd_ref,
                 y_out, s_out, shift_out, work, st, *, tm, c, front, grp):
    j = pl.program_id(1)

    @pl.when(j == 0)
    def _():
        work[:, 0:SUBLANE, :] = shift_ref[...]
        st[...] = s0_ref[...]

    h = _norm_in_rows(x_ref, g1_ref, j, tm, front)
    cols = jnp.dot(h, wb_ref[...], preferred_element_type=F32)
    shifted = []
    for g in range(grp):
        tail = _carry_rows(work, g, cols[g * tm:(g + 1) * tm], tm)
        shifted.append(work[g, pl.ds(SUBLANE - 1, tm), :])
        work[g, 0:SUBLANE, :] = tail
        shift_out[g] = tail
    shifted = jnp.concatenate(shifted, axis=0)
    xm = cols + (shifted - cols) * mu_ref[...]
    r = xm[:, 0:256]
    k = xm[:, 256:512]
    v = xm[:, 512:768]
    lora = xm[:, 768:896]
    dg = xm[:, 896:1024]
    lora = jnp.where(_iota((grp * tm, LANE), 1) < 64, jnp.tanh(lora), lora)
    ll = _mm(lora, wl_ref[...])
    w_log = -_softplus(-(w0_ref[...] + ll[:, 0:256])) - 0.5
    logd = -jnp.exp(w_log)
    a = _sigmoid(a0_ref[...] + ll[:, 256:512])
    g_gate = _mm(_sigmoid(dg), g2_ref[...])
    bd = bd_ref[...]
    kkr = k * kk_ref[...]
    kk = kkr * lax.rsqrt(_hsum(kkr * kkr, bd) + L2_EPS)
    k2 = k * (1.0 + (a - 1.0) * ka_ref[...])

    sk = _Stk(c, hm_ref, tri_ref, tl_ref)
    ids, rows = _chunk_ids(grp, tm, c)
    strict, incl = sk.mask(TRI_STRICT), sk.mask(TRI_INCL)
    gcs = {i: sk.cumsum(logd[rows[i]]) for i in ids}
    pre = {}
    for i in ids:
        gc, ld = gcs[i], logd[rows[i]]
        eg, eng = jnp.exp(gc), jnp.exp(-gc)
        bt = kk[rows[i]] * a[rows[i]] * eng
        kt = k2[rows[i]] * eng
        dc = eg[c - 1:c, :]
        pre[i] = dict(at_s=sk.stack(-kk[rows[i]] * jnp.exp(gc - ld)), rt_s=sk.stack(r[rows[i]] * eg),
                      v_s=sk.stack(v[rows[i]]), bt_t=sk.tile(bt), kt_t=sk.tile(kt), dc=dc,
                      bd_s=sk.stack(bt * dc), kd_s=sk.stack(kt * dc))
    lab = {i: _mm_nt(pre[i]['at_s'], pre[i]['bt_t']) * strict for i in ids}
    aak = {i: _mm_nt(pre[i]['at_s'], pre[i]['kt_t']) * strict for i in ids}
    arb = {i: _mm_nt(pre[i]['rt_s'], pre[i]['bt_t']) * incl for i in ids}
    ark = {i: _mm_nt(pre[i]['rt_s'], pre[i]['kt_t']) * incl for i in ids}
    minv = dict(zip(ids, sk.tri_inv_many([-lab[i] for i in ids])))
    a2 = {i: _mm(aak[i], pre[i]['v_s']) for i in ids}
    u0 = {i: _mm(minv[i], a2[i]) for i in ids}
    m1 = {i: _mm(minv[i], pre[i]['at_s']) for i in ids}
    ork = {i: _mm(ark[i], pre[i]['v_s']) for i in ids}
    skv = {i: _mm_tn(pre[i]['v_s'], pre[i]['kd_s']) for i in ids}

    s = [st[g] for g in range(grp)]
    outs = {}
    for ci in range(tm // c):
        sb = [s[g].astype(BF16) for g in range(grp)]
        us = [u0[(g, ci)] + _mm_nt(m1[(g, ci)], sb[g]) for g in range(grp)]
        oq = [_mm_nt(pre[(g, ci)]['rt_s'], sb[g]) for g in range(grp)]
        for g in range(grp):
            i = (g, ci)
            s[g] = s[g] * pre[i]['dc'] + _mm_tn(us[g], pre[i]['bd_s']) + skv[i]
            outs[i] = sk.unstack(oq[g] + _mm(arb[i], us[g]) + ork[i])
    for g in range(grp):
        st[g] = s[g]
        s_out[g] = s[g]
    o = jnp.concatenate([outs[(g, ci)] for g in range(grp) for ci in range(tm // c)], axis=0)
    mean = _hsum(o, bd) * (1.0 / HEAD_W)
    d = o - mean
    var = _hsum(d * d, bd) * (1.0 / HEAD_W)
    o = d * lax.rsqrt(var + B_GN_EPS) * gnw_ref[...] + gnb_ref[...]
    bonus = _hsum(r * k2 * rk_ref[...], bd) * v
    y_out[...] = ((o + bonus) * g_gate).reshape(grp, tm, GROUP_W)


def _conv4_rows(work, new, wv, hist_out, grp, tm):
    ys = []
    for g in range(grp):
        tail = _carry_rows(work, g, new[g * tm:(g + 1) * tm], tm)
        y = work[g, pl.ds(SUBLANE - 3, tm), :] * wv[0:1, :]
        y = y + work[g, pl.ds(SUBLANE - 2, tm), :] * wv[1:2, :]
        y = y + work[g, pl.ds(SUBLANE - 1, tm), :] * wv[2:3, :]
        ys.append(y + work[g, pl.ds(SUBLANE, tm), :] * wv[3:4, :])
        work[g, 0:SUBLANE, :] = tail
        hist_out[g] = tail
    return jnp.concatenate(ys, axis=0)


def _ssd_kernel(x_ref, g1_ref, wc_ref, hist_ref, s0_ref, cw_ref, cb_ref, dtb_ref, alog_ref,
                dskip_ref, gn_ref, hm_ref, tri_ref, tl_ref, bd_ref,
                y_out, s_out, hist_out, work, st, *, tm, c, front, grp):
    j = pl.program_id(1)

    @pl.when(j == 0)
    def _():
        work[:, 0:SUBLANE, :] = hist_ref[...]
        st[...] = s0_ref[...]

    h = _norm_in_rows(x_ref, g1_ref, j, tm, front)
    pc = jnp.dot(h, wc_ref[...], preferred_element_type=F32)
    z = pc[:, 0:256]
    xbc = _silu(_conv4_rows(work, pc[:, 256:1024], cw_ref[...], hist_out, grp, tm) + cb_ref[...])
    xs = xbc[:, 0:256]
    bm = xbc[:, 256:512]
    cm = xbc[:, 512:768]
    dt = _softplus(pc[:, 1024:1280] + dtb_ref[...])
    if front > 0:
        pos = j * tm + _iota((1, tm, 1), 1)
        dt = jnp.where(pos >= front, dt.reshape(grp, tm, GROUP_W), 0.0).reshape(grp * tm, GROUP_W)
    a = dt * (-jnp.exp(alog_ref[...]))
    xdt = xs * dt

    sk = _Stk(c, hm_ref, tri_ref, tl_ref)
    ids, rows = _chunk_ids(grp, tm, c)
    acs = {i: sk.cumsum(a[rows[i]]) for i in ids}
    dms = {i: sk.decay(acs[i]) for i in ids}
    xdt_s = {i: sk.stack(xdt[rows[i]]) for i in ids}
    amat = {i: _mm_nt(sk.stack(cm[rows[i]]), sk.tile(bm[rows[i]])) * dms[i] for i in ids}
    ydiag = {i: _mm(amat[i], xdt_s[i]) for i in ids}
    sx = {i: _mm_tn(sk.stack(bm[rows[i]] * jnp.exp(acs[i][c - 1:c, :] - acs[i])), xdt_s[i]) for i in ids}
    ce_s = {i: sk.stack(cm[rows[i]] * jnp.exp(acs[i])) for i in ids}

    s = [st[g] for g in range(grp)]
    outs = {}
    for ci in range(tm // c):
        for g in range(grp):
            i = (g, ci)
            outs[i] = sk.unstack(ydiag[i] + _mm(ce_s[i], s[g]))
            s[g] = s[g] * jnp.exp(acs[i][c - 1:c, :]) + sx[i]
    for g in range(grp):
        st[g] = s[g]
        s_out[g] = s[g]
    y = jnp.concatenate([outs[(g, ci)] for g in range(grp) for ci in range(tm // c)], axis=0)
    y = y + dskip_ref[...] * xs
    y_out[...] = _rms(y * _silu(z), gn_ref[...], GROUP_W).reshape(grp, tm, GROUP_W)


def _gdn_kernel(x_ref, g1_ref, wd_ref, hist_ref, s0_ref, cw_ref, alog_ref, dtb_ref, gn_ref,
                hm_ref, tri_ref, tl_ref, bd_ref,
                y_out, s_out, hist_out, work, st, *, tm, c, front, grp):
    j = pl.program_id(1)

    @pl.when(j == 0)
    def _():
        work[:, 0:SUBLANE, :] = hist_ref[...]
        st[...] = s0_ref[...]

    h = _norm_in_rows(x_ref, g1_ref, j, tm, front)
    pd = jnp.dot(h, wd_ref[...], preferred_element_type=F32)
    qkv = _silu(_conv4_rows(work, pd[:, 0:768], cw_ref[...], hist_out, grp, tm))
    z = pd[:, 768:1024]
    beta = _sigmoid(pd[:, 1024:1280])
    g_log = -jnp.exp(alog_ref[...]) * _softplus(pd[:, 1280:1536] + dtb_ref[...])
    bd = bd_ref[...]
    q = qkv[:, 0:256]
    k = qkv[:, 256:512]
    v = qkv[:, 512:768]
    q = q * lax.rsqrt(_hsum(q * q, bd) + L2_EPS) * (HEAD_W ** -0.5)
    k = k * lax.rsqrt(_hsum(k * k, bd) + L2_EPS)

    sk = _Stk(c, hm_ref, tri_ref, tl_ref)
    ids, rows = _chunk_ids(grp, tm, c)
    strict = sk.mask(TRI_STRICT)
    gcs = {i: sk.cumsum(g_log[rows[i]]) for i in ids}
    dms = {i: sk.decay(gcs[i]) for i in ids}
    kb = {i: k[rows[i]] * beta[rows[i]] for i in ids}
    k_t = {i: sk.tile(k[rows[i]]) for i in ids}
    lower = {i: _mm_nt(sk.stack(kb[i]), k_t[i]) * dms[i] * strict for i in ids}
    aqk = {i: _mm_nt(sk.stack(q[rows[i]]), k_t[i]) * dms[i] for i in ids}
    tinv = dict(zip(ids, sk.tri_inv_many([lower[i] for i in ids])))
    u = {i: _mm(tinv[i], sk.stack(v[rows[i]] * beta[rows[i]])) for i in ids}
    w = {i: _mm(tinv[i], sk.stack(kb[i] * jnp.exp(gcs[i]))) for i in ids}
    qe_s = {i: sk.stack(q[rows[i]] * jnp.exp(gcs[i])) for i in ids}
    kd_s = {i: sk.stack(k[rows[i]] * jnp.exp(gcs[i][c - 1:c, :] - gcs[i])) for i in ids}

    s = [st[g] for g in range(grp)]
    outs = {}
    for ci in range(tm // c):
        sb = [s[g].astype(BF16) for g in range(grp)]
        vn = [u[(g, ci)] - _mm(w[(g, ci)], sb[g]) for g in range(grp)]
        oq = [_mm(qe_s[(g, ci)], sb[g]) for g in range(grp)]
        for g in range(grp):
            i = (g, ci)
            s[g] = s[g] * jnp.exp(gcs[i][c - 1:c, :]) + _mm_tn(kd_s[i], vn[g])
            outs[i] = sk.unstack(oq[g] + _mm(aqk[i], vn[g]))
    for g in range(grp):
        st[g] = s[g]
        s_out[g] = s[g]
    o = jnp.concatenate([outs[(g, ci)] for g in range(grp) for ci in range(tm // c)], axis=0)
    ms = _hsum(o * o, bd) * (1.0 / HEAD_W)
    y_out[...] = (o * lax.rsqrt(ms + EPS) * gn_ref[...] * _silu(z)).reshape(grp, tm, GROUP_W)


def _ffn_kernel(x_ref, ya_ref, yb_ref, yc_ref, yd_ref, wo_ref, g2_ref, wup_ref, cw_ref, wdn_ref,
                hist_ref, xo_ref, hist_out, carry, work, *, tm, front):
    j = pl.program_id(1)

    @pl.when(j == 0)
    def _():
        carry[...] = hist_ref[0]

    x = x_ref[0] + _mm_tn(ya_ref[0], wo_ref[0:GROUP_W, :])
    for idx, y_ref in ((1, yb_ref), (2, yc_ref), (3, yd_ref)):
        x = x + jnp.dot(y_ref[0].astype(BF16), wo_ref[GROUP_W * idx:GROUP_W * (idx + 1), :],
                        preferred_element_type=F32)
    h2 = _rms(x, g2_ref[...], D_MODEL)
    if front > 0:
        rows = j * tm + _iota((tm, 1), 0)
        h2 = jnp.where(rows >= front, h2, 0.0)
    h2 = h2.astype(BF16)
    acc = jnp.zeros((tm, D_MODEL), F32)
    w2 = 2 * FFN_CW
    for f in range(D_FF // FFN_CW):
        cols = slice(f * w2, (f + 1) * w2)
        u = jnp.dot(h2, wup_ref[:, cols], preferred_element_type=F32)
        work[0:SUBLANE, :] = carry[:, cols]
        work[SUBLANE:SUBLANE + tm, :] = u
        cw = cw_ref[:, cols]
        y = (work[pl.ds(SUBLANE - 2, tm), :] * cw[0:1, :]
             + work[pl.ds(SUBLANE - 1, tm), :] * cw[1:2, :] + u * cw[2:3, :])
        carry[:, cols] = work[tm:tm + SUBLANE, :]
        act = _silu(y[:, 0:FFN_CW]) * y[:, FFN_CW:w2]
        acc = acc + jnp.dot(act.astype(BF16), wdn_ref[f * FFN_CW:(f + 1) * FFN_CW, :],
                            preferred_element_type=F32)
    xo_ref[0] = x + acc
    hist_out[0] = carry[...]


def _final_norm_kernel(x_ref, g_ref, o_ref):
    o_ref[0] = _rms(x_ref[0], g_ref[...], D_MODEL)


def _const_spec(arr):
    nd = arr.ndim
    return pl.BlockSpec(arr.shape, lambda b, j: (0,) * nd, pipeline_mode=pl.Buffered(1))


def _tile_spec(tm, width):
    return pl.BlockSpec((1, tm, width), lambda b, j: (b, j, 0))


def _batch_spec(rows, width):
    return pl.BlockSpec((1, rows, width), lambda b, j: (b, 0, 0))


def _params():
    return pltpu.CompilerParams(dimension_semantics=("arbitrary", "arbitrary"),
                                vmem_limit_bytes=VMEM_LIMIT)


def _cols_spec(rows, tm):
    return pl.BlockSpec((1, rows, tm), lambda b, j: (b, 0, j))


def _mla_prep(x, g1, wa, gq, wq1t, wq2t, gkv, tabq, tabk, tm, front):
    b, l, _ = x.shape
    consts = (g1, wa, gq, wq1t, wq2t, gkv)
    return pl.pallas_call(
        functools.partial(_mla_prep_kernel, tm=tm, front=front),
        grid=(b, l // tm),
        in_specs=[_tile_spec(tm, D_MODEL)] + [_const_spec(a) for a in consts]
        + [pl.BlockSpec((2 * LANE, tm), lambda bb, j: (0, j)),
           pl.BlockSpec((tm, 2 * LANE), lambda bb, j: (j, 0))],
        out_specs=[_cols_spec(512, tm), _tile_spec(tm, A_KVRANK), _tile_spec(tm, A_ROPE)],
        out_shape=[jax.ShapeDtypeStruct((b, 512, l), BF16),
                   jax.ShapeDtypeStruct((b, l, A_KVRANK), F32),
                   jax.ShapeDtypeStruct((b, l, A_ROPE), F32)],
        compiler_params=_params(), name="mla_prep",
    )(x, *consts, tabq, tabk)


def _kv_up(c_all, kr_all, wk, ek, wvt, ones_col, tm):
    b, n, _ = c_all.shape
    consts = (wk, ek, wvt, ones_col)
    return pl.pallas_call(
        _kv_up_kernel,
        grid=(b, n // tm),
        in_specs=[_tile_spec(tm, A_KVRANK), _tile_spec(tm, A_ROPE)] + [_const_spec(a) for a in consts],
        out_specs=[_tile_spec(tm, 512), pl.BlockSpec((1, 1, 512, tm), lambda bb, j: (bb, j, 0, 0))],
        out_shape=[jax.ShapeDtypeStruct((b, n, 512), BF16),
                   jax.ShapeDtypeStruct((b, n // tm, 512, tm), BF16)],
        compiler_params=_params(), name="kv_up",
    )(c_all, kr_all, *consts)


def _flash(qt, k, vt, gout_col, tq, tk, causal, klo, khi):
    b, _, l = qt.shape
    n = k.shape[1]
    nkv = n // tk
    return pl.pallas_call(
        functools.partial(_flash_kernel, tq=tq, tk=tk, nkv=nkv, causal=causal, klo=klo, khi=khi),
        grid=(b, l // tq),
        in_specs=[_cols_spec(512, tq),
                  pl.BlockSpec((1, n, 512), lambda bb, j: (bb, 0, 0), pipeline_mode=pl.Buffered(1)),
                  pl.BlockSpec((1, nkv, 512, tk), lambda bb, j: (bb, 0, 0, 0), pipeline_mode=pl.Buffered(1)),
                  _const_spec(gout_col)],
        out_specs=_cols_spec(GROUP_W, tq),
        out_shape=jax.ShapeDtypeStruct((b, GROUP_W, l), F32),
        scratch_shapes=[pltpu.VMEM((N_HEADS * SUBLANE, tq), F32), pltpu.VMEM((N_HEADS * LANE, tq), F32),
                        pltpu.VMEM((N_HEADS * tk, tq), F32), pltpu.VMEM((N_HEADS * tk, tq), F32)],
        compiler_params=_params(), name="mla_flash",
    )(qt, k, vt, gout_col)


def _stack_consts(c):
    n = N_HEADS * c
    lc = int(math.log2(c))
    r, l = np.arange(n)[:, None], np.arange(GROUP_W)[None, :]
    hm = ((r >> lc) == (l >> 6)).astype(np.float32)
    rr, cc = np.arange(n)[:, None], np.arange(n)[None, :]
    same = (rr >> lc) == (cc >> lc)
    tri = [same & (cc <= rr), same & (cc < rr), rr == cc]
    for lb in range(lc):
        tri.append(((rr >> (lb + 1)) == (cc >> (lb + 1))) & (((rr >> lb) & 1) == 1) & (((cc >> lb) & 1) == 0))
    tl = np.arange(c)[None, :] <= np.arange(c)[:, None]
    hh = np.arange(GROUP_W)
    bd = (hh[:, None] >> 6) == (hh[None, :] >> 6)
    return (jnp.asarray(hm), jnp.asarray(np.stack(tri).astype(np.float32)),
            jnp.asarray(tl, BF16), jnp.asarray(bd, BF16))


def _scan_call(kernel, name, x, consts_a, hist, s0, consts_b, width_in, tm, c, front, grp):
    b, l, _ = x.shape
    assert b % grp == 0
    consts_b = tuple(consts_b) + _stack_consts(c)
    rows_spec = lambda rows, width: pl.BlockSpec((grp, rows, width), lambda bb, j: (bb, 0, 0))
    return pl.pallas_call(
        functools.partial(kernel, tm=tm, c=c, front=front, grp=grp),
        grid=(b // grp, l // tm),
        in_specs=[pl.BlockSpec((grp, tm, D_MODEL), lambda bb, j: (bb, j, 0))]
        + [_const_spec(a) for a in consts_a]
        + [rows_spec(SUBLANE, width_in), rows_spec(GROUP_W, GROUP_W)]
        + [_const_spec(a) for a in consts_b],
        out_specs=[pl.BlockSpec((grp, tm, GROUP_W), lambda bb, j: (bb, j, 0)),
                   rows_spec(GROUP_W, GROUP_W), rows_spec(SUBLANE, width_in)],
        out_shape=[jax.ShapeDtypeStruct((b, l, GROUP_W), F32),
                   jax.ShapeDtypeStruct((b, GROUP_W, GROUP_W), F32),
                   jax.ShapeDtypeStruct((b, SUBLANE, width_in), F32)],
        scratch_shapes=[pltpu.VMEM((grp, tm + SUBLANE, width_in), F32),
                        pltpu.VMEM((grp, GROUP_W, GROUP_W), F32)],
        compiler_params=_params(), name=name,
    )(x, *consts_a, hist, s0, *consts_b)


def _ffn(x, ya, yb, yc, yd, wo, g2, wup, cw, wdn, hist, tm, front):
    b, l, _ = x.shape
    return pl.pallas_call(
        functools.partial(_ffn_kernel, tm=tm, front=front),
        grid=(b, l // tm),
        in_specs=[_tile_spec(tm, D_MODEL), _cols_spec(GROUP_W, tm)] + [_tile_spec(tm, GROUP_W)] * 3
        + [_const_spec(a) for a in (wo, g2, wup, cw, wdn)] + [_batch_spec(SUBLANE, 2 * D_FF)],
        out_specs=[_tile_spec(tm, D_MODEL), _batch_spec(SUBLANE, 2 * D_FF)],
        out_shape=[jax.ShapeDtypeStruct((b, l, D_MODEL), F32),
                   jax.ShapeDtypeStruct((b, SUBLANE, 2 * D_FF), F32)],
        scratch_shapes=[pltpu.VMEM((SUBLANE, 2 * D_FF), F32),
                        pltpu.VMEM((tm + SUBLANE, 2 * FFN_CW), F32)],
        compiler_params=_params(), name="out_ffn",
    )(x, ya, yb, yc, yd, wo, g2, wup, cw, wdn, hist)


def _final_norm(x, g, tm, skip_tiles, out_rows):
    b = x.shape[0]
    return pl.pallas_call(
        _final_norm_kernel,
        grid=(b, out_rows // tm),
        in_specs=[pl.BlockSpec((1, tm, D_MODEL), lambda bb, j: (bb, j + skip_tiles, 0)), _const_spec(g)],
        out_specs=_tile_spec(tm, D_MODEL),
        out_shape=jax.ShapeDtypeStruct((b, out_rows, D_MODEL), F32),
        compiler_params=_params(), name="final_norm",
    )(x, g)


def _np_idx():
    z = IN_COLS
    zpad = lambda n: [z] * n
    rep = lambda base: [base + i for i in range(N_HEADS) for _ in range(HEAD_W)]
    grp = lambda base: [base + g * HEAD_W + i for g in (0, 0, 1, 1) for i in range(HEAD_W)]
    a = (list(range(0, 192)) + zpad(64) + list(range(192, 320))
         + list(range(320, 352)) + zpad(96)
         + list(range(336, 352)) + list(range(320, 336)) + zpad(96))
    b0 = A_COLS
    bcols = list(range(b0, b0 + B_COLS))
    c0 = b0 + B_COLS
    ccols = (list(range(c0, c0 + 256)) + list(range(c0 + 256, c0 + 512))
             + grp(c0 + 512) + grp(c0 + 640) + rep(c0 + 768))
    d0 = c0 + C_COLS
    dcols = list(range(d0, d0 + 1024)) + rep(d0 + 1024) + rep(d0 + 1028)
    xbc_exp = list(range(256)) + grp(256) + grp(384)
    xbc_back = (list(range(256)) + list(range(256, 320)) + list(range(384, 448))
                + list(range(512, 576)) + list(range(640, 704)))
    ffn_perm = []
    for f in range(D_FF // FFN_CW):
        ffn_perm += list(range(f * FFN_CW, (f + 1) * FFN_CW))
        ffn_perm += list(range(D_FF + f * FFN_CW, D_FF + (f + 1) * FFN_CW))
    ffn_back = np.argsort(np.array(ffn_perm))
    as_i = lambda v: np.asarray(v, np.int32)
    return dict(a=as_i(a), b=as_i(bcols), c=as_i(ccols), d=as_i(dcols), xbc_exp=as_i(xbc_exp),
                xbc_back=as_i(xbc_back), ffn_perm=as_i(ffn_perm), ffn_back=as_i(ffn_back))


_IDX = _np_idx()


def _rep_heads(v):
    return jnp.repeat(v, HEAD_W, axis=-1)


def _pad_rows(w, rows=SUBLANE):
    return jnp.pad(w, [(0, rows - w.shape[0])] + [(0, 0)] * (w.ndim - 1))


def _layer_consts(P, l):
    row = lambda v: v.reshape(1, -1).astype(F32)
    w_in = jnp.concatenate([P['w_in'][l], jnp.zeros((D_MODEL, 1), F32)], axis=1)
    c = {}
    c['g1'] = row(P['norm1_g'][l])
    c['wa'] = w_in[:, _IDX['a']].astype(BF16)
    c['wb'] = w_in[:, _IDX['b']].astype(BF16)
    c['wc'] = w_in[:, _IDX['c']].astype(BF16)
    c['wd'] = w_in[:, _IDX['d']].astype(BF16)
    c['gq'] = row(jnp.pad(P['a_gq'][l], (0, 64)))
    wuq = P['a_wuq'][l].reshape(A_QRANK, N_HEADS, A_NOPE + A_ROPE)
    rope = wuq[:, :, A_NOPE:]
    swap = jnp.concatenate([rope[..., 16:], rope[..., :16]], axis=-1)
    zeros = lambda n: jnp.zeros((A_QRANK, N_HEADS, n), F32)
    wq1 = jnp.concatenate([wuq, zeros(32)], axis=-1).reshape(A_QRANK, 512)
    wq2 = jnp.concatenate([zeros(64), swap, zeros(32)], axis=-1).reshape(A_QRANK, 512)
    c['wq1t'] = jnp.pad(wq1, ((0, 64), (0, 0))).T.astype(BF16)
    c['wq2t'] = jnp.pad(wq2, ((0, 64), (0, 0))).T.astype(BF16)
    c['gkv'] = row(P['a_gkv'][l])
    wuk = P['a_wuk'][l].reshape(A_KVRANK, N_HEADS, A_NOPE)
    c['wk'] = jnp.concatenate([wuk, jnp.zeros((A_KVRANK, N_HEADS, 64), F32)], axis=-1
                              ).reshape(A_KVRANK, 512).astype(BF16)
    ek = np.zeros((A_ROPE, N_HEADS, LANE), np.float32)
    for hh in range(N_HEADS):
        ek[np.arange(A_ROPE), hh, A_NOPE + np.arange(A_ROPE)] = 1.0
    c['ek'] = jnp.asarray(ek.reshape(A_ROPE, 512), BF16)
    wuv = P['a_wuv'][l].reshape(A_KVRANK, N_HEADS, HEAD_W)
    c['wvt'] = jnp.concatenate([wuv, jnp.zeros((A_KVRANK, N_HEADS, HEAD_W), F32)], axis=-1
                               ).reshape(A_KVRANK, 512).T.astype(BF16)
    ones = np.zeros((N_HEADS, LANE, 1), np.float32)
    ones[:, HEAD_W, 0] = 1.0
    c['ones_col'] = jnp.asarray(ones.reshape(512, 1))
    c['gout_col'] = P['a_gout'][l].reshape(GROUP_W, 1).astype(F32)
    c['mu'] = row(P['b_mu'][l])
    c['w0'] = row(P['b_w0'][l])
    z64 = jnp.zeros((64, GROUP_W), F32)
    c['wl'] = jnp.concatenate([jnp.concatenate([P['b_w2'][l], z64], axis=1),
                               jnp.concatenate([z64, P['b_a2'][l]], axis=1)], axis=0).astype(BF16)
    c['a0'] = row(P['b_a0'][l])
    c['g2b'] = P['b_g2'][l].astype(BF16)
    c['kk'] = row(P['b_kk'][l])
    c['ka'] = row(P['b_ka'][l])
    c['rk'] = row(P['b_rk'][l])
    c['gnw'] = row(P['b_gnw'][l])
    c['gnb'] = row(P['b_gnb'][l])
    c['c_cw'] = _pad_rows(P['c_convw'][l][:, _IDX['xbc_exp']])
    c['c_cb'] = row(P['c_convb'][l][_IDX['xbc_exp']])
    c['c_dtb'] = row(_rep_heads(P['c_dtb'][l]))
    c['c_alog'] = row(_rep_heads(P['c_alog'][l]))
    c['c_d'] = row(_rep_heads(P['c_d'][l]))
    c['c_gn'] = row(P['c_gnorm'][l])
    c['d_cw'] = _pad_rows(P['d_convw'][l])
    c['d_alog'] = row(_rep_heads(P['d_alog'][l]))
    c['d_dtb'] = row(_rep_heads(P['d_dtb'][l]))
    c['d_gn'] = row(jnp.tile(P['d_gnorm'][l], N_HEADS))
    c['wo'] = P['w_out'][l].astype(BF16)
    c['g2'] = row(P['norm2_g'][l])
    c['wup'] = P['f_wup'][l][:, _IDX['ffn_perm']].astype(BF16)
    c['f_cw'] = _pad_rows(P['f_convw'][l][:, _IDX['ffn_perm']])
    c['wdn'] = P['f_wdown'][l].astype(BF16)
    return c


def _embed_bd(s):
    b = s.shape[0]
    eye = jnp.eye(N_HEADS, dtype=s.dtype)
    return jnp.einsum('bhij,hg->bhigj', s, eye).reshape(b, GROUP_W, GROUP_W)


def _extract_bd(s):
    b = s.shape[0]
    s5 = s.reshape(b, N_HEADS, HEAD_W, N_HEADS, HEAD_W)
    return jnp.stack([s5[:, hh, :, hh, :] for hh in range(N_HEADS)], axis=1)


def _hist8(hist):
    return jnp.pad(hist, ((0, 0), (SUBLANE - hist.shape[1], 0), (0, 0)))


def _rope_table(pos):
    half = A_ROPE // 2
    inv = jnp.power(ROPE_BASE, -jnp.arange(half, dtype=F32) / half)
    ang = pos.astype(F32)[:, None] * inv
    cos, sin = jnp.cos(ang), jnp.sin(ang)
    cos2 = jnp.concatenate([cos, cos], axis=-1)
    sin2 = jnp.concatenate([-sin, sin], axis=-1)
    n = pos.shape[0]
    one = jnp.ones((n, A_NOPE), F32)
    z = lambda w: jnp.zeros((n, w), F32)
    tabq = jnp.concatenate([one, cos2, z(32), z(64), sin2, z(32)], axis=-1).T
    tabk = jnp.concatenate([cos2, z(96), sin2, z(96)], axis=-1)
    return tabq, tabk


def _trunk(x, pos, front, st, P, *, tm, tm_ffn, c, tq, causal, n_keys_pad):
    b, l, _ = x.shape
    tabq, tabk = _rope_table(pos)
    new = {name: [] for name in ('ckv', 'krope', 'rwkv_S', 'rwkv_shift', 'ssd_S', 'ssd_conv',
                                 'gdn_S', 'gdn_conv', 'ffn_conv')}
    zeros_bd = jnp.zeros((b, GROUP_W, GROUP_W), F32)
    for li in range(DEPTH):
        c_ = P[li]
        qt, ckv, krope = _mla_prep(x, c_['g1'], c_['wa'], c_['gq'], c_['wq1t'], c_['wq2t'], c_['gkv'],
                                   tabq, tabk, tm_ffn, front)
        if st is None:
            c_all, kr_all, klo, khi = ckv, krope, front, l
            s_b = s_c = s_d = zeros_bd
            shift8 = jnp.zeros((b, SUBLANE, B_COLS), F32)
            chist = jnp.zeros((b, SUBLANE, 768), F32)
            dhist = jnp.zeros((b, SUBLANE, 768), F32)
            fhist = jnp.zeros((b, SUBLANE, 2 * D_FF), F32)
        else:
            past = st['ckv'].shape[2]
            padk = n_keys_pad - past - l
            c_all = jnp.concatenate([st['ckv'][li], ckv, jnp.zeros((b, padk, A_KVRANK), F32)], axis=1)
            kr_all = jnp.concatenate([st['krope'][li], krope, jnp.zeros((b, padk, A_ROPE), F32)], axis=1)
            klo, khi = 0, past + l
            s_b = _embed_bd(st['rwkv_S'][li])
            s_c = _embed_bd(jnp.swapaxes(st['ssd_S'][li], -1, -2))
            s_d = _embed_bd(st['gdn_S'][li])
            shift8 = _hist8(st['rwkv_shift'][li][:, None, :])
            chist = _hist8(st['ssd_conv'][li][:, :, _IDX['xbc_exp']])
            dhist = _hist8(st['gdn_conv'][li])
            fhist = _hist8(st['ffn_conv'][li][:, :, _IDX['ffn_perm']])
        nk = c_all.shape[1]
        tkv = tq if causal else nk
        kf, vt = _kv_up(c_all, kr_all, c_['wk'], c_['ek'], c_['wvt'], c_['ones_col'], tkv)
        ya = _flash(qt, kf, vt, c_['gout_col'], tq, tkv, causal, klo, khi)
        yb, sb_new, shift_new = _scan_call(
            _rwkv_kernel, "rwkv7", x, (c_['g1'], c_['wb']), shift8, s_b,
            (c_['mu'], c_['w0'], c_['wl'], c_['a0'], c_['g2b'], c_['kk'], c_['ka'], c_['rk'],
             c_['gnw'], c_['gnb']), B_COLS, tm, c, front, SCAN_ROWS)
        yc, sc_new, chist_new = _scan_call(
            _ssd_kernel, "ssd", x, (c_['g1'], c_['wc']), chist, s_c,
            (c_['c_cw'], c_['c_cb'], c_['c_dtb'], c_['c_alog'], c_['c_d'], c_['c_gn']),
            768, tm, c, front, SCAN_ROWS)
        yd, sd_new, dhist_new = _scan_call(
            _gdn_kernel, "gdn", x, (c_['g1'], c_['wd']), dhist, s_d,
            (c_['d_cw'], c_['d_alog'], c_['d_dtb'], c_['d_gn']), 768, tm, c, front, SCAN_ROWS)
        x, fhist_new = _ffn(x, ya, yb, yc, yd, c_['wo'], c_['g2'], c_['wup'], c_['f_cw'], c_['wdn'],
                            fhist, tm_ffn, front)
        new['ckv'].append(ckv[:, front:])
        new['krope'].append(krope[:, front:])
        new['rwkv_S'].append(_extract_bd(sb_new))
        new['rwkv_shift'].append(shift_new[:, SUBLANE - 1])
        new['ssd_S'].append(jnp.swapaxes(_extract_bd(sc_new), -1, -2))
        new['ssd_conv'].append(chist_new[:, SUBLANE - (C_CONV - 1):][:, :, _IDX['xbc_back']])
        new['gdn_S'].append(_extract_bd(sd_new))
        new['gdn_conv'].append(dhist_new[:, SUBLANE - (D_CONV - 1):])
        new['ffn_conv'].append(fhist_new[:, SUBLANE - (FFN_CONV - 1):][:, :, _IDX['ffn_back']])
    return x, {name: jnp.stack(vals) for name, vals in new.items()}


def kernel(x_prompt, x_sample, cache_mla_ckv, cache_mla_krope, state_rwkv, state_rwkv_shift, state_ssd, state_ssd_conv, state_gdn, state_gdn_conv, state_ffn_conv, meta_tokens, norm1_g, w_in, a_gq, a_wuq, a_gkv, a_wuk, a_wuv, a_gout, b_mu, b_w0, b_w2, b_a0, b_a2, b_g2, b_kk, b_ka, b_rk, b_gnw, b_gnb, c_convw, c_convb, c_dtb, c_alog, c_d, c_gnorm, d_convw, d_alog, d_dtb, d_gnorm, w_out, norm2_g, f_wup, f_convw, f_wdown, final_g):
    P = dict(norm1_g=norm1_g, w_in=w_in, a_gq=a_gq, a_wuq=a_wuq, a_gkv=a_gkv, a_wuk=a_wuk,
             a_wuv=a_wuv, a_gout=a_gout, b_mu=b_mu, b_w0=b_w0, b_w2=b_w2, b_a0=b_a0, b_a2=b_a2,
             b_g2=b_g2, b_kk=b_kk, b_ka=b_ka, b_rk=b_rk, b_gnw=b_gnw, b_gnb=b_gnb,
             c_convw=c_convw, c_convb=c_convb, c_dtb=c_dtb, c_alog=c_alog, c_d=c_d, c_gnorm=c_gnorm,
             d_convw=d_convw, d_alog=d_alog, d_dtb=d_dtb, d_gnorm=d_gnorm,
             w_out=w_out, norm2_g=norm2_g, f_wup=f_wup, f_convw=f_convw, f_wdown=f_wdown)
    P = [_layer_consts(P, li) for li in range(DEPTH)]
    fin = final_g.reshape(1, -1).astype(F32)
    b_p, seq, _ = x_prompt.shape
    lx = N_META + seq
    assert seq % ATT_BLOCK == 0 and N_META <= ATT_BLOCK
    front = ATT_BLOCK - N_META
    l_pad = front + lx
    meta = jnp.broadcast_to(meta_tokens[None].astype(F32), (b_p, N_META, D_MODEL))
    x_ext = jnp.concatenate([jnp.zeros((b_p, front, D_MODEL), F32), meta, x_prompt], axis=1)
    pos_p = jnp.arange(l_pad, dtype=jnp.int32) - (front + N_META)
    tm_ffn = max(t for t in (ROW_TILE, 2 * ROW_TILE, 3 * ROW_TILE) if l_pad % t == 0)
    y_p, ns_p = _trunk(x_ext, pos_p, front, None, P, tm=ROW_TILE, tm_ffn=tm_ffn, c=CHUNK, tq=ATT_BLOCK,
                       causal=True, n_keys_pad=l_pad)
    y_prompt = _final_norm(y_p, fin, ROW_TILE, (front + N_META) // ROW_TILE, seq)
    b_s, t_s, _ = x_sample.shape
    past = cache_mla_ckv.shape[2]
    assert t_s <= CHUNK and t_s % 16 == 0 and (t_s & (t_s - 1)) == 0
    st_s = dict(ckv=cache_mla_ckv, krope=cache_mla_krope, rwkv_S=state_rwkv, rwkv_shift=state_rwkv_shift,
                ssd_S=state_ssd, ssd_conv=state_ssd_conv, gdn_S=state_gdn, gdn_conv=state_gdn_conv,
                ffn_conv=state_ffn_conv)
    pos_s = past + jnp.arange(t_s, dtype=jnp.int32)
    n_keys_pad = -(-(past + t_s) // LANE) * LANE
    y_s, ns_s = _trunk(x_sample, pos_s, 0, st_s, P, tm=t_s, tm_ffn=t_s, c=t_s, tq=t_s, causal=False,
                       n_keys_pad=n_keys_pad)
    y_sample = _final_norm(y_s, fin, t_s, 0, t_s)
    keys = ('ckv', 'krope', 'rwkv_S', 'rwkv_shift', 'ssd_S', 'ssd_conv', 'gdn_S', 'gdn_conv', 'ffn_conv')
    return (y_prompt, y_sample) + tuple(ns_p[k] for k in keys) + tuple(ns_s[k] for k in keys)
```

```python
import functools
import math

import numpy as np
import jax
import jax.numpy as jnp
from jax import lax
from jax.experimental import pallas as pl
from jax.experimental.pallas import tpu as pltpu

F32 = jnp.float32
BF16 = jnp.bfloat16

D_MODEL = 1024
DEPTH = 4
CHUNK = 64
N_META = 16
EPS = 1e-6
L2_EPS = 1e-6
GROUP_W = 256
N_HEADS = 4
HEAD_W = 64
A_NOPE = 64
A_ROPE = 32
A_QRANK = 192
A_KVRANK = 128
A_SCALE = (A_NOPE + A_ROPE) ** -0.5
ROPE_BASE = 10000.0
B_GN_EPS = 64e-5
B_COLS = 1024
C_CONV = 4
D_CONV = 4
D_FF = 2816
FFN_CONV = 3
A_COLS = 352
C_COLS = 772
D_COLS = 1032
IN_COLS = A_COLS + B_COLS + C_COLS + D_COLS

LANE = 128
SUBLANE = 8
VMEM_LIMIT = 56 * 1024 * 1024
NEG = -1e30
LOG2E = math.log2(math.e)
FFN_CW = 256
FFN_DOWN_GROUP = 4
ATT_BLOCK = 256
ROW_TILE = 256
SCAN_ROWS = 2


def _mm(a, b):
    return jnp.dot(a.astype(BF16), b.astype(BF16), preferred_element_type=F32)


def _mm_nt(a, b):
    return lax.dot_general(a.astype(BF16), b.astype(BF16), (((1,), (1,)), ((), ())),
                           preferred_element_type=F32)


def _mm_tn(a, b):
    return lax.dot_general(a.astype(BF16), b.astype(BF16), (((0,), (0,)), ((), ())),
                           preferred_element_type=F32)


def _split2(x):
    hi = x.astype(BF16)
    lo = (x - hi.astype(F32)).astype(BF16)
    return hi, lo


def _mm_x2(x, w):
    hi, lo = _split2(x)
    return (jnp.dot(hi, w, preferred_element_type=F32)
            + jnp.dot(lo, w, preferred_element_type=F32))


def _rms(x, g, n):
    ms = jnp.sum(x * x, axis=-1, keepdims=True) * (1.0 / n)
    return x * lax.rsqrt(ms + EPS) * g


def _softplus(x):
    return jnp.maximum(x, 0.0) + jnp.log1p(jnp.exp(-jnp.abs(x)))


def _sigmoid(x):
    return 1.0 / (1.0 + jnp.exp(-x))


def _silu(x):
    return x * _sigmoid(x)


def _iota(shape, axis):
    return lax.broadcasted_iota(jnp.int32, shape, axis)


def _norm_in(x_ref, g_ref, j, tm, front):
    h = _rms(x_ref[0], g_ref[...], D_MODEL)
    if front > 0:
        rows = j * tm + _iota((tm, 1), 0)
        h = jnp.where(rows >= front, h, 0.0)
    return h.astype(BF16)


def _norm_in_rows(x_ref, g_ref, j, tm, front):
    x = x_ref[...]
    h = _rms(x, g_ref[...], D_MODEL)
    if front > 0:
        pos = j * tm + _iota((1, tm, 1), 1)
        h = jnp.where(pos >= front, h, 0.0)
    return h.reshape(x.shape[0] * tm, D_MODEL).astype(BF16)


def _hsum(x, bd):
    return _mm_x2(x, bd)


TRI_INCL, TRI_STRICT, TRI_EYE, TRI_JOIN = 0, 1, 2, 3


class _Stk:
    def __init__(self, c, hm_ref, tri_ref, tl_ref):
        self.c = c
        self.n = N_HEADS * c
        self.steps = int(math.log2(c))
        assert 1 << self.steps == c
        self.hm_ref, self.tri_ref, self.tl_ref = hm_ref, tri_ref, tl_ref

    def mask(self, which):
        return self.tri_ref[which]

    def tile(self, x):
        return jnp.concatenate([x] * N_HEADS, axis=0)

    def stack(self, x):
        return self.tile(x) * self.hm_ref[...]

    def unstack(self, xs):
        c = self.c
        return xs[0:c] + xs[c:2 * c] + xs[2 * c:3 * c] + xs[3 * c:4 * c]

    def cumsum(self, x):
        hi, lo = _split2(x)
        tl = self.tl_ref[...]
        return jnp.dot(tl, hi, preferred_element_type=F32) + jnp.dot(tl, lo, preferred_element_type=F32)

    def decay(self, g_cum):
        gs = self.stack(g_cum)
        gcol = jnp.min(gs, axis=-1, keepdims=True)
        hi, lo = _split2(gs)
        o2 = jnp.full((self.n, GROUP_W), 1.0 / HEAD_W, BF16)
        grow = _mm_nt(o2, hi) + _mm_nt(o2, lo)
        return jnp.exp(jnp.minimum(gcol - grow, 0.0)) * self.mask(TRI_INCL)

    def tri_inv_many(self, lowers):
        eye = self.mask(TRI_EYE)
        minvs = [eye - lw * self.mask(TRI_JOIN) for lw in lowers]
        for lb in range(1, self.steps):
            ts = [_mm(mi, lw * self.mask(TRI_JOIN + lb)) for mi, lw in zip(minvs, lowers)]
            minvs = [mi - _mm(t, mi) for t, mi in zip(ts, minvs)]
        return minvs


def _mla_prep_kernel(x_ref, g1_ref, wa_ref, gq_ref, wq1t_ref, wq2t_ref, gkv_ref, tabq_ref, tabk_ref,
                     qt_out, ckv_out, kr_out, *, tm, front):
    j = pl.program_id(1)
    h = _norm_in(x_ref, g1_ref, j, tm, front)
    pa = jnp.dot(h, wa_ref[...], preferred_element_type=F32)
    qn = _rms(pa[:, 0:256], gq_ref[...], A_QRANK)
    q1t = _mm_nt(wq1t_ref[...], qn)
    q2t = _mm_nt(wq2t_ref[...], qn)
    tabq = tabq_ref[...]
    cos4 = jnp.concatenate([tabq[0:128]] * N_HEADS, axis=0)
    sin4 = jnp.concatenate([tabq[128:256]] * N_HEADS, axis=0)
    qt_out[0] = ((q1t * cos4 + q2t * sin4) * (A_SCALE * LOG2E)).astype(BF16)
    ckv_out[0] = _rms(pa[:, 256:384], gkv_ref[...], A_KVRANK)
    tabk = tabk_ref[...]
    kr = pa[:, 384:512] * tabk[:, 0:128] + pa[:, 512:640] * tabk[:, 128:256]
    kr_out[0] = kr[:, 0:A_ROPE]


def _kv_up_kernel(c_ref, kr_ref, wk_ref, ek_ref, wvt_ref, ones_ref, k_out, vt_out):
    c = c_ref[0].astype(BF16)
    kr = kr_ref[0].astype(BF16)
    k = (jnp.dot(c, wk_ref[...], preferred_element_type=F32)
         + jnp.dot(kr, ek_ref[...], preferred_element_type=F32))
    k_out[0] = k.astype(BF16)
    vt_out[0, 0] = (_mm_nt(wvt_ref[...], c) + ones_ref[...]).astype(BF16)


def _flash_kernel(qt_ref, k_ref, vt_ref, gout_ref, o_ref, m_sc, acc_sc, sa_sc, sb_sc,
                  *, tq, tk, nkv, causal, klo, khi):
    i = pl.program_id(1)
    m_sc[...] = jnp.full((N_HEADS * SUBLANE, tq), NEG, F32)
    acc_sc[...] = jnp.zeros((N_HEADS * LANE, tq), F32)
    heads = [slice(LANE * h, LANE * (h + 1)) for h in range(N_HEADS)]
    s_bufs = (sa_sc, sb_sc)

    def produce(jb, slot):
        start = pl.multiple_of(jb * tk, tk)
        for h, rows in enumerate(heads):
            s_bufs[slot][h * tk:(h + 1) * tk, :] = jnp.dot(
                k_ref[0, pl.ds(start, tk), rows], qt_ref[0, rows, :], preferred_element_type=F32)

    def consume(jb, slot, masked, nxt):
        if nxt is not None:
            produce(*nxt)
        if masked:
            kpos = jb * tk + _iota((tk, 1), 0)
            qpos = i * tq + _iota((1, tq), 1)
            vis = (kpos >= klo) & (kpos < khi)
            if causal:
                vis = vis & ((kpos >> 6) <= (qpos >> 6))
        m_all = m_sc[...]
        acc_all = acc_sc[...]
        m_out, alphas, ps = [], [], []
        for h in range(N_HEADS):
            s = s_bufs[slot][h * tk:(h + 1) * tk, :]
            if masked:
                s = jnp.where(vis, s, NEG)
            m_prev = m_all[SUBLANE * h:SUBLANE * (h + 1)]
            m_new = jnp.maximum(m_prev, jnp.max(s, axis=0, keepdims=True))
            alphas.append(jnp.exp2(m_prev[0:1] - m_new[0:1]))
            ps.append(jnp.exp2(s - m_new[0:1]).astype(BF16))
            m_out.append(m_new)
        acc_out = [alphas[h] * acc_all[rows]
                   + jnp.dot(vt_ref[0, jb, rows, :], ps[h], preferred_element_type=F32)
                   for h, rows in enumerate(heads)]
        m_sc[...] = jnp.concatenate(m_out, axis=0)
        acc_sc[...] = jnp.concatenate(acc_out, axis=0)

    produce(0, 0)
    if causal:
        @pl.when(i == 0)
        def _():
            consume(0, 0, True, None)

        @pl.when(i > 0)
        def _():
            consume(0, 0, True, (1, 1))
            quads = (i - 1) >> 2

            def pair(jb):
                consume(jb, 1, False, (jb + 1, 0))
                consume(jb + 1, 0, False, (jb + 2, 1))

            def body(t, carry):
                pair(1 + 4 * t)
                pair(3 + 4 * t)
                return carry

            lax.fori_loop(0, quads, body, 0)

            @pl.when(((i - 1) & 2) != 0)
            def _():
                pair(1 + 4 * quads)

            @pl.when((i & 1) == 0)
            def _():
                consume(i - 1, 1, False, (i, 0))
                consume(i, 0, True, None)

            @pl.when((i & 1) == 1)
            def _():
                consume(i, 1, True, None)
    else:
        for jb in range(nkv):
            consume(jb, jb & 1, True, (jb + 1, (jb + 1) & 1) if jb + 1 < nkv else None)
    outs = []
    for h in range(N_HEADS):
        a = acc_sc[LANE * h:LANE * (h + 1), :]
        outs.append(a[0:HEAD_W] / a[HEAD_W:HEAD_W + 1])
    yat = jnp.concatenate(outs, axis=0)
    ms = jnp.sum(yat * yat, axis=0, keepdims=True) * (1.0 / GROUP_W)
    o_ref[0] = yat * lax.rsqrt(ms + EPS) * gout_ref[...]


def _chunk_ids(grp, tm, c):
    ids = [(g, ci) for ci in range(tm // c) for g in range(grp)]
    return ids, {(g, ci): slice(g * tm + ci * c, g * tm + (ci + 1) * c) for g, ci in ids}


def _carry_rows(work, g, new_rows, tm):
    work[g, SUBLANE:SUBLANE + tm, :] = new_rows
    return work[g, tm:tm + SUBLANE, :]


def _rwkv_kernel(x_ref, g1_ref, wb_ref, shift_ref, s0_ref, mu_ref, w0_ref, wl_ref, a0_ref,
                 g2_ref, kk_ref, ka_ref, rk_ref, gnw_ref, gnb_ref, hm_ref, tri_ref, tl_ref, bd_ref,
                 y_out, s_out, shift_out, work, st, *, tm, c, front, grp):
    j = pl.program_id(1)

    @pl.when(j == 0)
    def _():
        work[:, 0:SUBLANE, :] = shift_ref[...]
        st[...] = s0_ref[...]

    h = _norm_in_rows(x_ref, g1_ref, j, tm, front)
    cols = jnp.dot(h, wb_ref[...], preferred_element_type=F32)
    shifted = []
    for g in range(grp):
        tail = _carry_rows(work, g, cols[g * tm:(g + 1) * tm], tm)
        shifted.append(work[g, pl.ds(SUBLANE - 1, tm), :])
        work[g, 0:SUBLANE, :] = tail
        shift_out[g] = tail
    shifted = jnp.concatenate(shifted, axis=0)
    xm = cols + (shifted - cols) * mu_ref[...]
    r = xm[:, 0:256]
    k = xm[:, 256:512]
    v = xm[:, 512:768]
    lora = xm[:, 768:896]
    dg = xm[:, 896:1024]
    lora = jnp.where(_iota((grp * tm, LANE), 1) < 64, jnp.tanh(lora), lora)
    ll = _mm(lora, wl_ref[...])
    w_log = -_softplus(-(w0_ref[...] + ll[:, 0:256])) - 0.5
    logd = -jnp.exp(w_log)
    a = _sigmoid(a0_ref[...] + ll[:, 256:512])
    g_gate = _mm(_sigmoid(dg), g2_ref[...])
    bd = bd_ref[...]
    kkr = k * kk_ref[...]
    kk = kkr * lax.rsqrt(_hsum(kkr * kkr, bd) + L2_EPS)
    k2 = k * (1.0 + (a - 1.0) * ka_ref[...])

    sk = _Stk(c, hm_ref, tri_ref, tl_ref)
    ids, rows = _chunk_ids(grp, tm, c)
    strict, incl = sk.mask(TRI_STRICT), sk.mask(TRI_INCL)
    gcs = {i: sk.cumsum(logd[rows[i]]) for i in ids}
    pre = {}
    for i in ids:
        gc, ld = gcs[i], logd[rows[i]]
        eg, eng = jnp.exp(gc), jnp.exp(-gc)
        bt = kk[rows[i]] * a[rows[i]] * eng
        kt = k2[rows[i]] * eng
        dc = eg[c - 1:c, :]
        pre[i] = dict(at_s=sk.stack(-kk[rows[i]] * jnp.exp(gc - ld)), rt_s=sk.stack(r[rows[i]] * eg),
                      v_s=sk.stack(v[rows[i]]), bt_t=sk.tile(bt), kt_t=sk.tile(kt), dc=dc,
                      bd_s=sk.stack(bt * dc), kd_s=sk.stack(kt * dc))
    lab = {i: _mm_nt(pre[i]['at_s'], pre[i]['bt_t']) * strict for i in ids}
    aak = {i: _mm_nt(pre[i]['at_s'], pre[i]['kt_t']) * strict for i in ids}
    arb = {i: _mm_nt(pre[i]['rt_s'], pre[i]['bt_t']) * incl for i in ids}
    ark = {i: _mm_nt(pre[i]['rt_s'], pre[i]['kt_t']) * incl for i in ids}
    minv = dict(zip(ids, sk.tri_inv_many([-lab[i] for i in ids])))
    a2 = {i: _mm(aak[i], pre[i]['v_s']) for i in ids}
    u0 = {i: _mm(minv[i], a2[i]) for i in ids}
    m1 = {i: _mm(minv[i], pre[i]['at_s']) for i in ids}
    ork = {i: _mm(ark[i], pre[i]['v_s']) for i in ids}
    skv = {i: _mm_tn(pre[i]['v_s'], pre[i]['kd_s']) for i in ids}

    s = [st[g] for g in range(grp)]
    outs = {}
    for ci in range(tm // c):
        sb = [s[g].astype(BF16) for g in range(grp)]
        us = [u0[(g, ci)] + _mm_nt(m1[(g, ci)], sb[g]) for g in range(grp)]
        oq = [_mm_nt(pre[(g, ci)]['rt_s'], sb[g]) for g in range(grp)]
        for g in range(grp):
            i = (g, ci)
            s[g] = s[g] * pre[i]['dc'] + _mm_tn(us[g], pre[i]['bd_s']) + skv[i]
            outs[i] = sk.unstack(oq[g] + _mm(arb[i], us[g]) + ork[i])
    for g in range(grp):
        st[g] = s[g]
        s_out[g] = s[g]
    o = jnp.concatenate([outs[(g, ci)] for g in range(grp) for ci in range(tm // c)], axis=0)
    mean = _hsum(o, bd) * (1.0 / HEAD_W)
    d = o - mean
    var = _hsum(d * d, bd) * (1.0 / HEAD_W)
    o = d * lax.rsqrt(var + B_GN_EPS) * gnw_ref[...] + gnb_ref[...]
    bonus = _hsum(r * k2 * rk_ref[...], bd) * v
    y_out[...] = ((o + bonus) * g_gate).reshape(grp, tm, GROUP_W)


def _conv4_rows(work, new, wv, hist_out, grp, tm):
    ys = []
    for g in range(grp):
        tail = _carry_rows(work, g, new[g * tm:(g + 1) * tm], tm)
        y = work[g, pl.ds(SUBLANE - 3, tm), :] * wv[0:1, :]
        y = y + work[g, pl.ds(SUBLANE - 2, tm), :] * wv[1:2, :]
        y = y + work[g, pl.ds(SUBLANE - 1, tm), :] * wv[2:3, :]
        ys.append(y + work[g, pl.ds(SUBLANE, tm), :] * wv[3:4, :])
        work[g, 0:SUBLANE, :] = tail
        hist_out[g] = tail
    return jnp.concatenate(ys, axis=0)


def _ssd_kernel(x_ref, g1_ref, wc_ref, hist_ref, s0_ref, cw_ref, cb_ref, dtb_ref, alog_ref,
                dskip_ref, gn_ref, hm_ref, tri_ref, tl_ref, bd_ref,
                y_out, s_out, hist_out, work, st, *, tm, c, front, grp):
    j = pl.program_id(1)

    @pl.when(j == 0)
    def _():
        work[:, 0:SUBLANE, :] = hist_ref[...]
        st[...] = s0_ref[...]

    h = _norm_in_rows(x_ref, g1_ref, j, tm, front)
    pc = jnp.dot(h, wc_ref[...], preferred_element_type=F32)
    z = pc[:, 0:256]
    xbc = _silu(_conv4_rows(work, pc[:, 256:1024], cw_ref[...], hist_out, grp, tm) + cb_ref[...])
    xs = xbc[:, 0:256]
    bm = xbc[:, 256:512]
    cm = xbc[:, 512:768]
    dt = _softplus(pc[:, 1024:1280] + dtb_ref[...])
    if front > 0:
        pos = j * tm + _iota((1, tm, 1), 1)
        dt = jnp.where(pos >= front, dt.reshape(grp, tm, GROUP_W), 0.0).reshape(grp * tm, GROUP_W)
    a = dt * (-jnp.exp(alog_ref[...]))
    xdt = xs * dt

    sk = _Stk(c, hm_ref, tri_ref, tl_ref)
    ids, rows = _chunk_ids(grp, tm, c)
    acs = {i: sk.cumsum(a[rows[i]]) for i in ids}
    dms = {i: sk.decay(acs[i]) for i in ids}
    xdt_s = {i: sk.stack(xdt[rows[i]]) for i in ids}
    amat = {i: _mm_nt(sk.stack(cm[rows[i]]), sk.tile(bm[rows[i]])) * dms[i] for i in ids}
    ydiag = {i: _mm(amat[i], xdt_s[i]) for i in ids}
    sx = {i: _mm_tn(sk.stack(bm[rows[i]] * jnp.exp(acs[i][c - 1:c, :] - acs[i])), xdt_s[i]) for i in ids}
    ce_s = {i: sk.stack(cm[rows[i]] * jnp.exp(acs[i])) for i in ids}

    s = [st[g] for g in range(grp)]
    outs = {}
    for ci in range(tm // c):
        for g in range(grp):
            i = (g, ci)
            outs[i] = sk.unstack(ydiag[i] + _mm(ce_s[i], s[g]))
            s[g] = s[g] * jnp.exp(acs[i][c - 1:c, :]) + sx[i]
    for g in range(grp):
        st[g] = s[g]
        s_out[g] = s[g]
    y = jnp.concatenate([outs[(g, ci)] for g in range(grp) for ci in range(tm // c)], axis=0)
    y = y + dskip_ref[...] * xs
    y_out[...] = _rms(y * _silu(z), gn_ref[...], GROUP_W).reshape(grp, tm, GROUP_W)


def _gdn_kernel(x_ref, g1_ref, wd_ref, hist_ref, s0_ref, cw_ref, alog_ref, dtb_ref, gn_ref,
                hm_ref, tri_ref, tl_ref, bd_ref,
                y_out, s_out, hist_out, work, st, *, tm, c, front, grp):
    j = pl.program_id(1)

    @pl.when(j == 0)
    def _():
        work[:, 0:SUBLANE, :] = hist_ref[...]
        st[...] = s0_ref[...]

    h = _norm_in_rows(x_ref, g1_ref, j, tm, front)
    pd = jnp.dot(h, wd_ref[...], preferred_element_type=F32)
    qkv = _silu(_conv4_rows(work, pd[:, 0:768], cw_ref[...], hist_out, grp, tm))
    z = pd[:, 768:1024]
    beta = _sigmoid(pd[:, 1024:1280])
    g_log = -jnp.exp(alog_ref[...]) * _softplus(pd[:, 1280:1536] + dtb_ref[...])
    bd = bd_ref[...]
    q = qkv[:, 0:256]
    k = qkv[:, 256:512]
    v = qkv[:, 512:768]
    q = q * lax.rsqrt(_hsum(q * q, bd) + L2_EPS) * (HEAD_W ** -0.5)
    k = k * lax.rsqrt(_hsum(k * k, bd) + L2_EPS)

    sk = _Stk(c, hm_ref, tri_ref, tl_ref)
    ids, rows = _chunk_ids(grp, tm, c)
    strict = sk.mask(TRI_STRICT)
    gcs = {i: sk.cumsum(g_log[rows[i]]) for i in ids}
    dms = {i: sk.decay(gcs[i]) for i in ids}
    kb = {i: k[rows[i]] * beta[rows[i]] for i in ids}
    k_t = {i: sk.tile(k[rows[i]]) for i in ids}
    lower = {i: _mm_nt(sk.stack(kb[i]), k_t[i]) * dms[i] * strict for i in ids}
    aqk = {i: _mm_nt(sk.stack(q[rows[i]]), k_t[i]) * dms[i] for i in ids}
    tinv = dict(zip(ids, sk.tri_inv_many([lower[i] for i in ids])))
    u = {i: _mm(tinv[i], sk.stack(v[rows[i]] * beta[rows[i]])) for i in ids}
    w = {i: _mm(tinv[i], sk.stack(kb[i] * jnp.exp(gcs[i]))) for i in ids}
    qe_s = {i: sk.stack(q[rows[i]] * jnp.exp(gcs[i])) for i in ids}
    kd_s = {i: sk.stack(k[rows[i]] * jnp.exp(gcs[i][c - 1:c, :] - gcs[i])) for i in ids}

    s = [st[g] for g in range(grp)]
    outs = {}
    for ci in range(tm // c):
        sb = [s[g].astype(BF16) for g in range(grp)]
        vn = [u[(g, ci)] - _mm(w[(g, ci)], sb[g]) for g in range(grp)]
        oq = [_mm(qe_s[(g, ci)], sb[g]) for g in range(grp)]
        for g in range(grp):
            i = (g, ci)
            s[g] = s[g] * jnp.exp(gcs[i][c - 1:c, :]) + _mm_tn(kd_s[i], vn[g])
            outs[i] = sk.unstack(oq[g] + _mm(aqk[i], vn[g]))
    for g in range(grp):
        st[g] = s[g]
        s_out[g] = s[g]
    o = jnp.concatenate([outs[(g, ci)] for g in range(grp) for ci in range(tm // c)], axis=0)
    ms = _hsum(o * o, bd) * (1.0 / HEAD_W)
    y_out[...] = (o * lax.rsqrt(ms + EPS) * gn_ref[...] * _silu(z)).reshape(grp, tm, GROUP_W)


def _ffn_kernel(x_ref, ya_ref, yb_ref, yc_ref, yd_ref, wo_ref, g2_ref, wup_ref, cw_ref, wdn_ref,
                hist_ref, xo_ref, hist_out, carry, work_a, work_b, *, tm, front):
    j = pl.program_id(1)

    @pl.when(j == 0)
    def _():
        carry[...] = hist_ref[0]

    x = x_ref[0] + _mm_tn(ya_ref[0], wo_ref[0:GROUP_W, :])
    for idx, y_ref in ((1, yb_ref), (2, yc_ref), (3, yd_ref)):
        x = x + jnp.dot(y_ref[0].astype(BF16), wo_ref[GROUP_W * idx:GROUP_W * (idx + 1), :],
                        preferred_element_type=F32)
    h2 = _rms(x, g2_ref[...], D_MODEL)
    if front > 0:
        rows = j * tm + _iota((tm, 1), 0)
        h2 = jnp.where(rows >= front, h2, 0.0)
    h2 = h2.astype(BF16)
    acc = jnp.zeros((tm, D_MODEL), F32)
    w2 = 2 * FFN_CW
    n_f = D_FF // FFN_CW
    up = lambda f: jnp.dot(h2, wup_ref[:, f * w2:(f + 1) * w2], preferred_element_type=F32)
    u_next = up(0)
    acts, k0 = [], 0
    for f in range(n_f):
        cols = slice(f * w2, (f + 1) * w2)
        u = u_next
        if f + 1 < n_f:
            u_next = up(f + 1)
        work = (work_a, work_b)[f % 2]
        work[0:SUBLANE, :] = carry[:, cols]
        work[SUBLANE:SUBLANE + tm, :] = u
        cw = cw_ref[:, cols]
        y = (work[pl.ds(SUBLANE - 2, tm), :] * cw[0:1, :]
             + work[pl.ds(SUBLANE - 1, tm), :] * cw[1:2, :] + u * cw[2:3, :])
        carry[:, cols] = work[tm:tm + SUBLANE, :]
        acts.append((_silu(y[:, 0:FFN_CW]) * y[:, FFN_CW:w2]).astype(BF16))
        if len(acts) == FFN_DOWN_GROUP or f + 1 == n_f:
            k1 = k0 + FFN_CW * len(acts)
            acc = acc + jnp.dot(jnp.concatenate(acts, axis=1) if len(acts) > 1 else acts[0],
                                wdn_ref[k0:k1, :], preferred_element_type=F32)
            acts, k0 = [], k1
    xo_ref[0] = x + acc
    hist_out[0] = carry[...]


def _final_norm_kernel(x_ref, g_ref, o_ref):
    o_ref[0] = _rms(x_ref[0], g_ref[...], D_MODEL)


def _const_spec(arr):
    nd = arr.ndim
    return pl.BlockSpec(arr.shape, lambda b, j: (0,) * nd, pipeline_mode=pl.Buffered(1))


def _tile_spec(tm, width):
    return pl.BlockSpec((1, tm, width), lambda b, j: (b, j, 0))


def _batch_spec(rows, width):
    return pl.BlockSpec((1, rows, width), lambda b, j: (b, 0, 0))


def _params():
    return pltpu.CompilerParams(dimension_semantics=("arbitrary", "arbitrary"),
                                vmem_limit_bytes=VMEM_LIMIT)


def _cols_spec(rows, tm):
    return pl.BlockSpec((1, rows, tm), lambda b, j: (b, 0, j))


def _mla_prep(x, g1, wa, gq, wq1t, wq2t, gkv, tabq, tabk, tm, front):
    b, l, _ = x.shape
    consts = (g1, wa, gq, wq1t, wq2t, gkv)
    return pl.pallas_call(
        functools.partial(_mla_prep_kernel, tm=tm, front=front),
        grid=(b, l // tm),
        in_specs=[_tile_spec(tm, D_MODEL)] + [_const_spec(a) for a in consts]
        + [pl.BlockSpec((2 * LANE, tm), lambda bb, j: (0, j)),
           pl.BlockSpec((tm, 2 * LANE), lambda bb, j: (j, 0))],
        out_specs=[_cols_spec(512, tm), _tile_spec(tm, A_KVRANK), _tile_spec(tm, A_ROPE)],
        out_shape=[jax.ShapeDtypeStruct((b, 512, l), BF16),
                   jax.ShapeDtypeStruct((b, l, A_KVRANK), F32),
                   jax.ShapeDtypeStruct((b, l, A_ROPE), F32)],
        compiler_params=_params(), name="mla_prep",
    )(x, *consts, tabq, tabk)


def _kv_up(c_all, kr_all, wk, ek, wvt, ones_col, tm):
    b, n, _ = c_all.shape
    consts = (wk, ek, wvt, ones_col)
    return pl.pallas_call(
        _kv_up_kernel,
        grid=(b, n // tm),
        in_specs=[_tile_spec(tm, A_KVRANK), _tile_spec(tm, A_ROPE)] + [_const_spec(a) for a in consts],
        out_specs=[_tile_spec(tm, 512), pl.BlockSpec((1, 1, 512, tm), lambda bb, j: (bb, j, 0, 0))],
        out_shape=[jax.ShapeDtypeStruct((b, n, 512), BF16),
                   jax.ShapeDtypeStruct((b, n // tm, 512, tm), BF16)],
        compiler_params=_params(), name="kv_up",
    )(c_all, kr_all, *consts)


def _flash(qt, k, vt, gout_col, tq, tk, causal, klo, khi):
    b, _, l = qt.shape
    n = k.shape[1]
    nkv = n // tk
    return pl.pallas_call(
        functools.partial(_flash_kernel, tq=tq, tk=tk, nkv=nkv, causal=causal, klo=klo, khi=khi),
        grid=(b, l // tq),
        in_specs=[_cols_spec(512, tq),
                  pl.BlockSpec((1, n, 512), lambda bb, j: (bb, 0, 0), pipeline_mode=pl.Buffered(1)),
                  pl.BlockSpec((1, nkv, 512, tk), lambda bb, j: (bb, 0, 0, 0), pipeline_mode=pl.Buffered(1)),
                  _const_spec(gout_col)],
        out_specs=_cols_spec(GROUP_W, tq),
        out_shape=jax.ShapeDtypeStruct((b, GROUP_W, l), F32),
        scratch_shapes=[pltpu.VMEM((N_HEADS * SUBLANE, tq), F32), pltpu.VMEM((N_HEADS * LANE, tq), F32),
                        pltpu.VMEM((N_HEADS * tk, tq), F32), pltpu.VMEM((N_HEADS * tk, tq), F32)],
        compiler_params=_params(), name="mla_flash",
    )(qt, k, vt, gout_col)


def _stack_consts(c):
    n = N_HEADS * c
    lc = int(math.log2(c))
    r, l = np.arange(n)[:, None], np.arange(GROUP_W)[None, :]
    hm = ((r >> lc) == (l >> 6)).astype(np.float32)
    rr, cc = np.arange(n)[:, None], np.arange(n)[None, :]
    same = (rr >> lc) == (cc >> lc)
    tri = [same & (cc <= rr), same & (cc < rr), rr == cc]
    for lb in range(lc):
        tri.append(((rr >> (lb + 1)) == (cc >> (lb + 1))) & (((rr >> lb) & 1) == 1) & (((cc >> lb) & 1) == 0))
    tl = np.arange(c)[None, :] <= np.arange(c)[:, None]
    hh = np.arange(GROUP_W)
    bd = (hh[:, None] >> 6) == (hh[None, :] >> 6)
    return (jnp.asarray(hm), jnp.asarray(np.stack(tri).astype(np.float32)),
            jnp.asarray(tl, BF16), jnp.asarray(bd, BF16))


def _scan_call(kernel, name, x, consts_a, hist, s0, consts_b, width_in, tm, c, front, grp):
    b, l, _ = x.shape
    assert b % grp == 0
    consts_b = tuple(consts_b) + _stack_consts(c)
    rows_spec = lambda rows, width: pl.BlockSpec((grp, rows, width), lambda bb, j: (bb, 0, 0))
    return pl.pallas_call(
        functools.partial(kernel, tm=tm, c=c, front=front, grp=grp),
        grid=(b // grp, l // tm),
        in_specs=[pl.BlockSpec((grp, tm, D_MODEL), lambda bb, j: (bb, j, 0))]
        + [_const_spec(a) for a in consts_a]
        + [rows_spec(SUBLANE, width_in), rows_spec(GROUP_W, GROUP_W)]
        + [_const_spec(a) for a in consts_b],
        out_specs=[pl.BlockSpec((grp, tm, GROUP_W), lambda bb, j: (bb, j, 0)),
                   rows_spec(GROUP_W, GROUP_W), rows_spec(SUBLANE, width_in)],
        out_shape=[jax.ShapeDtypeStruct((b, l, GROUP_W), F32),
                   jax.ShapeDtypeStruct((b, GROUP_W, GROUP_W), F32),
                   jax.ShapeDtypeStruct((b, SUBLANE, width_in), F32)],
        scratch_shapes=[pltpu.VMEM((grp, tm + SUBLANE, width_in), F32),
                        pltpu.VMEM((grp, GROUP_W, GROUP_W), F32)],
        compiler_params=_params(), name=name,
    )(x, *consts_a, hist, s0, *consts_b)


def _ffn(x, ya, yb, yc, yd, wo, g2, wup, cw, wdn, hist, tm, front):
    b, l, _ = x.shape
    return pl.pallas_call(
        functools.partial(_ffn_kernel, tm=tm, front=front),
        grid=(b, l // tm),
        in_specs=[_tile_spec(tm, D_MODEL), _cols_spec(GROUP_W, tm)] + [_tile_spec(tm, GROUP_W)] * 3
        + [_const_spec(a) for a in (wo, g2, wup, cw, wdn)] + [_batch_spec(SUBLANE, 2 * D_FF)],
        out_specs=[_tile_spec(tm, D_MODEL), _batch_spec(SUBLANE, 2 * D_FF)],
        out_shape=[jax.ShapeDtypeStruct((b, l, D_MODEL), F32),
                   jax.ShapeDtypeStruct((b, SUBLANE, 2 * D_FF), F32)],
        scratch_shapes=[pltpu.VMEM((SUBLANE, 2 * D_FF), F32),
                        pltpu.VMEM((tm + SUBLANE, 2 * FFN_CW), F32),
                        pltpu.VMEM((tm + SUBLANE, 2 * FFN_CW), F32)],
        compiler_params=_params(), name="out_ffn",
    )(x, ya, yb, yc, yd, wo, g2, wup, cw, wdn, hist)


def _final_norm(x, g, tm, skip_tiles, out_rows):
    b = x.shape[0]
    return pl.pallas_call(
        _final_norm_kernel,
        grid=(b, out_rows // tm),
        in_specs=[pl.BlockSpec((1, tm, D_MODEL), lambda bb, j: (bb, j + skip_tiles, 0)), _const_spec(g)],
        out_specs=_tile_spec(tm, D_MODEL),
        out_shape=jax.ShapeDtypeStruct((b, out_rows, D_MODEL), F32),
        compiler_params=_params(), name="final_norm",
    )(x, g)


def _np_idx():
    z = IN_COLS
    zpad = lambda n: [z] * n
    rep = lambda base: [base + i for i in range(N_HEADS) for _ in range(HEAD_W)]
    grp = lambda base: [base + g * HEAD_W + i for g in (0, 0, 1, 1) for i in range(HEAD_W)]
    a = (list(range(0, 192)) + zpad(64) + list(range(192, 320))
         + list(range(320, 352)) + zpad(96)
         + list(range(336, 352)) + list(range(320, 336)) + zpad(96))
    b0 = A_COLS
    bcols = list(range(b0, b0 + B_COLS))
    c0 = b0 + B_COLS
    ccols = (list(range(c0, c0 + 256)) + list(range(c0 + 256, c0 + 512))
             + grp(c0 + 512) + grp(c0 + 640) + rep(c0 + 768))
    d0 = c0 + C_COLS
    dcols = list(range(d0, d0 + 1024)) + rep(d0 + 1024) + rep(d0 + 1028)
    xbc_exp = list(range(256)) + grp(256) + grp(384)
    xbc_back = (list(range(256)) + list(range(256, 320)) + list(range(384, 448))
                + list(range(512, 576)) + list(range(640, 704)))
    ffn_perm = []
    for f in range(D_FF // FFN_CW):
        ffn_perm += list(range(f * FFN_CW, (f + 1) * FFN_CW))
        ffn_perm += list(range(D_FF + f * FFN_CW, D_FF + (f + 1) * FFN_CW))
    ffn_back = np.argsort(np.array(ffn_perm))
    as_i = lambda v: np.asarray(v, np.int32)
    return dict(a=as_i(a), b=as_i(bcols), c=as_i(ccols), d=as_i(dcols), xbc_exp=as_i(xbc_exp),
                xbc_back=as_i(xbc_back), ffn_perm=as_i(ffn_perm), ffn_back=as_i(ffn_back))


_IDX = _np_idx()


def _rep_heads(v):
    return jnp.repeat(v, HEAD_W, axis=-1)


def _pad_rows(w, rows=SUBLANE):
    return jnp.pad(w, [(0, rows - w.shape[0])] + [(0, 0)] * (w.ndim - 1))


def _layer_consts(P, l):
    row = lambda v: v.reshape(1, -1).astype(F32)
    w_in = jnp.concatenate([P['w_in'][l], jnp.zeros((D_MODEL, 1), F32)], axis=1)
    c = {}
    c['g1'] = row(P['norm1_g'][l])
    c['wa'] = w_in[:, _IDX['a']].astype(BF16)
    c['wb'] = w_in[:, _IDX['b']].astype(BF16)
    c['wc'] = w_in[:, _IDX['c']].astype(BF16)
    c['wd'] = w_in[:, _IDX['d']].astype(BF16)
    c['gq'] = row(jnp.pad(P['a_gq'][l], (0, 64)))
    wuq = P['a_wuq'][l].reshape(A_QRANK, N_HEADS, A_NOPE + A_ROPE)
    rope = wuq[:, :, A_NOPE:]
    swap = jnp.concatenate([rope[..., 16:], rope[..., :16]], axis=-1)
    zeros = lambda n: jnp.zeros((A_QRANK, N_HEADS, n), F32)
    wq1 = jnp.concatenate([wuq, zeros(32)], axis=-1).reshape(A_QRANK, 512)
    wq2 = jnp.concatenate([zeros(64), swap, zeros(32)], axis=-1).reshape(A_QRANK, 512)
    c['wq1t'] = jnp.pad(wq1, ((0, 64), (0, 0))).T.astype(BF16)
    c['wq2t'] = jnp.pad(wq2, ((0, 64), (0, 0))).T.astype(BF16)
    c['gkv'] = row(P['a_gkv'][l])
    wuk = P['a_wuk'][l].reshape(A_KVRANK, N_HEADS, A_NOPE)
    c['wk'] = jnp.concatenate([wuk, jnp.zeros((A_KVRANK, N_HEADS, 64), F32)], axis=-1
                              ).reshape(A_KVRANK, 512).astype(BF16)
    ek = np.zeros((A_ROPE, N_HEADS, LANE), np.float32)
    for hh in range(N_HEADS):
        ek[np.arange(A_ROPE), hh, A_NOPE + np.arange(A_ROPE)] = 1.0
    c['ek'] = jnp.asarray(ek.reshape(A_ROPE, 512), BF16)
    wuv = P['a_wuv'][l].reshape(A_KVRANK, N_HEADS, HEAD_W)
    c['wvt'] = jnp.concatenate([wuv, jnp.zeros((A_KVRANK, N_HEADS, HEAD_W), F32)], axis=-1
                               ).reshape(A_KVRANK, 512).T.astype(BF16)
    ones = np.zeros((N_HEADS, LANE, 1), np.float32)
    ones[:, HEAD_W, 0] = 1.0
    c['ones_col'] = jnp.asarray(ones.reshape(512, 1))
    c['gout_col'] = P['a_gout'][l].reshape(GROUP_W, 1).astype(F32)
    c['mu'] = row(P['b_mu'][l])
    c['w0'] = row(P['b_w0'][l])
    z64 = jnp.zeros((64, GROUP_W), F32)
    c['wl'] = jnp.concatenate([jnp.concatenate([P['b_w2'][l], z64], axis=1),
                               jnp.concatenate([z64, P['b_a2'][l]], axis=1)], axis=0).astype(BF16)
    c['a0'] = row(P['b_a0'][l])
    c['g2b'] = P['b_g2'][l].astype(BF16)
    c['kk'] = row(P['b_kk'][l])
    c['ka'] = row(P['b_ka'][l])
    c['rk'] = row(P['b_rk'][l])
    c['gnw'] = row(P['b_gnw'][l])
    c['gnb'] = row(P['b_gnb'][l])
    c['c_cw'] = _pad_rows(P['c_convw'][l][:, _IDX['xbc_exp']])
    c['c_cb'] = row(P['c_convb'][l][_IDX['xbc_exp']])
    c['c_dtb'] = row(_rep_heads(P['c_dtb'][l]))
    c['c_alog'] = row(_rep_heads(P['c_alog'][l]))
    c['c_d'] = row(_rep_heads(P['c_d'][l]))
    c['c_gn'] = row(P['c_gnorm'][l])
    c['d_cw'] = _pad_rows(P['d_convw'][l])
    c['d_alog'] = row(_rep_heads(P['d_alog'][l]))
    c['d_dtb'] = row(_rep_heads(P['d_dtb'][l]))
    c['d_gn'] = row(jnp.tile(P['d_gnorm'][l], N_HEADS))
    c['wo'] = P['w_out'][l].astype(BF16)
    c['g2'] = row(P['norm2_g'][l])
    c['wup'] = P['f_wup'][l][:, _IDX['ffn_perm']].astype(BF16)
    c['f_cw'] = _pad_rows(P['f_convw'][l][:, _IDX['ffn_perm']])
    c['wdn'] = P['f_wdown'][l].astype(BF16)
    return c


def _embed_bd(s):
    b = s.shape[0]
    eye = jnp.eye(N_HEADS, dtype=s.dtype)
    return jnp.einsum('bhij,hg->bhigj', s, eye).reshape(b, GROUP_W, GROUP_W)


def _extract_bd(s):
    b = s.shape[0]
    s5 = s.reshape(b, N_HEADS, HEAD_W, N_HEADS, HEAD_W)
    return jnp.stack([s5[:, hh, :, hh, :] for hh in range(N_HEADS)], axis=1)


def _hist8(hist):
    return jnp.pad(hist, ((0, 0), (SUBLANE - hist.shape[1], 0), (0, 0)))


def _rope_table(pos):
    half = A_ROPE // 2
    inv = jnp.power(ROPE_BASE, -jnp.arange(half, dtype=F32) / half)
    ang = pos.astype(F32)[:, None] * inv
    cos, sin = jnp.cos(ang), jnp.sin(ang)
    cos2 = jnp.concatenate([cos, cos], axis=-1)
    sin2 = jnp.concatenate([-sin, sin], axis=-1)
    n = pos.shape[0]
    one = jnp.ones((n, A_NOPE), F32)
    z = lambda w: jnp.zeros((n, w), F32)
    tabq = jnp.concatenate([one, cos2, z(32), z(64), sin2, z(32)], axis=-1).T
    tabk = jnp.concatenate([cos2, z(96), sin2, z(96)], axis=-1)
    return tabq, tabk


def _trunk(x, pos, front, st, P, *, tm, tm_ffn, c, tq, causal, n_keys_pad):
    b, l, _ = x.shape
    tabq, tabk = _rope_table(pos)
    new = {name: [] for name in ('ckv', 'krope', 'rwkv_S', 'rwkv_shift', 'ssd_S', 'ssd_conv',
                                 'gdn_S', 'gdn_conv', 'ffn_conv')}
    zeros_bd = jnp.zeros((b, GROUP_W, GROUP_W), F32)
    for li in range(DEPTH):
        c_ = P[li]
        qt, ckv, krope = _mla_prep(x, c_['g1'], c_['wa'], c_['gq'], c_['wq1t'], c_['wq2t'], c_['gkv'],
                                   tabq, tabk, tm_ffn, front)
        if st is None:
            c_all, kr_all, klo, khi = ckv, krope, front, l
            s_b = s_c = s_d = zeros_bd
            shift8 = jnp.zeros((b, SUBLANE, B_COLS), F32)
            chist = jnp.zeros((b, SUBLANE, 768), F32)
            dhist = jnp.zeros((b, SUBLANE, 768), F32)
            fhist = jnp.zeros((b, SUBLANE, 2 * D_FF), F32)
        else:
            past = st['ckv'].shape[2]
            padk = n_keys_pad - past - l
            c_all = jnp.concatenate([st['ckv'][li], ckv, jnp.zeros((b, padk, A_KVRANK), F32)], axis=1)
            kr_all = jnp.concatenate([st['krope'][li], krope, jnp.zeros((b, padk, A_ROPE), F32)], axis=1)
            klo, khi = 0, past + l
            s_b = _embed_bd(st['rwkv_S'][li])
            s_c = _embed_bd(jnp.swapaxes(st['ssd_S'][li], -1, -2))
            s_d = _embed_bd(st['gdn_S'][li])
            shift8 = _hist8(st['rwkv_shift'][li][:, None, :])
            chist = _hist8(st['ssd_conv'][li][:, :, _IDX['xbc_exp']])
            dhist = _hist8(st['gdn_conv'][li])
            fhist = _hist8(st['ffn_conv'][li][:, :, _IDX['ffn_perm']])
        nk = c_all.shape[1]
        tkv = tq if causal else nk
        kf, vt = _kv_up(c_all, kr_all, c_['wk'], c_['ek'], c_['wvt'], c_['ones_col'], tkv)
        ya = _flash(qt, kf, vt, c_['gout_col'], tq, tkv, causal, klo, khi)
        yb, sb_new, shift_new = _scan_call(
            _rwkv_kernel, "rwkv7", x, (c_['g1'], c_['wb']), shift8, s_b,
            (c_['mu'], c_['w0'], c_['wl'], c_['a0'], c_['g2b'], c_['kk'], c_['ka'], c_['rk'],
             c_['gnw'], c_['gnb']), B_COLS, tm, c, front, SCAN_ROWS)
        yc, sc_new, chist_new = _scan_call(
            _ssd_kernel, "ssd", x, (c_['g1'], c_['wc']), chist, s_c,
            (c_['c_cw'], c_['c_cb'], c_['c_dtb'], c_['c_alog'], c_['c_d'], c_['c_gn']),
            768, tm, c, front, SCAN_ROWS)
        yd, sd_new, dhist_new = _scan_call(
            _gdn_kernel, "gdn", x, (c_['g1'], c_['wd']), dhist, s_d,
            (c_['d_cw'], c_['d_alog'], c_['d_dtb'], c_['d_gn']), 768, tm, c, front, SCAN_ROWS)
        x, fhist_new = _ffn(x, ya, yb, yc, yd, c_['wo'], c_['g2'], c_['wup'], c_['f_cw'], c_['wdn'],
                            fhist, tm_ffn, front)
        new['ckv'].append(ckv[:, front:])
        new['krope'].append(krope[:, front:])
        new['rwkv_S'].append(_extract_bd(sb_new))
        new['rwkv_shift'].append(shift_new[:, SUBLANE - 1])
        new['ssd_S'].append(jnp.swapaxes(_extract_bd(sc_new), -1, -2))
        new['ssd_conv'].append(chist_new[:, SUBLANE - (C_CONV - 1):][:, :, _IDX['xbc_back']])
        new['gdn_S'].append(_extract_bd(sd_new))
        new['gdn_conv'].append(dhist_new[:, SUBLANE - (D_CONV - 1):])
        new['ffn_conv'].append(fhist_new[:, SUBLANE - (FFN_CONV - 1):][:, :, _IDX['ffn_back']])
    return x, {name: jnp.stack(vals) for name, vals in new.items()}


def kernel(x_prompt, x_sample, cache_mla_ckv, cache_mla_krope, state_rwkv, state_rwkv_shift, state_ssd, state_ssd_conv, state_gdn, state_gdn_conv, state_ffn_conv, meta_tokens, norm1_g, w_in, a_gq, a_wuq, a_gkv, a_wuk, a_wuv, a_gout, b_mu, b_w0, b_w2, b_a0, b_a2, b_g2, b_kk, b_ka, b_rk, b_gnw, b_gnb, c_convw, c_convb, c_dtb, c_alog, c_d, c_gnorm, d_convw, d_alog, d_dtb, d_gnorm, w_out, norm2_g, f_wup, f_convw, f_wdown, final_g):
    P = dict(norm1_g=norm1_g, w_in=w_in, a_gq=a_gq, a_wuq=a_wuq, a_gkv=a_gkv, a_wuk=a_wuk,
             a_wuv=a_wuv, a_gout=a_gout, b_mu=b_mu, b_w0=b_w0, b_w2=b_w2, b_a0=b_a0, b_a2=b_a2,
             b_g2=b_g2, b_kk=b_kk, b_ka=b_ka, b_rk=b_rk, b_gnw=b_gnw, b_gnb=b_gnb,
             c_convw=c_convw, c_convb=c_convb, c_dtb=c_dtb, c_alog=c_alog, c_d=c_d, c_gnorm=c_gnorm,
             d_convw=d_convw, d_alog=d_alog, d_dtb=d_dtb, d_gnorm=d_gnorm,
             w_out=w_out, norm2_g=norm2_g, f_wup=f_wup, f_convw=f_convw, f_wdown=f_wdown)
    P = [_layer_consts(P, li) for li in range(DEPTH)]
    fin = final_g.reshape(1, -1).astype(F32)
    b_p, seq, _ = x_prompt.shape
    lx = N_META + seq
    assert seq % ATT_BLOCK == 0 and N_META <= ATT_BLOCK
    front = ATT_BLOCK - N_META
    l_pad = front + lx
    meta = jnp.broadcast_to(meta_tokens[None].astype(F32), (b_p, N_META, D_MODEL))
    x_ext = jnp.concatenate([jnp.zeros((b_p, front, D_MODEL), F32), meta, x_prompt], axis=1)
    pos_p = jnp.arange(l_pad, dtype=jnp.int32) - (front + N_META)
    tm_ffn = max(t for t in (ROW_TILE, 2 * ROW_TILE, 3 * ROW_TILE) if l_pad % t == 0)
    y_p, ns_p = _trunk(x_ext, pos_p, front, None, P, tm=ROW_TILE, tm_ffn=tm_ffn, c=CHUNK, tq=ATT_BLOCK,
                       causal=True, n_keys_pad=l_pad)
    y_prompt = _final_norm(y_p, fin, ROW_TILE, (front + N_META) // ROW_TILE, seq)
    b_s, t_s, _ = x_sample.shape
    past = cache_mla_ckv.shape[2]
    assert t_s <= CHUNK and t_s % 16 == 0 and (t_s & (t_s - 1)) == 0
    st_s = dict(ckv=cache_mla_ckv, krope=cache_mla_krope, rwkv_S=state_rwkv, rwkv_shift=state_rwkv_shift,
                ssd_S=state_ssd, ssd_conv=state_ssd_conv, gdn_S=state_gdn, gdn_conv=state_gdn_conv,
                ffn_conv=state_ffn_conv)
    pos_s = past + jnp.arange(t_s, dtype=jnp.int32)
    n_keys_pad = -(-(past + t_s) // LANE) * LANE
    y_s, ns_s = _trunk(x_sample, pos_s, 0, st_s, P, tm=t_s, tm_ffn=t_s, c=t_s, tq=t_s, causal=False,
                       n_keys_pad=n_keys_pad)
    y_sample = _final_norm(y_s, fin, t_s, 0, t_s)
    keys = ('ckv', 'krope', 'rwkv_S', 'rwkv_shift', 'ssd_S', 'ssd_conv', 'gdn_S', 'gdn_conv', 'ffn_conv')
    return (y_prompt, y_sample) + tuple(ns_p[k] for k in keys) + tuple(ns_s[k] for k in keys)
```

```python
import functools
import math

import numpy as np
import jax
import jax.numpy as jnp
from jax import lax
from jax.experimental import pallas as pl
from jax.experimental.pallas import tpu as pltpu

F32 = jnp.float32
BF16 = jnp.bfloat16

D_MODEL = 1024
DEPTH = 4
CHUNK = 64
N_META = 16
EPS = 1e-6
L2_EPS = 1e-6
GROUP_W = 256
N_HEADS = 4
HEAD_W = 64
A_NOPE = 64
A_ROPE = 32
A_QRANK = 192
A_KVRANK = 128
A_SCALE = (A_NOPE + A_ROPE) ** -0.5
ROPE_BASE = 10000.0
B_GN_EPS = 64e-5
B_COLS = 1024
C_CONV = 4
D_CONV = 4
D_FF = 2816
FFN_CONV = 3
A_COLS = 352
C_COLS = 772
D_COLS = 1032
IN_COLS = A_COLS + B_COLS + C_COLS + D_COLS

LANE = 128
SUBLANE = 8
VMEM_LIMIT = 56 * 1024 * 1024
NEG = -1e30
LOG2E = math.log2(math.e)
FFN_CW = 256
FFN_DOWN_GROUP = 4
ATT_BLOCK = 256
ROW_TILE = 256
SCAN_ROWS = 2
SCAN_ROWS_SHORT = 8


def _mm(a, b):
    return jnp.dot(a.astype(BF16), b.astype(BF16), preferred_element_type=F32)


def _mm_nt(a, b):
    return lax.dot_general(a.astype(BF16), b.astype(BF16), (((1,), (1,)), ((), ())),
                           preferred_element_type=F32)


def _mm_tn(a, b):
    return lax.dot_general(a.astype(BF16), b.astype(BF16), (((0,), (0,)), ((), ())),
                           preferred_element_type=F32)


def _split2(x):
    hi = x.astype(BF16)
    lo = (x - hi.astype(F32)).astype(BF16)
    return hi, lo


def _mm_x2(x, w):
    hi, lo = _split2(x)
    return (jnp.dot(hi, w, preferred_element_type=F32)
            + jnp.dot(lo, w, preferred_element_type=F32))


def _rms(x, g, n):
    ms = jnp.sum(x * x, axis=-1, keepdims=True) * (1.0 / n)
    return x * lax.rsqrt(ms + EPS) * g


def _softplus(x):
    return jnp.maximum(x, 0.0) + jnp.log1p(jnp.exp(-jnp.abs(x)))


def _sigmoid(x):
    return 1.0 / (1.0 + jnp.exp(-x))


def _silu(x):
    return x * _sigmoid(x)


def _iota(shape, axis):
    return lax.broadcasted_iota(jnp.int32, shape, axis)


def _norm_in(x_ref, g_ref, j, tm, front):
    h = _rms(x_ref[0], g_ref[...], D_MODEL)
    if front > 0:
        rows = j * tm + _iota((tm, 1), 0)
        h = jnp.where(rows >= front, h, 0.0)
    return h.astype(BF16)


def _norm_in_rows(x_ref, g_ref, j, tm, front):
    x = x_ref[...]
    h = _rms(x, g_ref[...], D_MODEL)
    if front > 0:
        pos = j * tm + _iota((1, tm, 1), 1)
        h = jnp.where(pos >= front, h, 0.0)
    return h.reshape(x.shape[0] * tm, D_MODEL).astype(BF16)


def _hsum(x, bd):
    return _mm_x2(x, bd)


TRI_INCL, TRI_STRICT = 0, 1


class _Stk:
    def __init__(self, c, hm_ref, tri_ref, join_ref, tl_ref):
        self.c = c
        self.n = N_HEADS * c
        self.steps = int(math.log2(c))
        assert 1 << self.steps == c
        self.hm_ref, self.tri_ref, self.join_ref, self.tl_ref = hm_ref, tri_ref, join_ref, tl_ref
        self.hm_b = None

    def mask(self, which):
        return self.tri_ref[which]

    def tile(self, x):
        return jnp.concatenate([x] * N_HEADS, axis=0)

    def stack(self, x):
        return self.tile(x) * self.hm_ref[...]

    def unstack(self, xs):
        c = self.c
        return xs[0:c] + xs[c:2 * c] + xs[2 * c:3 * c] + xs[3 * c:4 * c]

    def cumsum(self, x):
        hi, lo = _split2(x)
        tl = self.tl_ref[...]
        return jnp.dot(tl, hi, preferred_element_type=F32) + jnp.dot(tl, lo, preferred_element_type=F32)

    def decay(self, g_cum):
        gcol = jnp.min(self.stack(g_cum), axis=-1, keepdims=True)
        grow = jnp.broadcast_to(gcol, (self.n, LANE)).T[0:1, :]
        return jnp.exp(jnp.minimum(gcol - grow, 0.0)) * self.mask(TRI_INCL)

    def tri_inv_many(self, lowers):
        lowers = [lw.astype(BF16) for lw in lowers]
        minvs = [self.join_ref[self.steps] - lw * self.join_ref[0] for lw in lowers]
        for lb in range(1, self.steps):
            ts = [_mm(m, lw * self.join_ref[lb]) for m, lw in zip(minvs, lowers)]
            minvs = [m - _mm(t, m).astype(BF16) for t, m in zip(ts, minvs)]
        return minvs

    def stack_b(self, x):
        if self.hm_b is None:
            self.hm_b = self.hm_ref[...].astype(BF16)
        return self.tile(x.astype(BF16)) * self.hm_b


def _mla_prep_kernel(x_ref, g1_ref, wa_ref, gq_ref, wq1t_ref, wq2t_ref, gkv_ref, tabq_ref, tabk_ref,
                     qt_out, ckv_out, kr_out, *, tm, front):
    j = pl.program_id(1)
    h = _norm_in(x_ref, g1_ref, j, tm, front)
    pa = jnp.dot(h, wa_ref[...], preferred_element_type=F32)
    qn = _rms(pa[:, 0:256], gq_ref[...], A_QRANK)
    q1t = _mm_nt(wq1t_ref[...], qn)
    q2t = _mm_nt(wq2t_ref[...], qn)
    tabq = tabq_ref[...]
    cos4 = jnp.concatenate([tabq[0:128]] * N_HEADS, axis=0)
    sin4 = jnp.concatenate([tabq[128:256]] * N_HEADS, axis=0)
    qt_out[0] = ((q1t * cos4 + q2t * sin4) * (A_SCALE * LOG2E)).astype(BF16)
    ckv_out[0] = _rms(pa[:, 256:384], gkv_ref[...], A_KVRANK)
    tabk = tabk_ref[...]
    kr = pa[:, 384:512] * tabk[:, 0:128] + pa[:, 512:640] * tabk[:, 128:256]
    kr_out[0] = kr[:, 0:A_ROPE]


def _kv_up_kernel(c_ref, kr_ref, wk_ref, ek_ref, wvt_ref, ones_ref, k_out, vt_out):
    c = c_ref[0].astype(BF16)
    kr = kr_ref[0].astype(BF16)
    k = (jnp.dot(c, wk_ref[...], preferred_element_type=F32)
         + jnp.dot(kr, ek_ref[...], preferred_element_type=F32))
    k_out[0] = k.astype(BF16)
    vt_out[0, 0] = (_mm_nt(wvt_ref[...], c) + ones_ref[...]).astype(BF16)


def _flash_kernel(qt_ref, k_ref, vt_ref, gout_ref, o_ref, m_sc, acc_sc, sa_sc, sb_sc,
                  *, tq, tk, nkv, causal, klo, khi):
    i = pl.program_id(1)
    m_sc[...] = jnp.full((N_HEADS * SUBLANE, tq), NEG, F32)
    acc_sc[...] = jnp.zeros((N_HEADS * LANE, tq), F32)
    heads = [slice(LANE * h, LANE * (h + 1)) for h in range(N_HEADS)]
    s_bufs = (sa_sc, sb_sc)

    def produce(jb, slot):
        start = pl.multiple_of(jb * tk, tk)
        for h, rows in enumerate(heads):
            s_bufs[slot][h * tk:(h + 1) * tk, :] = jnp.dot(
                k_ref[0, pl.ds(start, tk), rows], qt_ref[0, rows, :], preferred_element_type=F32)

    def consume(jb, slot, masked, nxt):
        if nxt is not None:
            produce(*nxt)
        if masked:
            kpos = jb * tk + _iota((tk, 1), 0)
            qpos = i * tq + _iota((1, tq), 1)
            vis = (kpos >= klo) & (kpos < khi)
            if causal:
                vis = vis & ((kpos >> 6) <= (qpos >> 6))
        m_all = m_sc[...]
        acc_all = acc_sc[...]
        m_out, alphas, ps = [], [], []
        for h in range(N_HEADS):
            s = s_bufs[slot][h * tk:(h + 1) * tk, :]
            if masked:
                s = jnp.where(vis, s, NEG)
            m_prev = m_all[SUBLANE * h:SUBLANE * (h + 1)]
            m_new = jnp.maximum(m_prev, jnp.max(s, axis=0, keepdims=True))
            alphas.append(jnp.exp2(m_prev[0:1] - m_new[0:1]))
            ps.append(jnp.exp2(s - m_new[0:1]).astype(BF16))
            m_out.append(m_new)
        acc_out = [alphas[h] * acc_all[rows]
                   + jnp.dot(vt_ref[0, jb, rows, :], ps[h], preferred_element_type=F32)
                   for h, rows in enumerate(heads)]
        m_sc[...] = jnp.concatenate(m_out, axis=0)
        acc_sc[...] = jnp.concatenate(acc_out, axis=0)

    produce(0, 0)
    if causal:
        @pl.when(i == 0)
        def _():
            consume(0, 0, True, None)

        @pl.when(i > 0)
        def _():
            consume(0, 0, True, (1, 1))
            quads = (i - 1) >> 2

            def pair(jb):
                consume(jb, 1, False, (jb + 1, 0))
                consume(jb + 1, 0, False, (jb + 2, 1))

            def body(t, carry):
                pair(1 + 4 * t)
                pair(3 + 4 * t)
                return carry

            lax.fori_loop(0, quads, body, 0)

            @pl.when(((i - 1) & 2) != 0)
            def _():
                pair(1 + 4 * quads)

            @pl.when((i & 1) == 0)
            def _():
                consume(i - 1, 1, False, (i, 0))
                consume(i, 0, True, None)

            @pl.when((i & 1) == 1)
            def _():
                consume(i, 1, True, None)
    else:
        for jb in range(nkv):
            consume(jb, jb & 1, True, (jb + 1, (jb + 1) & 1) if jb + 1 < nkv else None)
    outs = []
    for h in range(N_HEADS):
        a = acc_sc[LANE * h:LANE * (h + 1), :]
        outs.append(a[0:HEAD_W] / a[HEAD_W:HEAD_W + 1])
    yat = jnp.concatenate(outs, axis=0)
    ms = jnp.sum(yat * yat, axis=0, keepdims=True) * (1.0 / GROUP_W)
    o_ref[0] = yat * lax.rsqrt(ms + EPS) * gout_ref[...]


def _chunk_ids(grp, tm, c):
    ids = [(g, ci) for ci in range(tm // c) for g in range(grp)]
    return ids, {(g, ci): slice(g * tm + ci * c, g * tm + (ci + 1) * c) for g, ci in ids}


def _carry_rows(work, g, new_rows, tm):
    work[g, SUBLANE:SUBLANE + tm, :] = new_rows
    return work[g, tm:tm + SUBLANE, :]


def _rwkv_kernel(x_ref, g1_ref, wb_ref, shift_ref, s0_ref, mu_ref, w0_ref, wl_ref, a0_ref,
                 g2_ref, kk_ref, ka_ref, rk_ref, gnw_ref, gnb_ref, hm_ref, tri_ref, join_ref, tl_ref, bd_ref,
                 y_out, s_out, shift_out, work, st, *, tm, c, front, grp):
    j = pl.program_id(1)

    @pl.when(j == 0)
    def _():
        work[:, 0:SUBLANE, :] = shift_ref[...]
        st[...] = s0_ref[...]

    h = _norm_in_rows(x_ref, g1_ref, j, tm, front)
    cols = jnp.dot(h, wb_ref[...], preferred_element_type=F32)
    shifted = []
    for g in range(grp):
        tail = _carry_rows(work, g, cols[g * tm:(g + 1) * tm], tm)
        shifted.append(work[g, pl.ds(SUBLANE - 1, tm), :])
        work[g, 0:SUBLANE, :] = tail
        shift_out[g] = tail
    shifted = jnp.concatenate(shifted, axis=0)
    xm = cols + (shifted - cols) * mu_ref[...]
    r = xm[:, 0:256]
    k = xm[:, 256:512]
    v = xm[:, 512:768]
    lora = xm[:, 768:896]
    dg = xm[:, 896:1024]
    lora = jnp.where(_iota((grp * tm, LANE), 1) < 64, jnp.tanh(lora), lora)
    ll = _mm(lora, wl_ref[...])
    w_log = -_softplus(-(w0_ref[...] + ll[:, 0:256])) - 0.5
    logd = -jnp.exp(w_log)
    a = _sigmoid(a0_ref[...] + ll[:, 256:512])
    g_gate = _mm(_sigmoid(dg), g2_ref[...])
    bd = bd_ref[...]
    kkr = k * kk_ref[...]
    kk = kkr * lax.rsqrt(_hsum(kkr * kkr, bd) + L2_EPS)
    k2 = k * (1.0 + (a - 1.0) * ka_ref[...])

    sk = _Stk(c, hm_ref, tri_ref, join_ref, tl_ref)
    ids, rows = _chunk_ids(grp, tm, c)
    strict, incl = sk.mask(TRI_STRICT), sk.mask(TRI_INCL)
    gcs = {i: sk.cumsum(logd[rows[i]]) for i in ids}
    pre = {}
    for i in ids:
        gc, ld = gcs[i], logd[rows[i]]
        eg, eng = jnp.exp(gc), jnp.exp(-gc)
        bt = kk[rows[i]] * a[rows[i]] * eng
        kt = k2[rows[i]] * eng
        dc = eg[c - 1:c, :]
        pre[i] = dict(at_s=sk.stack_b(-kk[rows[i]] * jnp.exp(gc - ld)), rt_s=sk.stack_b(r[rows[i]] * eg),
                      v_s=sk.stack_b(v[rows[i]]), bt_t=sk.tile(bt.astype(BF16)),
                      kt_t=sk.tile(kt.astype(BF16)), dc=dc,
                      bd_s=sk.stack_b(bt * dc), kd_s=sk.stack_b(kt * dc))
    lab = {i: _mm_nt(pre[i]['at_s'], pre[i]['bt_t']) * strict for i in ids}
    aak = {i: (_mm_nt(pre[i]['at_s'], pre[i]['kt_t']) * strict).astype(BF16) for i in ids}
    arb = {i: (_mm_nt(pre[i]['rt_s'], pre[i]['bt_t']) * incl).astype(BF16) for i in ids}
    ark = {i: (_mm_nt(pre[i]['rt_s'], pre[i]['kt_t']) * incl).astype(BF16) for i in ids}
    minv = {i: m.astype(BF16) for i, m in zip(ids, sk.tri_inv_many([-lab[i] for i in ids]))}
    a2 = {i: _mm(aak[i], pre[i]['v_s']) for i in ids}
    u0 = {i: _mm(minv[i], a2[i]) for i in ids}
    m1 = {i: _mm(minv[i], pre[i]['at_s']).astype(BF16) for i in ids}
    ork = {i: _mm(ark[i], pre[i]['v_s']) for i in ids}
    skv = {i: _mm_tn(pre[i]['v_s'], pre[i]['kd_s']) for i in ids}

    s = [st[g] for g in range(grp)]
    outs = {}
    for ci in range(tm // c):
        sb = [s[g].astype(BF16) for g in range(grp)]
        us = [u0[(g, ci)] + _mm_nt(m1[(g, ci)], sb[g]) for g in range(grp)]
        oq = [_mm_nt(pre[(g, ci)]['rt_s'], sb[g]) for g in range(grp)]
        for g in range(grp):
            i = (g, ci)
            s[g] = s[g] * pre[i]['dc'] + _mm_tn(us[g], pre[i]['bd_s']) + skv[i]
            outs[i] = sk.unstack(oq[g] + _mm(arb[i], us[g]) + ork[i])
    for g in range(grp):
        st[g] = s[g]
        s_out[g] = s[g]
    o = jnp.concatenate([outs[(g, ci)] for g in range(grp) for ci in range(tm // c)], axis=0)
    mean = _hsum(o, bd) * (1.0 / HEAD_W)
    d = o - mean
    var = _hsum(d * d, bd) * (1.0 / HEAD_W)
    o = d * lax.rsqrt(var + B_GN_EPS) * gnw_ref[...] + gnb_ref[...]
    bonus = _hsum(r * k2 * rk_ref[...], bd) * v
    y_out[...] = ((o + bonus) * g_gate).reshape(grp, tm, GROUP_W)


def _conv4_rows(work, new, wv, hist_out, grp, tm):
    ys = []
    for g in range(grp):
        tail = _carry_rows(work, g, new[g * tm:(g + 1) * tm], tm)
        y = work[g, pl.ds(SUBLANE - 3, tm), :] * wv[0:1, :]
        y = y + work[g, pl.ds(SUBLANE - 2, tm), :] * wv[1:2, :]
        y = y + work[g, pl.ds(SUBLANE - 1, tm), :] * wv[2:3, :]
        ys.append(y + work[g, pl.ds(SUBLANE, tm), :] * wv[3:4, :])
        work[g, 0:SUBLANE, :] = tail
        hist_out[g] = tail
    return jnp.concatenate(ys, axis=0)


def _ssd_kernel(x_ref, g1_ref, wc_ref, hist_ref, s0_ref, cw_ref, cb_ref, dtb_ref, alog_ref,
                dskip_ref, gn_ref, hm_ref, tri_ref, join_ref, tl_ref, bd_ref,
                y_out, s_out, hist_out, work, st, *, tm, c, front, grp):
    j = pl.program_id(1)

    @pl.when(j == 0)
    def _():
        work[:, 0:SUBLANE, :] = hist_ref[...]
        st[...] = s0_ref[...]

    h = _norm_in_rows(x_ref, g1_ref, j, tm, front)
    pc = jnp.dot(h, wc_ref[...], preferred_element_type=F32)
    z = pc[:, 0:256]
    xbc = _silu(_conv4_rows(work, pc[:, 256:1024], cw_ref[...], hist_out, grp, tm) + cb_ref[...])
    xs = xbc[:, 0:256]
    bm = xbc[:, 256:512]
    cm = xbc[:, 512:768]
    dt = _softplus(pc[:, 1024:1280] + dtb_ref[...])
    if front > 0:
        pos = j * tm + _iota((1, tm, 1), 1)
        dt = jnp.where(pos >= front, dt.reshape(grp, tm, GROUP_W), 0.0).reshape(grp * tm, GROUP_W)
    a = dt * (-jnp.exp(alog_ref[...]))
    xdt = xs * dt

    sk = _Stk(c, hm_ref, tri_ref, join_ref, tl_ref)
    ids, rows = _chunk_ids(grp, tm, c)
    acs = {i: sk.cumsum(a[rows[i]]) for i in ids}
    dms = {i: sk.decay(acs[i]) for i in ids}
    xdt_s = {i: sk.stack_b(xdt[rows[i]]) for i in ids}
    amat = {i: _mm_nt(sk.stack_b(cm[rows[i]]), sk.tile(bm[rows[i]].astype(BF16))) * dms[i] for i in ids}
    ydiag = {i: _mm(amat[i], xdt_s[i]) for i in ids}
    sx = {i: _mm_tn(sk.stack_b(bm[rows[i]] * jnp.exp(acs[i][c - 1:c, :] - acs[i])), xdt_s[i]) for i in ids}
    ce_s = {i: sk.stack_b(cm[rows[i]] * jnp.exp(acs[i])) for i in ids}

    s = [st[g] for g in range(grp)]
    outs = {}
    for ci in range(tm // c):
        for g in range(grp):
            i = (g, ci)
            outs[i] = sk.unstack(ydiag[i] + _mm(ce_s[i], s[g]))
            s[g] = s[g] * jnp.exp(acs[i][c - 1:c, :]) + sx[i]
    for g in range(grp):
        st[g] = s[g]
        s_out[g] = s[g]
    y = jnp.concatenate([outs[(g, ci)] for g in range(grp) for ci in range(tm // c)], axis=0)
    y = y + dskip_ref[...] * xs
    y_out[...] = _rms(y * _silu(z), gn_ref[...], GROUP_W).reshape(grp, tm, GROUP_W)


def _gdn_kernel(x_ref, g1_ref, wd_ref, hist_ref, s0_ref, cw_ref, alog_ref, dtb_ref, gn_ref,
                hm_ref, tri_ref, join_ref, tl_ref, bd_ref,
                y_out, s_out, hist_out, work, st, *, tm, c, front, grp):
    j = pl.program_id(1)

    @pl.when(j == 0)
    def _():
        work[:, 0:SUBLANE, :] = hist_ref[...]
        st[...] = s0_ref[...]

    h = _norm_in_rows(x_ref, g1_ref, j, tm, front)
    pd = jnp.dot(h, wd_ref[...], preferred_element_type=F32)
    qkv = _silu(_conv4_rows(work, pd[:, 0:768], cw_ref[...], hist_out, grp, tm))
    z = pd[:, 768:1024]
    beta = _sigmoid(pd[:, 1024:1280])
    g_log = -jnp.exp(alog_ref[...]) * _softplus(pd[:, 1280:1536] + dtb_ref[...])
    bd = bd_ref[...]
    q = qkv[:, 0:256]
    k = qkv[:, 256:512]
    v = qkv[:, 512:768]
    q = q * lax.rsqrt(_hsum(q * q, bd) + L2_EPS) * (HEAD_W ** -0.5)
    k = k * lax.rsqrt(_hsum(k * k, bd) + L2_EPS)

    sk = _Stk(c, hm_ref, tri_ref, join_ref, tl_ref)
    ids, rows = _chunk_ids(grp, tm, c)
    strict = sk.mask(TRI_STRICT)
    gcs = {i: sk.cumsum(g_log[rows[i]]) for i in ids}
    dms = {i: sk.decay(gcs[i]) for i in ids}
    kb = {i: k[rows[i]] * beta[rows[i]] for i in ids}
    k_t = {i: sk.tile(k[rows[i]].astype(BF16)) for i in ids}
    lower = {i: _mm_nt(sk.stack_b(kb[i]), k_t[i]) * dms[i] * strict for i in ids}
    aqk = {i: (_mm_nt(sk.stack_b(q[rows[i]]), k_t[i]) * dms[i]).astype(BF16) for i in ids}
    tinv = {i: t.astype(BF16) for i, t in zip(ids, sk.tri_inv_many([lower[i] for i in ids]))}
    u = {i: _mm(tinv[i], sk.stack_b(v[rows[i]] * beta[rows[i]])) for i in ids}
    w = {i: _mm(tinv[i], sk.stack_b(kb[i] * jnp.exp(gcs[i]))).astype(BF16) for i in ids}
    qe_s = {i: sk.stack_b(q[rows[i]] * jnp.exp(gcs[i])) for i in ids}
    kd_s = {i: sk.stack_b(k[rows[i]] * jnp.exp(gcs[i][c - 1:c, :] - gcs[i])) for i in ids}

    s = [st[g] for g in range(grp)]
    outs = {}
    for ci in range(tm // c):
        sb = [s[g].astype(BF16) for g in range(grp)]
        vn = [u[(g, ci)] - _mm(w[(g, ci)], sb[g]) for g in range(grp)]
        oq = [_mm(qe_s[(g, ci)], sb[g]) for g in range(grp)]
        for g in range(grp):
            i = (g, ci)
            s[g] = s[g] * jnp.exp(gcs[i][c - 1:c, :]) + _mm_tn(kd_s[i], vn[g])
            outs[i] = sk.unstack(oq[g] + _mm(aqk[i], vn[g]))
    for g in range(grp):
        st[g] = s[g]
        s_out[g] = s[g]
    o = jnp.concatenate([outs[(g, ci)] for g in range(grp) for ci in range(tm // c)], axis=0)
    ms = _hsum(o * o, bd) * (1.0 / HEAD_W)
    y_out[...] = (o * lax.rsqrt(ms + EPS) * gn_ref[...] * _silu(z)).reshape(grp, tm, GROUP_W)


def _ffn_kernel(x_ref, ya_ref, yb_ref, yc_ref, yd_ref, wo_ref, g2_ref, wup_ref, cw_ref, wdn_ref,
                hist_ref, xo_ref, hist_out, carry, work_a, work_b, *, tm, front):
    j = pl.program_id(1)

    @pl.when(j == 0)
    def _():
        carry[...] = hist_ref[0]

    x = x_ref[0] + _mm_tn(ya_ref[0], wo_ref[0:GROUP_W, :])
    for idx, y_ref in ((1, yb_ref), (2, yc_ref), (3, yd_ref)):
        x = x + jnp.dot(y_ref[0].astype(BF16), wo_ref[GROUP_W * idx:GROUP_W * (idx + 1), :],
                        preferred_element_type=F32)
    h2 = _rms(x, g2_ref[...], D_MODEL)
    if front > 0:
        rows = j * tm + _iota((tm, 1), 0)
        h2 = jnp.where(rows >= front, h2, 0.0)
    h2 = h2.astype(BF16)
    acc = jnp.zeros((tm, D_MODEL), F32)
    w2 = 2 * FFN_CW
    n_f = D_FF // FFN_CW
    up = lambda f: jnp.dot(h2, wup_ref[:, f * w2:(f + 1) * w2], preferred_element_type=F32)
    u_next = up(0)
    acts, k0 = [], 0
    for f in range(n_f):
        cols = slice(f * w2, (f + 1) * w2)
        u = u_next
        if f + 1 < n_f:
            u_next = up(f + 1)
        work = (work_a, work_b)[f % 2]
        work[0:SUBLANE, :] = carry[:, cols]
        work[SUBLANE:SUBLANE + tm, :] = u
        cw = cw_ref[:, cols]
        y = (work[pl.ds(SUBLANE - 2, tm), :] * cw[0:1, :]
             + work[pl.ds(SUBLANE - 1, tm), :] * cw[1:2, :] + u * cw[2:3, :])
        carry[:, cols] = work[tm:tm + SUBLANE, :]
        acts.append((_silu(y[:, 0:FFN_CW]) * y[:, FFN_CW:w2]).astype(BF16))
        if len(acts) == FFN_DOWN_GROUP or f + 1 == n_f:
            k1 = k0 + FFN_CW * len(acts)
            acc = acc + jnp.dot(jnp.concatenate(acts, axis=1) if len(acts) > 1 else acts[0],
                                wdn_ref[k0:k1, :], preferred_element_type=F32)
            acts, k0 = [], k1
    xo_ref[0] = x + acc
    hist_out[0] = carry[...]


def _final_norm_kernel(x_ref, g_ref, o_ref):
    o_ref[0] = _rms(x_ref[0], g_ref[...], D_MODEL)


def _const_spec(arr):
    nd = arr.ndim
    return pl.BlockSpec(arr.shape, lambda b, j: (0,) * nd, pipeline_mode=pl.Buffered(1))


def _tile_spec(tm, width):
    return pl.BlockSpec((1, tm, width), lambda b, j: (b, j, 0))


def _batch_spec(rows, width):
    return pl.BlockSpec((1, rows, width), lambda b, j: (b, 0, 0))


def _params():
    return pltpu.CompilerParams(dimension_semantics=("arbitrary", "arbitrary"),
                                vmem_limit_bytes=VMEM_LIMIT)


def _cols_spec(rows, tm):
    return pl.BlockSpec((1, rows, tm), lambda b, j: (b, 0, j))


def _mla_prep(x, g1, wa, gq, wq1t, wq2t, gkv, tabq, tabk, tm, front):
    b, l, _ = x.shape
    consts = (g1, wa, gq, wq1t, wq2t, gkv)
    return pl.pallas_call(
        functools.partial(_mla_prep_kernel, tm=tm, front=front),
        grid=(b, l // tm),
        in_specs=[_tile_spec(tm, D_MODEL)] + [_const_spec(a) for a in consts]
        + [pl.BlockSpec((2 * LANE, tm), lambda bb, j: (0, j)),
           pl.BlockSpec((tm, 2 * LANE), lambda bb, j: (j, 0))],
        out_specs=[_cols_spec(512, tm), _tile_spec(tm, A_KVRANK), _tile_spec(tm, A_ROPE)],
        out_shape=[jax.ShapeDtypeStruct((b, 512, l), BF16),
                   jax.ShapeDtypeStruct((b, l, A_KVRANK), F32),
                   jax.ShapeDtypeStruct((b, l, A_ROPE), F32)],
        compiler_params=_params(), name="mla_prep",
    )(x, *consts, tabq, tabk)


def _kv_up(c_all, kr_all, wk, ek, wvt, ones_col, tm):
    b, n, _ = c_all.shape
    consts = (wk, ek, wvt, ones_col)
    return pl.pallas_call(
        _kv_up_kernel,
        grid=(b, n // tm),
        in_specs=[_tile_spec(tm, A_KVRANK), _tile_spec(tm, A_ROPE)] + [_const_spec(a) for a in consts],
        out_specs=[_tile_spec(tm, 512), pl.BlockSpec((1, 1, 512, tm), lambda bb, j: (bb, j, 0, 0))],
        out_shape=[jax.ShapeDtypeStruct((b, n, 512), BF16),
                   jax.ShapeDtypeStruct((b, n // tm, 512, tm), BF16)],
        compiler_params=_params(), name="kv_up",
    )(c_all, kr_all, *consts)


def _flash(qt, k, vt, gout_col, tq, tk, causal, klo, khi):
    b, _, l = qt.shape
    n = k.shape[1]
    nkv = n // tk
    return pl.pallas_call(
        functools.partial(_flash_kernel, tq=tq, tk=tk, nkv=nkv, causal=causal, klo=klo, khi=khi),
        grid=(b, l // tq),
        in_specs=[_cols_spec(512, tq),
                  pl.BlockSpec((1, n, 512), lambda bb, j: (bb, 0, 0), pipeline_mode=pl.Buffered(1)),
                  pl.BlockSpec((1, nkv, 512, tk), lambda bb, j: (bb, 0, 0, 0), pipeline_mode=pl.Buffered(1)),
                  _const_spec(gout_col)],
        out_specs=_cols_spec(GROUP_W, tq),
        out_shape=jax.ShapeDtypeStruct((b, GROUP_W, l), F32),
        scratch_shapes=[pltpu.VMEM((N_HEADS * SUBLANE, tq), F32), pltpu.VMEM((N_HEADS * LANE, tq), F32),
                        pltpu.VMEM((N_HEADS * tk, tq), F32), pltpu.VMEM((N_HEADS * tk, tq), F32)],
        compiler_params=_params(), name="mla_flash",
    )(qt, k, vt, gout_col)


def _stack_consts(c):
    n = N_HEADS * c
    lc = int(math.log2(c))
    r, l = np.arange(n)[:, None], np.arange(GROUP_W)[None, :]
    hm = ((r >> lc) == (l >> 6)).astype(np.float32)
    rr, cc = np.arange(n)[:, None], np.arange(n)[None, :]
    same = (rr >> lc) == (cc >> lc)
    tri = [same & (cc <= rr), same & (cc < rr)]
    join = [((rr >> (lb + 1)) == (cc >> (lb + 1))) & (((rr >> lb) & 1) == 1) & (((cc >> lb) & 1) == 0)
            for lb in range(lc)] + [rr == cc]
    tl = np.arange(c)[None, :] <= np.arange(c)[:, None]
    hh = np.arange(GROUP_W)
    bd = (hh[:, None] >> 6) == (hh[None, :] >> 6)
    return (jnp.asarray(hm), jnp.asarray(np.stack(tri).astype(np.float32)),
            jnp.asarray(np.stack(join), BF16), jnp.asarray(tl, BF16), jnp.asarray(bd, BF16))


def _scan_call(kernel, name, x, consts_a, hist, s0, consts_b, width_in, tm, c, front, grp):
    b, l, _ = x.shape
    assert b % grp == 0
    consts_b = tuple(consts_b) + _stack_consts(c)
    rows_spec = lambda rows, width: pl.BlockSpec((grp, rows, width), lambda bb, j: (bb, 0, 0))
    return pl.pallas_call(
        functools.partial(kernel, tm=tm, c=c, front=front, grp=grp),
        grid=(b // grp, l // tm),
        in_specs=[pl.BlockSpec((grp, tm, D_MODEL), lambda bb, j: (bb, j, 0))]
        + [_const_spec(a) for a in consts_a]
        + [rows_spec(SUBLANE, width_in), rows_spec(GROUP_W, GROUP_W)]
        + [_const_spec(a) for a in consts_b],
        out_specs=[pl.BlockSpec((grp, tm, GROUP_W), lambda bb, j: (bb, j, 0)),
                   rows_spec(GROUP_W, GROUP_W), rows_spec(SUBLANE, width_in)],
        out_shape=[jax.ShapeDtypeStruct((b, l, GROUP_W), F32),
                   jax.ShapeDtypeStruct((b, GROUP_W, GROUP_W), F32),
                   jax.ShapeDtypeStruct((b, SUBLANE, width_in), F32)],
        scratch_shapes=[pltpu.VMEM((grp, tm + SUBLANE, width_in), F32),
                        pltpu.VMEM((grp, GROUP_W, GROUP_W), F32)],
        compiler_params=_params(), name=name,
    )(x, *consts_a, hist, s0, *consts_b)


def _ffn(x, ya, yb, yc, yd, wo, g2, wup, cw, wdn, hist, tm, front):
    b, l, _ = x.shape
    return pl.pallas_call(
        functools.partial(_ffn_kernel, tm=tm, front=front),
        grid=(b, l // tm),
        in_specs=[_tile_spec(tm, D_MODEL), _cols_spec(GROUP_W, tm)] + [_tile_spec(tm, GROUP_W)] * 3
        + [_const_spec(a) for a in (wo, g2, wup, cw, wdn)] + [_batch_spec(SUBLANE, 2 * D_FF)],
        out_specs=[_tile_spec(tm, D_MODEL), _batch_spec(SUBLANE, 2 * D_FF)],
        out_shape=[jax.ShapeDtypeStruct((b, l, D_MODEL), F32),
                   jax.ShapeDtypeStruct((b, SUBLANE, 2 * D_FF), F32)],
        scratch_shapes=[pltpu.VMEM((SUBLANE, 2 * D_FF), F32),
                        pltpu.VMEM((tm + SUBLANE, 2 * FFN_CW), F32),
                        pltpu.VMEM((tm + SUBLANE, 2 * FFN_CW), F32)],
        compiler_params=_params(), name="out_ffn",
    )(x, ya, yb, yc, yd, wo, g2, wup, cw, wdn, hist)


def _final_norm(x, g, tm, skip_tiles, out_rows):
    b = x.shape[0]
    return pl.pallas_call(
        _final_norm_kernel,
        grid=(b, out_rows // tm),
        in_specs=[pl.BlockSpec((1, tm, D_MODEL), lambda bb, j: (bb, j + skip_tiles, 0)), _const_spec(g)],
        out_specs=_tile_spec(tm, D_MODEL),
        out_shape=jax.ShapeDtypeStruct((b, out_rows, D_MODEL), F32),
        compiler_params=_params(), name="final_norm",
    )(x, g)


def _np_idx():
    z = IN_COLS
    zpad = lambda n: [z] * n
    rep = lambda base: [base + i for i in range(N_HEADS) for _ in range(HEAD_W)]
    grp = lambda base: [base + g * HEAD_W + i for g in (0, 0, 1, 1) for i in range(HEAD_W)]
    a = (list(range(0, 192)) + zpad(64) + list(range(192, 320))
         + list(range(320, 352)) + zpad(96)
         + list(range(336, 352)) + list(range(320, 336)) + zpad(96))
    b0 = A_COLS
    bcols = list(range(b0, b0 + B_COLS))
    c0 = b0 + B_COLS
    ccols = (list(range(c0, c0 + 256)) + list(range(c0 + 256, c0 + 512))
             + grp(c0 + 512) + grp(c0 + 640) + rep(c0 + 768))
    d0 = c0 + C_COLS
    dcols = list(range(d0, d0 + 1024)) + rep(d0 + 1024) + rep(d0 + 1028)
    xbc_exp = list(range(256)) + grp(256) + grp(384)
    xbc_back = (list(range(256)) + list(range(256, 320)) + list(range(384, 448))
                + list(range(512, 576)) + list(range(640, 704)))
    ffn_perm = []
    for f in range(D_FF // FFN_CW):
        ffn_perm += list(range(f * FFN_CW, (f + 1) * FFN_CW))
        ffn_perm += list(range(D_FF + f * FFN_CW, D_FF + (f + 1) * FFN_CW))
    ffn_back = np.argsort(np.array(ffn_perm))
    as_i = lambda v: np.asarray(v, np.int32)
    return dict(a=as_i(a), b=as_i(bcols), c=as_i(ccols), d=as_i(dcols), xbc_exp=as_i(xbc_exp),
                xbc_back=as_i(xbc_back), ffn_perm=as_i(ffn_perm), ffn_back=as_i(ffn_back))


_IDX = _np_idx()


def _rep_heads(v):
    return jnp.repeat(v, HEAD_W, axis=-1)


def _pad_rows(w, rows=SUBLANE):
    return jnp.pad(w, [(0, rows - w.shape[0])] + [(0, 0)] * (w.ndim - 1))


def _layer_consts(P, l):
    row = lambda v: v.reshape(1, -1).astype(F32)
    w_in = jnp.concatenate([P['w_in'][l], jnp.zeros((D_MODEL, 1), F32)], axis=1)
    c = {}
    c['g1'] = row(P['norm1_g'][l])
    c['wa'] = w_in[:, _IDX['a']].astype(BF16)
    c['wb'] = w_in[:, _IDX['b']].astype(BF16)
    c['wc'] = w_in[:, _IDX['c']].astype(BF16)
    c['wd'] = w_in[:, _IDX['d']].astype(BF16)
    c['gq'] = row(jnp.pad(P['a_gq'][l], (0, 64)))
    wuq = P['a_wuq'][l].reshape(A_QRANK, N_HEADS, A_NOPE + A_ROPE)
    rope = wuq[:, :, A_NOPE:]
    swap = jnp.concatenate([rope[..., 16:], rope[..., :16]], axis=-1)
    zeros = lambda n: jnp.zeros((A_QRANK, N_HEADS, n), F32)
    wq1 = jnp.concatenate([wuq, zeros(32)], axis=-1).reshape(A_QRANK, 512)
    wq2 = jnp.concatenate([zeros(64), swap, zeros(32)], axis=-1).reshape(A_QRANK, 512)
    c['wq1t'] = jnp.pad(wq1, ((0, 64), (0, 0))).T.astype(BF16)
    c['wq2t'] = jnp.pad(wq2, ((0, 64), (0, 0))).T.astype(BF16)
    c['gkv'] = row(P['a_gkv'][l])
    wuk = P['a_wuk'][l].reshape(A_KVRANK, N_HEADS, A_NOPE)
    c['wk'] = jnp.concatenate([wuk, jnp.zeros((A_KVRANK, N_HEADS, 64), F32)], axis=-1
                              ).reshape(A_KVRANK, 512).astype(BF16)
    ek = np.zeros((A_ROPE, N_HEADS, LANE), np.float32)
    for hh in range(N_HEADS):
        ek[np.arange(A_ROPE), hh, A_NOPE + np.arange(A_ROPE)] = 1.0
    c['ek'] = jnp.asarray(ek.reshape(A_ROPE, 512), BF16)
    wuv = P['a_wuv'][l].reshape(A_KVRANK, N_HEADS, HEAD_W)
    c['wvt'] = jnp.concatenate([wuv, jnp.zeros((A_KVRANK, N_HEADS, HEAD_W), F32)], axis=-1
                               ).reshape(A_KVRANK, 512).T.astype(BF16)
    ones = np.zeros((N_HEADS, LANE, 1), np.float32)
    ones[:, HEAD_W, 0] = 1.0
    c['ones_col'] = jnp.asarray(ones.reshape(512, 1))
    c['gout_col'] = P['a_gout'][l].reshape(GROUP_W, 1).astype(F32)
    c['mu'] = row(P['b_mu'][l])
    c['w0'] = row(P['b_w0'][l])
    z64 = jnp.zeros((64, GROUP_W), F32)
    c['wl'] = jnp.concatenate([jnp.concatenate([P['b_w2'][l], z64], axis=1),
                               jnp.concatenate([z64, P['b_a2'][l]], axis=1)], axis=0).astype(BF16)
    c['a0'] = row(P['b_a0'][l])
    c['g2b'] = P['b_g2'][l].astype(BF16)
    c['kk'] = row(P['b_kk'][l])
    c['ka'] = row(P['b_ka'][l])
    c['rk'] = row(P['b_rk'][l])
    c['gnw'] = row(P['b_gnw'][l])
    c['gnb'] = row(P['b_gnb'][l])
    c['c_cw'] = _pad_rows(P['c_convw'][l][:, _IDX['xbc_exp']])
    c['c_cb'] = row(P['c_convb'][l][_IDX['xbc_exp']])
    c['c_dtb'] = row(_rep_heads(P['c_dtb'][l]))
    c['c_alog'] = row(_rep_heads(P['c_alog'][l]))
    c['c_d'] = row(_rep_heads(P['c_d'][l]))
    c['c_gn'] = row(P['c_gnorm'][l])
    c['d_cw'] = _pad_rows(P['d_convw'][l])
    c['d_alog'] = row(_rep_heads(P['d_alog'][l]))
    c['d_dtb'] = row(_rep_heads(P['d_dtb'][l]))
    c['d_gn'] = row(jnp.tile(P['d_gnorm'][l], N_HEADS))
    c['wo'] = P['w_out'][l].astype(BF16)
    c['g2'] = row(P['norm2_g'][l])
    c['wup'] = P['f_wup'][l][:, _IDX['ffn_perm']].astype(BF16)
    c['f_cw'] = _pad_rows(P['f_convw'][l][:, _IDX['ffn_perm']])
    c['wdn'] = P['f_wdown'][l].astype(BF16)
    return c


def _embed_bd(s):
    b = s.shape[0]
    eye = jnp.eye(N_HEADS, dtype=s.dtype)
    return jnp.einsum('bhij,hg->bhigj', s, eye).reshape(b, GROUP_W, GROUP_W)


def _extract_bd(s):
    b = s.shape[0]
    s5 = s.reshape(b, N_HEADS, HEAD_W, N_HEADS, HEAD_W)
    return jnp.stack([s5[:, hh, :, hh, :] for hh in range(N_HEADS)], axis=1)


def _hist8(hist):
    return jnp.pad(hist, ((0, 0), (SUBLANE - hist.shape[1], 0), (0, 0)))


def _rope_table(pos):
    half = A_ROPE // 2
    inv = jnp.power(ROPE_BASE, -jnp.arange(half, dtype=F32) / half)
    ang = pos.astype(F32)[:, None] * inv
    cos, sin = jnp.cos(ang), jnp.sin(ang)
    cos2 = jnp.concatenate([cos, cos], axis=-1)
    sin2 = jnp.concatenate([-sin, sin], axis=-1)
    n = pos.shape[0]
    one = jnp.ones((n, A_NOPE), F32)
    z = lambda w: jnp.zeros((n, w), F32)
    tabq = jnp.concatenate([one, cos2, z(32), z(64), sin2, z(32)], axis=-1).T
    tabk = jnp.concatenate([cos2, z(96), sin2, z(96)], axis=-1)
    return tabq, tabk


def _trunk(x, pos, front, st, P, *, tm, tm_ffn, c, tq, causal, n_keys_pad, scan_rows):
    b, l, _ = x.shape
    tabq, tabk = _rope_table(pos)
    new = {name: [] for name in ('ckv', 'krope', 'rwkv_S', 'rwkv_shift', 'ssd_S', 'ssd_conv',
                                 'gdn_S', 'gdn_conv', 'ffn_conv')}
    zeros_bd = jnp.zeros((b, GROUP_W, GROUP_W), F32)
    for li in range(DEPTH):
        c_ = P[li]
        qt, ckv, krope = _mla_prep(x, c_['g1'], c_['wa'], c_['gq'], c_['wq1t'], c_['wq2t'], c_['gkv'],
                                   tabq, tabk, tm_ffn, front)
        if st is None:
            c_all, kr_all, klo, khi = ckv, krope, front, l
            s_b = s_c = s_d = zeros_bd
            shift8 = jnp.zeros((b, SUBLANE, B_COLS), F32)
            chist = jnp.zeros((b, SUBLANE, 768), F32)
            dhist = jnp.zeros((b, SUBLANE, 768), F32)
            fhist = jnp.zeros((b, SUBLANE, 2 * D_FF), F32)
        else:
            past = st['ckv'].shape[2]
            padk = n_keys_pad - past - l
            c_all = jnp.concatenate([st['ckv'][li], ckv, jnp.zeros((b, padk, A_KVRANK), F32)], axis=1)
            kr_all = jnp.concatenate([st['krope'][li], krope, jnp.zeros((b, padk, A_ROPE), F32)], axis=1)
            klo, khi = 0, past + l
            s_b = _embed_bd(st['rwkv_S'][li])
            s_c = _embed_bd(jnp.swapaxes(st['ssd_S'][li], -1, -2))
            s_d = _embed_bd(st['gdn_S'][li])
            shift8 = _hist8(st['rwkv_shift'][li][:, None, :])
            chist = _hist8(st['ssd_conv'][li][:, :, _IDX['xbc_exp']])
            dhist = _hist8(st['gdn_conv'][li])
            fhist = _hist8(st['ffn_conv'][li][:, :, _IDX['ffn_perm']])
        nk = c_all.shape[1]
        tkv = tq if causal else nk
        kf, vt = _kv_up(c_all, kr_all, c_['wk'], c_['ek'], c_['wvt'], c_['ones_col'], tkv)
        ya = _flash(qt, kf, vt, c_['gout_col'], tq, tkv, causal, klo, khi)
        yb, sb_new, shift_new = _scan_call(
            _rwkv_kernel, "rwkv7", x, (c_['g1'], c_['wb']), shift8, s_b,
            (c_['mu'], c_['w0'], c_['wl'], c_['a0'], c_['g2b'], c_['kk'], c_['ka'], c_['rk'],
             c_['gnw'], c_['gnb']), B_COLS, tm, c, front, scan_rows)
        yc, sc_new, chist_new = _scan_call(
            _ssd_kernel, "ssd", x, (c_['g1'], c_['wc']), chist, s_c,
            (c_['c_cw'], c_['c_cb'], c_['c_dtb'], c_['c_alog'], c_['c_d'], c_['c_gn']),
            768, tm, c, front, scan_rows)
        yd, sd_new, dhist_new = _scan_call(
            _gdn_kernel, "gdn", x, (c_['g1'], c_['wd']), dhist, s_d,
            (c_['d_cw'], c_['d_alog'], c_['d_dtb'], c_['d_gn']), 768, tm, c, front, scan_rows)
        x, fhist_new = _ffn(x, ya, yb, yc, yd, c_['wo'], c_['g2'], c_['wup'], c_['f_cw'], c_['wdn'],
                            fhist, tm_ffn, front)
        new['ckv'].append(ckv[:, front:])
        new['krope'].append(krope[:, front:])
        new['rwkv_S'].append(_extract_bd(sb_new))
        new['rwkv_shift'].append(shift_new[:, SUBLANE - 1])
        new['ssd_S'].append(jnp.swapaxes(_extract_bd(sc_new), -1, -2))
        new['ssd_conv'].append(chist_new[:, SUBLANE - (C_CONV - 1):][:, :, _IDX['xbc_back']])
        new['gdn_S'].append(_extract_bd(sd_new))
        new['gdn_conv'].append(dhist_new[:, SUBLANE - (D_CONV - 1):])
        new['ffn_conv'].append(fhist_new[:, SUBLANE - (FFN_CONV - 1):][:, :, _IDX['ffn_back']])
    return x, {name: jnp.stack(vals) for name, vals in new.items()}


def kernel(x_prompt, x_sample, cache_mla_ckv, cache_mla_krope, state_rwkv, state_rwkv_shift, state_ssd, state_ssd_conv, state_gdn, state_gdn_conv, state_ffn_conv, meta_tokens, norm1_g, w_in, a_gq, a_wuq, a_gkv, a_wuk, a_wuv, a_gout, b_mu, b_w0, b_w2, b_a0, b_a2, b_g2, b_kk, b_ka, b_rk, b_gnw, b_gnb, c_convw, c_convb, c_dtb, c_alog, c_d, c_gnorm, d_convw, d_alog, d_dtb, d_gnorm, w_out, norm2_g, f_wup, f_convw, f_wdown, final_g):
    P = dict(norm1_g=norm1_g, w_in=w_in, a_gq=a_gq, a_wuq=a_wuq, a_gkv=a_gkv, a_wuk=a_wuk,
             a_wuv=a_wuv, a_gout=a_gout, b_mu=b_mu, b_w0=b_w0, b_w2=b_w2, b_a0=b_a0, b_a2=b_a2,
             b_g2=b_g2, b_kk=b_kk, b_ka=b_ka, b_rk=b_rk, b_gnw=b_gnw, b_gnb=b_gnb,
             c_convw=c_convw, c_convb=c_convb, c_dtb=c_dtb, c_alog=c_alog, c_d=c_d, c_gnorm=c_gnorm,
             d_convw=d_convw, d_alog=d_alog, d_dtb=d_dtb, d_gnorm=d_gnorm,
             w_out=w_out, norm2_g=norm2_g, f_wup=f_wup, f_convw=f_convw, f_wdown=f_wdown)
    P = [_layer_consts(P, li) for li in range(DEPTH)]
    fin = final_g.reshape(1, -1).astype(F32)
    b_p, seq, _ = x_prompt.shape
    lx = N_META + seq
    assert seq % ATT_BLOCK == 0 and N_META <= ATT_BLOCK
    front = ATT_BLOCK - N_META
    l_pad = front + lx
    meta = jnp.broadcast_to(meta_tokens[None].astype(F32), (b_p, N_META, D_MODEL))
    x_ext = jnp.concatenate([jnp.zeros((b_p, front, D_MODEL), F32), meta, x_prompt], axis=1)
    pos_p = jnp.arange(l_pad, dtype=jnp.int32) - (front + N_META)
    tm_ffn = max(t for t in (ROW_TILE, 2 * ROW_TILE, 3 * ROW_TILE) if l_pad % t == 0)
    y_p, ns_p = _trunk(x_ext, pos_p, front, None, P, tm=ROW_TILE, tm_ffn=tm_ffn, c=CHUNK, tq=ATT_BLOCK,
                       causal=True, n_keys_pad=l_pad, scan_rows=SCAN_ROWS)
    y_prompt = _final_norm(y_p, fin, ROW_TILE, (front + N_META) // ROW_TILE, seq)
    b_s, t_s, _ = x_sample.shape
    past = cache_mla_ckv.shape[2]
    assert t_s <= CHUNK and t_s % 16 == 0 and (t_s & (t_s - 1)) == 0
    st_s = dict(ckv=cache_mla_ckv, krope=cache_mla_krope, rwkv_S=state_rwkv, rwkv_shift=state_rwkv_shift,
                ssd_S=state_ssd, ssd_conv=state_ssd_conv, gdn_S=state_gdn, gdn_conv=state_gdn_conv,
                ffn_conv=state_ffn_conv)
    pos_s = past + jnp.arange(t_s, dtype=jnp.int32)
    n_keys_pad = -(-(past + t_s) // LANE) * LANE
    y_s, ns_s = _trunk(x_sample, pos_s, 0, st_s, P, tm=t_s, tm_ffn=t_s, c=t_s, tq=t_s, causal=False,
                       n_keys_pad=n_keys_pad, scan_rows=math.gcd(b_s, SCAN_ROWS_SHORT))
    y_sample = _final_norm(y_s, fin, t_s, 0, t_s)
    keys = ('ckv', 'krope', 'rwkv_S', 'rwkv_shift', 'ssd_S', 'ssd_conv', 'gdn_S', 'gdn_conv', 'ffn_conv')
    return (y_prompt, y_sample) + tuple(ns_p[k] for k in keys) + tuple(ns_s[k] for k in keys)
```

```python
import functools
import math

import numpy as np
import jax
import jax.numpy as jnp
from jax import lax
from jax.experimental import pallas as pl
from jax.experimental.pallas import tpu as pltpu

F32 = jnp.float32
BF16 = jnp.bfloat16

D_MODEL = 1024
DEPTH = 4
CHUNK = 64
N_META = 16
EPS = 1e-6
L2_EPS = 1e-6
GROUP_W = 256
N_HEADS = 4
HEAD_W = 64
A_NOPE = 64
A_ROPE = 32
A_QRANK = 192
A_KVRANK = 128
A_SCALE = (A_NOPE + A_ROPE) ** -0.5
ROPE_BASE = 10000.0
B_GN_EPS = 64e-5
B_COLS = 1024
C_CONV = 4
D_CONV = 4
D_FF = 2816
FFN_CONV = 3
A_COLS = 352
C_COLS = 772
D_COLS = 1032
IN_COLS = A_COLS + B_COLS + C_COLS + D_COLS

LANE = 128
SUBLANE = 8
VMEM_LIMIT = 56 * 1024 * 1024
NEG = -1e30
LOG2E = math.log2(math.e)
V_ROWS = 80
FFN_CW = 256
FFN_DOWN_GROUP = 4
ATT_BLOCK = 256
ROW_TILE = 256
SCAN_ROWS = 2
SCAN_ROWS_SHORT = 8


def _mm(a, b):
    return jnp.dot(a.astype(BF16), b.astype(BF16), preferred_element_type=F32)


def _mm_nt(a, b):
    return lax.dot_general(a.astype(BF16), b.astype(BF16), (((1,), (1,)), ((), ())),
                           preferred_element_type=F32)


def _mm_tn(a, b):
    return lax.dot_general(a.astype(BF16), b.astype(BF16), (((0,), (0,)), ((), ())),
                           preferred_element_type=F32)


def _split2(x):
    hi = x.astype(BF16)
    lo = (x - hi.astype(F32)).astype(BF16)
    return hi, lo


def _mm_x2(x, w):
    hi, lo = _split2(x)
    return (jnp.dot(hi, w, preferred_element_type=F32)
            + jnp.dot(lo, w, preferred_element_type=F32))


def _rms(x, g, n):
    ms = jnp.sum(x * x, axis=-1, keepdims=True) * (1.0 / n)
    return x * lax.rsqrt(ms + EPS) * g


def _softplus(x):
    return jnp.maximum(x, 0.0) + jnp.log1p(jnp.exp(-jnp.abs(x)))


def _sigmoid(x):
    return 1.0 / (1.0 + jnp.exp(-x))


def _silu(x):
    return x * _sigmoid(x)


def _iota(shape, axis):
    return lax.broadcasted_iota(jnp.int32, shape, axis)


def _norm_in(x_ref, g_ref, j, tm, front):
    h = _rms(x_ref[0], g_ref[...], D_MODEL)
    if front > 0:
        rows = j * tm + _iota((tm, 1), 0)
        h = jnp.where(rows >= front, h, 0.0)
    return h.astype(BF16)


def _norm_in_rows(x_ref, g_ref, j, tm, front):
    x = x_ref[...]
    h = _rms(x, g_ref[...], D_MODEL)
    if front > 0:
        pos = j * tm + _iota((1, tm, 1), 1)
        h = jnp.where(pos >= front, h, 0.0)
    return h.reshape(x.shape[0] * tm, D_MODEL).astype(BF16)


def _hsum(x, bd):
    return _mm_x2(x, bd)


TRI_INCL, TRI_STRICT = 0, 1


class _Stk:
    def __init__(self, c, hm_ref, tri_ref, join_ref, tl_ref):
        self.c = c
        self.n = N_HEADS * c
        self.steps = int(math.log2(c))
        assert 1 << self.steps == c
        self.hm_ref, self.tri_ref, self.join_ref, self.tl_ref = hm_ref, tri_ref, join_ref, tl_ref
        self.hm_b = None

    def mask(self, which):
        return self.tri_ref[which]

    def tile(self, x):
        return jnp.concatenate([x] * N_HEADS, axis=0)

    def stack(self, x):
        return self.tile(x) * self.hm_ref[...]

    def unstack(self, xs):
        c = self.c
        return xs[0:c] + xs[c:2 * c] + xs[2 * c:3 * c] + xs[3 * c:4 * c]

    def cumsum(self, x):
        hi, lo = _split2(x)
        tl = self.tl_ref[...]
        return jnp.dot(tl, hi, preferred_element_type=F32) + jnp.dot(tl, lo, preferred_element_type=F32)

    def decay(self, g_cum):
        gcol = jnp.min(self.stack(g_cum), axis=-1, keepdims=True)
        grow = jnp.broadcast_to(gcol, (self.n, LANE)).T[0:1, :]
        return jnp.exp(jnp.minimum(gcol - grow, 0.0)) * self.mask(TRI_INCL)

    def tri_inv_many(self, lowers):
        lowers = [lw.astype(BF16) for lw in lowers]
        minvs = [self.join_ref[self.steps] - lw * self.join_ref[0] for lw in lowers]
        for lb in range(1, self.steps):
            ts = [_mm(m, lw * self.join_ref[lb]) for m, lw in zip(minvs, lowers)]
            minvs = [m - _mm(t, m).astype(BF16) for t, m in zip(ts, minvs)]
        return minvs

    def stack_b(self, x):
        if self.hm_b is None:
            self.hm_b = self.hm_ref[...].astype(BF16)
        return self.tile(x.astype(BF16)) * self.hm_b


def _mla_prep_kernel(x_ref, g1_ref, wa_ref, gq_ref, wq1t_ref, wq2t_ref, gkv_ref, tabq_ref, tabk_ref,
                     *rest, tm, front, tk):
    if tk:
        wk_ref, ek_ref, wvt_ref, ones_ref, qt_out, ckv_out, kr_out, k_out, vt_out = rest
    else:
        qt_out, ckv_out, kr_out = rest
    j = pl.program_id(1)
    h = _norm_in(x_ref, g1_ref, j, tm, front)
    pa = jnp.dot(h, wa_ref[...], preferred_element_type=F32)
    qn = _rms(pa[:, 0:256], gq_ref[...], A_QRANK)
    q1t = _mm_nt(wq1t_ref[...], qn)
    q2t = _mm_nt(wq2t_ref[...], qn)
    tabq = tabq_ref[...]
    cos4 = jnp.concatenate([tabq[0:128]] * N_HEADS, axis=0)
    sin4 = jnp.concatenate([tabq[128:256]] * N_HEADS, axis=0)
    qt_out[0] = ((q1t * cos4 + q2t * sin4) * (A_SCALE * LOG2E)).astype(BF16)
    c = _rms(pa[:, 256:384], gkv_ref[...], A_KVRANK)
    ckv_out[0] = c
    tabk = tabk_ref[...]
    kr = pa[:, 384:512] * tabk[:, 0:128] + pa[:, 512:640] * tabk[:, 128:256]
    kr_out[0] = kr[:, 0:A_ROPE]
    if tk:
        cb = c.astype(BF16)
        k_out[0] = (jnp.dot(cb, wk_ref[...], preferred_element_type=F32)
                    + jnp.dot(kr.astype(BF16), ek_ref[...], preferred_element_type=F32)).astype(BF16)
        vt = (_mm_nt(wvt_ref[...], cb) + ones_ref[...]).astype(BF16)
        for t in range(tm // tk):
            vt_out[0, t] = vt[:, t * tk:(t + 1) * tk]


def _kv_up_kernel(c_ref, kr_ref, wk_ref, ek_ref, wvt_ref, ones_ref, k_out, vt_out):
    c = c_ref[0].astype(BF16)
    kr = kr_ref[0].astype(BF16)
    k = (jnp.dot(c, wk_ref[...], preferred_element_type=F32)
         + jnp.dot(kr, ek_ref[...], preferred_element_type=F32))
    k_out[0] = k.astype(BF16)
    vt_out[0, 0] = (_mm_nt(wvt_ref[...], c) + ones_ref[...]).astype(BF16)


def _flash_kernel(qt_ref, k_ref, vt_ref, gout_ref, o_ref, m_sc, acc_sc, sa_sc, sb_sc,
                  *, tq, tk, nkv, causal, klo, khi):
    i = pl.program_id(1)
    m_sc[...] = jnp.full((N_HEADS * SUBLANE, tq), NEG, F32)
    acc_sc[...] = jnp.zeros((N_HEADS * V_ROWS, tq), F32)
    vrows = [slice(V_ROWS * h, V_ROWS * (h + 1)) for h in range(N_HEADS)]
    heads = [slice(LANE * h, LANE * (h + 1)) for h in range(N_HEADS)]
    s_bufs = (sa_sc, sb_sc)

    def produce(jb, slot):
        start = pl.multiple_of(jb * tk, tk)
        for h, rows in enumerate(heads):
            s_bufs[slot][h * tk:(h + 1) * tk, :] = jnp.dot(
                k_ref[0, pl.ds(start, tk), rows], qt_ref[0, rows, :], preferred_element_type=F32)

    def consume(jb, slot, masked, nxt):
        if nxt is not None:
            produce(*nxt)
        if masked:
            kpos = jb * tk + _iota((tk, 1), 0)
            qpos = i * tq + _iota((1, tq), 1)
            vis = (kpos >= klo) & (kpos < khi)
            if causal:
                vis = vis & ((kpos >> 6) <= (qpos >> 6))
        m_all = m_sc[...]
        acc_all = acc_sc[...]
        m_out, alphas, ps = [], [], []
        for h in range(N_HEADS):
            s = s_bufs[slot][h * tk:(h + 1) * tk, :]
            if masked:
                s = jnp.where(vis, s, NEG)
            m_prev = m_all[SUBLANE * h:SUBLANE * (h + 1)]
            m_new = jnp.maximum(m_prev, jnp.max(s, axis=0, keepdims=True))
            alphas.append(jnp.exp2(m_prev[0:1] - m_new[0:1]))
            ps.append(jnp.exp2(s - m_new[0:1]).astype(BF16))
            m_out.append(m_new)
        acc_out = [alphas[h] * acc_all[rows]
                   + jnp.dot(vt_ref[0, jb, rows, :], ps[h], preferred_element_type=F32)
                   for h, rows in enumerate(vrows)]
        m_sc[...] = jnp.concatenate(m_out, axis=0)
        acc_sc[...] = jnp.concatenate(acc_out, axis=0)

    produce(0, 0)
    if causal:
        @pl.when(i == 0)
        def _():
            consume(0, 0, True, None)

        @pl.when(i > 0)
        def _():
            consume(0, 0, True, (1, 1))
            quads = (i - 1) >> 2

            def pair(jb):
                consume(jb, 1, False, (jb + 1, 0))
                consume(jb + 1, 0, False, (jb + 2, 1))

            def body(t, carry):
                pair(1 + 4 * t)
                pair(3 + 4 * t)
                return carry

            lax.fori_loop(0, quads, body, 0)

            @pl.when(((i - 1) & 2) != 0)
            def _():
                pair(1 + 4 * quads)

            @pl.when((i & 1) == 0)
            def _():
                consume(i - 1, 1, False, (i, 0))
                consume(i, 0, True, None)

            @pl.when((i & 1) == 1)
            def _():
                consume(i, 1, True, None)
    else:
        for jb in range(nkv):
            consume(jb, jb & 1, True, (jb + 1, (jb + 1) & 1) if jb + 1 < nkv else None)
    outs = []
    for h in range(N_HEADS):
        a = acc_sc[V_ROWS * h:V_ROWS * (h + 1), :]
        outs.append(a[0:HEAD_W] / a[HEAD_W:HEAD_W + 1])
    yat = jnp.concatenate(outs, axis=0)
    ms = jnp.sum(yat * yat, axis=0, keepdims=True) * (1.0 / GROUP_W)
    o_ref[0] = yat * lax.rsqrt(ms + EPS) * gout_ref[...]


def _chunk_ids(grp, tm, c):
    ids = [(g, ci) for ci in range(tm // c) for g in range(grp)]
    return ids, {(g, ci): slice(g * tm + ci * c, g * tm + (ci + 1) * c) for g, ci in ids}


def _carry_rows(work, g, new_rows, tm):
    work[g, SUBLANE:SUBLANE + tm, :] = new_rows
    return work[g, tm:tm + SUBLANE, :]


def _rwkv_kernel(x_ref, g1_ref, wb_ref, shift_ref, s0_ref, mu_ref, w0_ref, wl_ref, a0_ref,
                 g2_ref, kk_ref, ka_ref, rk_ref, gnw_ref, gnb_ref, hm_ref, tri_ref, join_ref, tl_ref, bd_ref,
                 y_out, s_out, shift_out, work, st, *, tm, c, front, grp):
    j = pl.program_id(1)

    @pl.when(j == 0)
    def _():
        work[:, 0:SUBLANE, :] = shift_ref[...]
        st[...] = s0_ref[...]

    h = _norm_in_rows(x_ref, g1_ref, j, tm, front)
    cols = jnp.dot(h, wb_ref[...], preferred_element_type=F32)
    shifted = []
    for g in range(grp):
        tail = _carry_rows(work, g, cols[g * tm:(g + 1) * tm], tm)
        shifted.append(work[g, pl.ds(SUBLANE - 1, tm), :])
        work[g, 0:SUBLANE, :] = tail
        shift_out[g] = tail
    shifted = jnp.concatenate(shifted, axis=0)
    xm = cols + (shifted - cols) * mu_ref[...]
    r = xm[:, 0:256]
    k = xm[:, 256:512]
    v = xm[:, 512:768]
    lora = xm[:, 768:896]
    dg = xm[:, 896:1024]
    lora = jnp.where(_iota((grp * tm, LANE), 1) < 64, jnp.tanh(lora), lora)
    ll = _mm(lora, wl_ref[...])
    w_log = -_softplus(-(w0_ref[...] + ll[:, 0:256])) - 0.5
    logd = -jnp.exp(w_log)
    a = _sigmoid(a0_ref[...] + ll[:, 256:512])
    g_gate = _mm(_sigmoid(dg), g2_ref[...])
    bd = bd_ref[...]
    kkr = k * kk_ref[...]
    kk = kkr * lax.rsqrt(_hsum(kkr * kkr, bd) + L2_EPS)
    k2 = k * (1.0 + (a - 1.0) * ka_ref[...])

    sk = _Stk(c, hm_ref, tri_ref, join_ref, tl_ref)
    ids, rows = _chunk_ids(grp, tm, c)
    strict, incl = sk.mask(TRI_STRICT), sk.mask(TRI_INCL)
    gcs = {i: sk.cumsum(logd[rows[i]]) for i in ids}
    pre = {}
    for i in ids:
        gc, ld = gcs[i], logd[rows[i]]
        eg, eng = jnp.exp(gc), jnp.exp(-gc)
        bt = kk[rows[i]] * a[rows[i]] * eng
        kt = k2[rows[i]] * eng
        dc = eg[c - 1:c, :]
        pre[i] = dict(at_s=sk.stack_b(-kk[rows[i]] * jnp.exp(gc - ld)), rt_s=sk.stack_b(r[rows[i]] * eg),
                      v_s=sk.stack_b(v[rows[i]]), bt_t=sk.tile(bt.astype(BF16)),
                      kt_t=sk.tile(kt.astype(BF16)), dc=dc,
                      bd_s=sk.stack_b(bt * dc), kd_s=sk.stack_b(kt * dc))
    lab = {i: _mm_nt(pre[i]['at_s'], pre[i]['bt_t']) * strict for i in ids}
    aak = {i: (_mm_nt(pre[i]['at_s'], pre[i]['kt_t']) * strict).astype(BF16) for i in ids}
    arb = {i: (_mm_nt(pre[i]['rt_s'], pre[i]['bt_t']) * incl).astype(BF16) for i in ids}
    ark = {i: (_mm_nt(pre[i]['rt_s'], pre[i]['kt_t']) * incl).astype(BF16) for i in ids}
    minv = {i: m.astype(BF16) for i, m in zip(ids, sk.tri_inv_many([-lab[i] for i in ids]))}
    a2 = {i: _mm(aak[i], pre[i]['v_s']) for i in ids}
    u0 = {i: _mm(minv[i], a2[i]) for i in ids}
    m1 = {i: _mm(minv[i], pre[i]['at_s']).astype(BF16) for i in ids}
    ork = {i: _mm(ark[i], pre[i]['v_s']) for i in ids}
    skv = {i: _mm_tn(pre[i]['v_s'], pre[i]['kd_s']) for i in ids}

    s = [st[g] for g in range(grp)]
    outs = {}
    for ci in range(tm // c):
        sb = [s[g].astype(BF16) for g in range(grp)]
        us = [u0[(g, ci)] + _mm_nt(m1[(g, ci)], sb[g]) for g in range(grp)]
        oq = [_mm_nt(pre[(g, ci)]['rt_s'], sb[g]) for g in range(grp)]
        for g in range(grp):
            i = (g, ci)
            s[g] = s[g] * pre[i]['dc'] + _mm_tn(us[g], pre[i]['bd_s']) + skv[i]
            outs[i] = sk.unstack(oq[g] + _mm(arb[i], us[g]) + ork[i])
    for g in range(grp):
        st[g] = s[g]
        s_out[g] = s[g]
    o = jnp.concatenate([outs[(g, ci)] for g in range(grp) for ci in range(tm // c)], axis=0)
    mean = _hsum(o, bd) * (1.0 / HEAD_W)
    d = o - mean
    var = _hsum(d * d, bd) * (1.0 / HEAD_W)
    o = d * lax.rsqrt(var + B_GN_EPS) * gnw_ref[...] + gnb_ref[...]
    bonus = _hsum(r * k2 * rk_ref[...], bd) * v
    y_out[...] = ((o + bonus) * g_gate).reshape(grp, tm, GROUP_W)


def _conv4_rows(work, new, wv, hist_out, grp, tm):
    ys = []
    for g in range(grp):
        tail = _carry_rows(work, g, new[g * tm:(g + 1) * tm], tm)
        y = work[g, pl.ds(SUBLANE - 3, tm), :] * wv[0:1, :]
        y = y + work[g, pl.ds(SUBLANE - 2, tm), :] * wv[1:2, :]
        y = y + work[g, pl.ds(SUBLANE - 1, tm), :] * wv[2:3, :]
        ys.append(y + work[g, pl.ds(SUBLANE, tm), :] * wv[3:4, :])
        work[g, 0:SUBLANE, :] = tail
        hist_out[g] = tail
    return jnp.concatenate(ys, axis=0)


def _ssd_kernel(x_ref, g1_ref, wc_ref, hist_ref, s0_ref, cw_ref, cb_ref, dtb_ref, alog_ref,
                dskip_ref, gn_ref, hm_ref, tri_ref, join_ref, tl_ref, bd_ref,
                y_out, s_out, hist_out, work, st, *, tm, c, front, grp):
    j = pl.program_id(1)

    @pl.when(j == 0)
    def _():
        work[:, 0:SUBLANE, :] = hist_ref[...]
        st[...] = s0_ref[...]

    h = _norm_in_rows(x_ref, g1_ref, j, tm, front)
    pc = jnp.dot(h, wc_ref[...], preferred_element_type=F32)
    z = pc[:, 0:256]
    xbc = _silu(_conv4_rows(work, pc[:, 256:1024], cw_ref[...], hist_out, grp, tm) + cb_ref[...])
    xs = xbc[:, 0:256]
    bm = xbc[:, 256:512]
    cm = xbc[:, 512:768]
    dt = _softplus(pc[:, 1024:1280] + dtb_ref[...])
    if front > 0:
        pos = j * tm + _iota((1, tm, 1), 1)
        dt = jnp.where(pos >= front, dt.reshape(grp, tm, GROUP_W), 0.0).reshape(grp * tm, GROUP_W)
    a = dt * (-jnp.exp(alog_ref[...]))
    xdt = xs * dt

    sk = _Stk(c, hm_ref, tri_ref, join_ref, tl_ref)
    ids, rows = _chunk_ids(grp, tm, c)
    acs = {i: sk.cumsum(a[rows[i]]) for i in ids}
    dms = {i: sk.decay(acs[i]) for i in ids}
    xdt_s = {i: sk.stack_b(xdt[rows[i]]) for i in ids}
    amat = {i: _mm_nt(sk.stack_b(cm[rows[i]]), sk.tile(bm[rows[i]].astype(BF16))) * dms[i] for i in ids}
    ydiag = {i: _mm(amat[i], xdt_s[i]) for i in ids}
    sx = {i: _mm_tn(sk.stack_b(bm[rows[i]] * jnp.exp(acs[i][c - 1:c, :] - acs[i])), xdt_s[i]) for i in ids}
    ce_s = {i: sk.stack_b(cm[rows[i]] * jnp.exp(acs[i])) for i in ids}

    s = [st[g] for g in range(grp)]
    outs = {}
    for ci in range(tm // c):
        for g in range(grp):
            i = (g, ci)
            outs[i] = sk.unstack(ydiag[i] + _mm(ce_s[i], s[g]))
            s[g] = s[g] * jnp.exp(acs[i][c - 1:c, :]) + sx[i]
    for g in range(grp):
        st[g] = s[g]
        s_out[g] = s[g]
    y = jnp.concatenate([outs[(g, ci)] for g in range(grp) for ci in range(tm // c)], axis=0)
    y = y + dskip_ref[...] * xs
    y_out[...] = _rms(y * _silu(z), gn_ref[...], GROUP_W).reshape(grp, tm, GROUP_W)


def _gdn_kernel(x_ref, g1_ref, wd_ref, hist_ref, s0_ref, cw_ref, alog_ref, dtb_ref, gn_ref,
                hm_ref, tri_ref, join_ref, tl_ref, bd_ref,
                y_out, s_out, hist_out, work, st, *, tm, c, front, grp):
    j = pl.program_id(1)

    @pl.when(j == 0)
    def _():
        work[:, 0:SUBLANE, :] = hist_ref[...]
        st[...] = s0_ref[...]

    h = _norm_in_rows(x_ref, g1_ref, j, tm, front)
    pd = jnp.dot(h, wd_ref[...], preferred_element_type=F32)
    qkv = _silu(_conv4_rows(work, pd[:, 0:768], cw_ref[...], hist_out, grp, tm))
    z = pd[:, 768:1024]
    beta = _sigmoid(pd[:, 1024:1280])
    g_log = -jnp.exp(alog_ref[...]) * _softplus(pd[:, 1280:1536] + dtb_ref[...])
    bd = bd_ref[...]
    q = qkv[:, 0:256]
    k = qkv[:, 256:512]
    v = qkv[:, 512:768]
    q = q * lax.rsqrt(_hsum(q * q, bd) + L2_EPS) * (HEAD_W ** -0.5)
    k = k * lax.rsqrt(_hsum(k * k, bd) + L2_EPS)

    sk = _Stk(c, hm_ref, tri_ref, join_ref, tl_ref)
    ids, rows = _chunk_ids(grp, tm, c)
    strict = sk.mask(TRI_STRICT)
    gcs = {i: sk.cumsum(g_log[rows[i]]) for i in ids}
    dms = {i: sk.decay(gcs[i]) for i in ids}
    kb = {i: k[rows[i]] * beta[rows[i]] for i in ids}
    k_t = {i: sk.tile(k[rows[i]].astype(BF16)) for i in ids}
    lower = {i: _mm_nt(sk.stack_b(kb[i]), k_t[i]) * dms[i] * strict for i in ids}
    aqk = {i: (_mm_nt(sk.stack_b(q[rows[i]]), k_t[i]) * dms[i]).astype(BF16) for i in ids}
    tinv = {i: t.astype(BF16) for i, t in zip(ids, sk.tri_inv_many([lower[i] for i in ids]))}
    u = {i: _mm(tinv[i], sk.stack_b(v[rows[i]] * beta[rows[i]])) for i in ids}
    w = {i: _mm(tinv[i], sk.stack_b(kb[i] * jnp.exp(gcs[i]))).astype(BF16) for i in ids}
    qe_s = {i: sk.stack_b(q[rows[i]] * jnp.exp(gcs[i])) for i in ids}
    kd_s = {i: sk.stack_b(k[rows[i]] * jnp.exp(gcs[i][c - 1:c, :] - gcs[i])) for i in ids}

    s = [st[g] for g in range(grp)]
    outs = {}
    for ci in range(tm // c):
        sb = [s[g].astype(BF16) for g in range(grp)]
        vn = [u[(g, ci)] - _mm(w[(g, ci)], sb[g]) for g in range(grp)]
        oq = [_mm(qe_s[(g, ci)], sb[g]) for g in range(grp)]
        for g in range(grp):
            i = (g, ci)
            s[g] = s[g] * jnp.exp(gcs[i][c - 1:c, :]) + _mm_tn(kd_s[i], vn[g])
            outs[i] = sk.unstack(oq[g] + _mm(aqk[i], vn[g]))
    for g in range(grp):
        st[g] = s[g]
        s_out[g] = s[g]
    o = jnp.concatenate([outs[(g, ci)] for g in range(grp) for ci in range(tm // c)], axis=0)
    ms = _hsum(o * o, bd) * (1.0 / HEAD_W)
    y_out[...] = (o * lax.rsqrt(ms + EPS) * gn_ref[...] * _silu(z)).reshape(grp, tm, GROUP_W)


def _ffn_kernel(x_ref, ya_ref, yb_ref, yc_ref, yd_ref, wo_ref, g2_ref, wup_ref, cw_ref, wdn_ref,
                hist_ref, xo_ref, hist_out, carry, work_a, work_b, *, tm, front):
    j = pl.program_id(1)

    @pl.when(j == 0)
    def _():
        carry[...] = hist_ref[0]

    x = x_ref[0] + _mm_tn(ya_ref[0], wo_ref[0:GROUP_W, :])
    for idx, y_ref in ((1, yb_ref), (2, yc_ref), (3, yd_ref)):
        x = x + jnp.dot(y_ref[0].astype(BF16), wo_ref[GROUP_W * idx:GROUP_W * (idx + 1), :],
                        preferred_element_type=F32)
    h2 = _rms(x, g2_ref[...], D_MODEL)
    if front > 0:
        rows = j * tm + _iota((tm, 1), 0)
        h2 = jnp.where(rows >= front, h2, 0.0)
    h2 = h2.astype(BF16)
    acc = jnp.zeros((tm, D_MODEL), F32)
    w2 = 2 * FFN_CW
    n_f = D_FF // FFN_CW
    up = lambda f: jnp.dot(h2, wup_ref[:, f * w2:(f + 1) * w2], preferred_element_type=F32)
    u_next = up(0)
    acts, k0 = [], 0
    for f in range(n_f):
        cols = slice(f * w2, (f + 1) * w2)
        u = u_next
        if f + 1 < n_f:
            u_next = up(f + 1)
        work = (work_a, work_b)[f % 2]
        work[0:SUBLANE, :] = carry[:, cols]
        work[SUBLANE:SUBLANE + tm, :] = u
        cw = cw_ref[:, cols]
        y = (work[pl.ds(SUBLANE - 2, tm), :] * cw[0:1, :]
             + work[pl.ds(SUBLANE - 1, tm), :] * cw[1:2, :] + u * cw[2:3, :])
        carry[:, cols] = work[tm:tm + SUBLANE, :]
        acts.append((_silu(y[:, 0:FFN_CW]) * y[:, FFN_CW:w2]).astype(BF16))
        if len(acts) == FFN_DOWN_GROUP or f + 1 == n_f:
            k1 = k0 + FFN_CW * len(acts)
            acc = acc + jnp.dot(jnp.concatenate(acts, axis=1) if len(acts) > 1 else acts[0],
                                wdn_ref[k0:k1, :], preferred_element_type=F32)
            acts, k0 = [], k1
    xo_ref[0] = x + acc
    hist_out[0] = carry[...]


def _final_norm_kernel(x_ref, g_ref, o_ref):
    o_ref[0] = _rms(x_ref[0], g_ref[...], D_MODEL)


def _const_spec(arr):
    nd = arr.ndim
    return pl.BlockSpec(arr.shape, lambda b, j: (0,) * nd, pipeline_mode=pl.Buffered(1))


def _tile_spec(tm, width):
    return pl.BlockSpec((1, tm, width), lambda b, j: (b, j, 0))


def _batch_spec(rows, width):
    return pl.BlockSpec((1, rows, width), lambda b, j: (b, 0, 0))


def _params():
    return pltpu.CompilerParams(dimension_semantics=("arbitrary", "arbitrary"),
                                vmem_limit_bytes=VMEM_LIMIT)


def _cols_spec(rows, tm):
    return pl.BlockSpec((1, rows, tm), lambda b, j: (b, 0, j))


def _mla_prep(x, g1, wa, gq, wq1t, wq2t, gkv, tabq, tabk, tm, front, kv_consts=(), tk=0):
    b, l, _ = x.shape
    consts = (g1, wa, gq, wq1t, wq2t, gkv)
    out_specs = [_cols_spec(512, tm), _tile_spec(tm, A_KVRANK), _tile_spec(tm, A_ROPE)]
    out_shape = [jax.ShapeDtypeStruct((b, 512, l), BF16),
                 jax.ShapeDtypeStruct((b, l, A_KVRANK), F32),
                 jax.ShapeDtypeStruct((b, l, A_ROPE), F32)]
    if tk:
        nb = tm // tk
        out_specs += [_tile_spec(tm, 512),
                      pl.BlockSpec((1, nb, N_HEADS * V_ROWS, tk), lambda bb, j: (bb, j, 0, 0))]
        out_shape += [jax.ShapeDtypeStruct((b, l, 512), BF16),
                      jax.ShapeDtypeStruct((b, l // tk, N_HEADS * V_ROWS, tk), BF16)]
    return pl.pallas_call(
        functools.partial(_mla_prep_kernel, tm=tm, front=front, tk=tk),
        grid=(b, l // tm),
        in_specs=[_tile_spec(tm, D_MODEL)] + [_const_spec(a) for a in consts]
        + [pl.BlockSpec((2 * LANE, tm), lambda bb, j: (0, j)),
           pl.BlockSpec((tm, 2 * LANE), lambda bb, j: (j, 0))] + [_const_spec(a) for a in kv_consts],
        out_specs=out_specs, out_shape=out_shape,
        compiler_params=_params(), name="mla_prep",
    )(x, *consts, tabq, tabk, *kv_consts)


def _kv_up(c_all, kr_all, wk, ek, wvt, ones_col, tm):
    b, n, _ = c_all.shape
    consts = (wk, ek, wvt, ones_col)
    return pl.pallas_call(
        _kv_up_kernel,
        grid=(b, n // tm),
        in_specs=[_tile_spec(tm, A_KVRANK), _tile_spec(tm, A_ROPE)] + [_const_spec(a) for a in consts],
        out_specs=[_tile_spec(tm, 512), pl.BlockSpec((1, 1, N_HEADS * V_ROWS, tm), lambda bb, j: (bb, j, 0, 0))],
        out_shape=[jax.ShapeDtypeStruct((b, n, 512), BF16),
                   jax.ShapeDtypeStruct((b, n // tm, N_HEADS * V_ROWS, tm), BF16)],
        compiler_params=_params(), name="kv_up",
    )(c_all, kr_all, *consts)


def _flash(qt, k, vt, gout_col, tq, tk, causal, klo, khi):
    b, _, l = qt.shape
    n = k.shape[1]
    nkv = n // tk
    return pl.pallas_call(
        functools.partial(_flash_kernel, tq=tq, tk=tk, nkv=nkv, causal=causal, klo=klo, khi=khi),
        grid=(b, l // tq),
        in_specs=[_cols_spec(512, tq),
                  pl.BlockSpec((1, n, 512), lambda bb, j: (bb, 0, 0), pipeline_mode=pl.Buffered(1)),
                  pl.BlockSpec((1, nkv, N_HEADS * V_ROWS, tk), lambda bb, j: (bb, 0, 0, 0), pipeline_mode=pl.Buffered(1)),
                  _const_spec(gout_col)],
        out_specs=_cols_spec(GROUP_W, tq),
        out_shape=jax.ShapeDtypeStruct((b, GROUP_W, l), F32),
        scratch_shapes=[pltpu.VMEM((N_HEADS * SUBLANE, tq), F32), pltpu.VMEM((N_HEADS * V_ROWS, tq), F32),
                        pltpu.VMEM((N_HEADS * tk, tq), F32), pltpu.VMEM((N_HEADS * tk, tq), F32)],
        compiler_params=_params(), name="mla_flash",
    )(qt, k, vt, gout_col)


def _stack_consts(c):
    n = N_HEADS * c
    lc = int(math.log2(c))
    r, l = np.arange(n)[:, None], np.arange(GROUP_W)[None, :]
    hm = ((r >> lc) == (l >> 6)).astype(np.float32)
    rr, cc = np.arange(n)[:, None], np.arange(n)[None, :]
    same = (rr >> lc) == (cc >> lc)
    tri = [same & (cc <= rr), same & (cc < rr)]
    join = [((rr >> (lb + 1)) == (cc >> (lb + 1))) & (((rr >> lb) & 1) == 1) & (((cc >> lb) & 1) == 0)
            for lb in range(lc)] + [rr == cc]
    tl = np.arange(c)[None, :] <= np.arange(c)[:, None]
    hh = np.arange(GROUP_W)
    bd = (hh[:, None] >> 6) == (hh[None, :] >> 6)
    return (jnp.asarray(hm), jnp.asarray(np.stack(tri).astype(np.float32)),
            jnp.asarray(np.stack(join), BF16), jnp.asarray(tl, BF16), jnp.asarray(bd, BF16))


def _scan_call(kernel, name, x, consts_a, hist, s0, consts_b, width_in, tm, c, front, grp):
    b, l, _ = x.shape
    assert b % grp == 0
    consts_b = tuple(consts_b) + _stack_consts(c)
    rows_spec = lambda rows, width: pl.BlockSpec((grp, rows, width), lambda bb, j: (bb, 0, 0))
    return pl.pallas_call(
        functools.partial(kernel, tm=tm, c=c, front=front, grp=grp),
        grid=(b // grp, l // tm),
        in_specs=[pl.BlockSpec((grp, tm, D_MODEL), lambda bb, j: (bb, j, 0))]
        + [_const_spec(a) for a in consts_a]
        + [rows_spec(SUBLANE, width_in), rows_spec(GROUP_W, GROUP_W)]
        + [_const_spec(a) for a in consts_b],
        out_specs=[pl.BlockSpec((grp, tm, GROUP_W), lambda bb, j: (bb, j, 0)),
                   rows_spec(GROUP_W, GROUP_W), rows_spec(SUBLANE, width_in)],
        out_shape=[jax.ShapeDtypeStruct((b, l, GROUP_W), F32),
                   jax.ShapeDtypeStruct((b, GROUP_W, GROUP_W), F32),
                   jax.ShapeDtypeStruct((b, SUBLANE, width_in), F32)],
        scratch_shapes=[pltpu.VMEM((grp, tm + SUBLANE, width_in), F32),
                        pltpu.VMEM((grp, GROUP_W, GROUP_W), F32)],
        compiler_params=_params(), name=name,
    )(x, *consts_a, hist, s0, *consts_b)


def _ffn(x, ya, yb, yc, yd, wo, g2, wup, cw, wdn, hist, tm, front):
    b, l, _ = x.shape
    return pl.pallas_call(
        functools.partial(_ffn_kernel, tm=tm, front=front),
        grid=(b, l // tm),
        in_specs=[_tile_spec(tm, D_MODEL), _cols_spec(GROUP_W, tm)] + [_tile_spec(tm, GROUP_W)] * 3
        + [_const_spec(a) for a in (wo, g2, wup, cw, wdn)] + [_batch_spec(SUBLANE, 2 * D_FF)],
        out_specs=[_tile_spec(tm, D_MODEL), _batch_spec(SUBLANE, 2 * D_FF)],
        out_shape=[jax.ShapeDtypeStruct((b, l, D_MODEL), F32),
                   jax.ShapeDtypeStruct((b, SUBLANE, 2 * D_FF), F32)],
        scratch_shapes=[pltpu.VMEM((SUBLANE, 2 * D_FF), F32),
                        pltpu.VMEM((tm + SUBLANE, 2 * FFN_CW), F32),
                        pltpu.VMEM((tm + SUBLANE, 2 * FFN_CW), F32)],
        compiler_params=_params(), name="out_ffn",
    )(x, ya, yb, yc, yd, wo, g2, wup, cw, wdn, hist)


def _final_norm(x, g, tm, skip_tiles, out_rows):
    b = x.shape[0]
    return pl.pallas_call(
        _final_norm_kernel,
        grid=(b, out_rows // tm),
        in_specs=[pl.BlockSpec((1, tm, D_MODEL), lambda bb, j: (bb, j + skip_tiles, 0)), _const_spec(g)],
        out_specs=_tile_spec(tm, D_MODEL),
        out_shape=jax.ShapeDtypeStruct((b, out_rows, D_MODEL), F32),
        compiler_params=_params(), name="final_norm",
    )(x, g)


def _np_idx():
    z = IN_COLS
    zpad = lambda n: [z] * n
    rep = lambda base: [base + i for i in range(N_HEADS) for _ in range(HEAD_W)]
    grp = lambda base: [base + g * HEAD_W + i for g in (0, 0, 1, 1) for i in range(HEAD_W)]
    a = (list(range(0, 192)) + zpad(64) + list(range(192, 320))
         + list(range(320, 352)) + zpad(96)
         + list(range(336, 352)) + list(range(320, 336)) + zpad(96))
    b0 = A_COLS
    bcols = list(range(b0, b0 + B_COLS))
    c0 = b0 + B_COLS
    ccols = (list(range(c0, c0 + 256)) + list(range(c0 + 256, c0 + 512))
             + grp(c0 + 512) + grp(c0 + 640) + rep(c0 + 768))
    d0 = c0 + C_COLS
    dcols = list(range(d0, d0 + 1024)) + rep(d0 + 1024) + rep(d0 + 1028)
    xbc_exp = list(range(256)) + grp(256) + grp(384)
    xbc_back = (list(range(256)) + list(range(256, 320)) + list(range(384, 448))
                + list(range(512, 576)) + list(range(640, 704)))
    ffn_perm = []
    for f in range(D_FF // FFN_CW):
        ffn_perm += list(range(f * FFN_CW, (f + 1) * FFN_CW))
        ffn_perm += list(range(D_FF + f * FFN_CW, D_FF + (f + 1) * FFN_CW))
    ffn_back = np.argsort(np.array(ffn_perm))
    as_i = lambda v: np.asarray(v, np.int32)
    return dict(a=as_i(a), b=as_i(bcols), c=as_i(ccols), d=as_i(dcols), xbc_exp=as_i(xbc_exp),
                xbc_back=as_i(xbc_back), ffn_perm=as_i(ffn_perm), ffn_back=as_i(ffn_back))


_IDX = _np_idx()


def _rep_heads(v):
    return jnp.repeat(v, HEAD_W, axis=-1)


def _pad_rows(w, rows=SUBLANE):
    return jnp.pad(w, [(0, rows - w.shape[0])] + [(0, 0)] * (w.ndim - 1))


def _layer_consts(P, l):
    row = lambda v: v.reshape(1, -1).astype(F32)
    w_in = jnp.concatenate([P['w_in'][l], jnp.zeros((D_MODEL, 1), F32)], axis=1)
    c = {}
    c['g1'] = row(P['norm1_g'][l])
    c['wa'] = w_in[:, _IDX['a']].astype(BF16)
    c['wb'] = w_in[:, _IDX['b']].astype(BF16)
    c['wc'] = w_in[:, _IDX['c']].astype(BF16)
    c['wd'] = w_in[:, _IDX['d']].astype(BF16)
    c['gq'] = row(jnp.pad(P['a_gq'][l], (0, 64)))
    wuq = P['a_wuq'][l].reshape(A_QRANK, N_HEADS, A_NOPE + A_ROPE)
    rope = wuq[:, :, A_NOPE:]
    swap = jnp.concatenate([rope[..., 16:], rope[..., :16]], axis=-1)
    zeros = lambda n: jnp.zeros((A_QRANK, N_HEADS, n), F32)
    wq1 = jnp.concatenate([wuq, zeros(32)], axis=-1).reshape(A_QRANK, 512)
    wq2 = jnp.concatenate([zeros(64), swap, zeros(32)], axis=-1).reshape(A_QRANK, 512)
    c['wq1t'] = jnp.pad(wq1, ((0, 64), (0, 0))).T.astype(BF16)
    c['wq2t'] = jnp.pad(wq2, ((0, 64), (0, 0))).T.astype(BF16)
    c['gkv'] = row(P['a_gkv'][l])
    wuk = P['a_wuk'][l].reshape(A_KVRANK, N_HEADS, A_NOPE)
    c['wk'] = jnp.concatenate([wuk, jnp.zeros((A_KVRANK, N_HEADS, 64), F32)], axis=-1
                              ).reshape(A_KVRANK, 512).astype(BF16)
    ek = np.zeros((A_ROPE, N_HEADS, LANE), np.float32)
    for hh in range(N_HEADS):
        ek[np.arange(A_ROPE), hh, A_NOPE + np.arange(A_ROPE)] = 1.0
    c['ek'] = jnp.asarray(ek.reshape(A_ROPE, 512), BF16)
    c['ek128'] = jnp.pad(c['ek'], ((0, LANE - A_ROPE), (0, 0)))
    wuv = P['a_wuv'][l].reshape(A_KVRANK, N_HEADS, HEAD_W)
    c['wvt'] = jnp.concatenate([wuv, jnp.zeros((A_KVRANK, N_HEADS, V_ROWS - HEAD_W), F32)], axis=-1
                               ).reshape(A_KVRANK, N_HEADS * V_ROWS).T.astype(BF16)
    ones = np.zeros((N_HEADS, V_ROWS, 1), np.float32)
    ones[:, HEAD_W, 0] = 1.0
    c['ones_col'] = jnp.asarray(ones.reshape(N_HEADS * V_ROWS, 1))
    c['gout_col'] = P['a_gout'][l].reshape(GROUP_W, 1).astype(F32)
    c['mu'] = row(P['b_mu'][l])
    c['w0'] = row(P['b_w0'][l])
    z64 = jnp.zeros((64, GROUP_W), F32)
    c['wl'] = jnp.concatenate([jnp.concatenate([P['b_w2'][l], z64], axis=1),
                               jnp.concatenate([z64, P['b_a2'][l]], axis=1)], axis=0).astype(BF16)
    c['a0'] = row(P['b_a0'][l])
    c['g2b'] = P['b_g2'][l].astype(BF16)
    c['kk'] = row(P['b_kk'][l])
    c['ka'] = row(P['b_ka'][l])
    c['rk'] = row(P['b_rk'][l])
    c['gnw'] = row(P['b_gnw'][l])
    c['gnb'] = row(P['b_gnb'][l])
    c['c_cw'] = _pad_rows(P['c_convw'][l][:, _IDX['xbc_exp']])
    c['c_cb'] = row(P['c_convb'][l][_IDX['xbc_exp']])
    c['c_dtb'] = row(_rep_heads(P['c_dtb'][l]))
    c['c_alog'] = row(_rep_heads(P['c_alog'][l]))
    c['c_d'] = row(_rep_heads(P['c_d'][l]))
    c['c_gn'] = row(P['c_gnorm'][l])
    c['d_cw'] = _pad_rows(P['d_convw'][l])
    c['d_alog'] = row(_rep_heads(P['d_alog'][l]))
    c['d_dtb'] = row(_rep_heads(P['d_dtb'][l]))
    c['d_gn'] = row(jnp.tile(P['d_gnorm'][l], N_HEADS))
    c['wo'] = P['w_out'][l].astype(BF16)
    c['g2'] = row(P['norm2_g'][l])
    c['wup'] = P['f_wup'][l][:, _IDX['ffn_perm']].astype(BF16)
    c['f_cw'] = _pad_rows(P['f_convw'][l][:, _IDX['ffn_perm']])
    c['wdn'] = P['f_wdown'][l].astype(BF16)
    return c


def _embed_bd(s):
    b = s.shape[0]
    eye = jnp.eye(N_HEADS, dtype=s.dtype)
    return jnp.einsum('bhij,hg->bhigj', s, eye).reshape(b, GROUP_W, GROUP_W)


def _extract_bd(s):
    b = s.shape[0]
    s5 = s.reshape(b, N_HEADS, HEAD_W, N_HEADS, HEAD_W)
    return jnp.stack([s5[:, hh, :, hh, :] for hh in range(N_HEADS)], axis=1)


def _hist8(hist):
    return jnp.pad(hist, ((0, 0), (SUBLANE - hist.shape[1], 0), (0, 0)))


def _rope_table(pos):
    half = A_ROPE // 2
    inv = jnp.power(ROPE_BASE, -jnp.arange(half, dtype=F32) / half)
    ang = pos.astype(F32)[:, None] * inv
    cos, sin = jnp.cos(ang), jnp.sin(ang)
    cos2 = jnp.concatenate([cos, cos], axis=-1)
    sin2 = jnp.concatenate([-sin, sin], axis=-1)
    n = pos.shape[0]
    one = jnp.ones((n, A_NOPE), F32)
    z = lambda w: jnp.zeros((n, w), F32)
    tabq = jnp.concatenate([one, cos2, z(32), z(64), sin2, z(32)], axis=-1).T
    tabk = jnp.concatenate([cos2, z(96), sin2, z(96)], axis=-1)
    return tabq, tabk


def _trunk(x, pos, front, st, P, *, tm, tm_ffn, c, tq, causal, n_keys_pad, scan_rows):
    b, l, _ = x.shape
    tabq, tabk = _rope_table(pos)
    new = {name: [] for name in ('ckv', 'krope', 'rwkv_S', 'rwkv_shift', 'ssd_S', 'ssd_conv',
                                 'gdn_S', 'gdn_conv', 'ffn_conv')}
    zeros_bd = jnp.zeros((b, GROUP_W, GROUP_W), F32)
    for li in range(DEPTH):
        c_ = P[li]
        prep = (x, c_['g1'], c_['wa'], c_['gq'], c_['wq1t'], c_['wq2t'], c_['gkv'], tabq, tabk, tm_ffn, front)
        if st is None:
            qt, ckv, krope, kf, vt = _mla_prep(
                *prep, kv_consts=(c_['wk'], c_['ek128'], c_['wvt'], c_['ones_col']), tk=tq)
            tkv, klo, khi = tq, front, l
            s_b = s_c = s_d = zeros_bd
            shift8 = jnp.zeros((b, SUBLANE, B_COLS), F32)
            chist = jnp.zeros((b, SUBLANE, 768), F32)
            dhist = jnp.zeros((b, SUBLANE, 768), F32)
            fhist = jnp.zeros((b, SUBLANE, 2 * D_FF), F32)
        else:
            qt, ckv, krope = _mla_prep(*prep)
            past = st['ckv'].shape[2]
            padk = n_keys_pad - past - l
            c_all = jnp.concatenate([st['ckv'][li], ckv, jnp.zeros((b, padk, A_KVRANK), F32)], axis=1)
            kr_all = jnp.concatenate([st['krope'][li], krope, jnp.zeros((b, padk, A_ROPE), F32)], axis=1)
            tkv, klo, khi = n_keys_pad, 0, past + l
            kf, vt = _kv_up(c_all, kr_all, c_['wk'], c_['ek'], c_['wvt'], c_['ones_col'], tkv)
            s_b = _embed_bd(st['rwkv_S'][li])
            s_c = _embed_bd(jnp.swapaxes(st['ssd_S'][li], -1, -2))
            s_d = _embed_bd(st['gdn_S'][li])
            shift8 = _hist8(st['rwkv_shift'][li][:, None, :])
            chist = _hist8(st['ssd_conv'][li][:, :, _IDX['xbc_exp']])
            dhist = _hist8(st['gdn_conv'][li])
            fhist = _hist8(st['ffn_conv'][li][:, :, _IDX['ffn_perm']])
        ya = _flash(qt, kf, vt, c_['gout_col'], tq, tkv, causal, klo, khi)
        yb, sb_new, shift_new = _scan_call(
            _rwkv_kernel, "rwkv7", x, (c_['g1'], c_['wb']), shift8, s_b,
            (c_['mu'], c_['w0'], c_['wl'], c_['a0'], c_['g2b'], c_['kk'], c_['ka'], c_['rk'],
             c_['gnw'], c_['gnb']), B_COLS, tm, c, front, scan_rows)
        yc, sc_new, chist_new = _scan_call(
            _ssd_kernel, "ssd", x, (c_['g1'], c_['wc']), chist, s_c,
            (c_['c_cw'], c_['c_cb'], c_['c_dtb'], c_['c_alog'], c_['c_d'], c_['c_gn']),
            768, tm, c, front, scan_rows)
        yd, sd_new, dhist_new = _scan_call(
            _gdn_kernel, "gdn", x, (c_['g1'], c_['wd']), dhist, s_d,
            (c_['d_cw'], c_['d_alog'], c_['d_dtb'], c_['d_gn']), 768, tm, c, front, scan_rows)
        x, fhist_new = _ffn(x, ya, yb, yc, yd, c_['wo'], c_['g2'], c_['wup'], c_['f_cw'], c_['wdn'],
                            fhist, tm_ffn, front)
        new['ckv'].append(ckv[:, front:])
        new['krope'].append(krope[:, front:])
        new['rwkv_S'].append(_extract_bd(sb_new))
        new['rwkv_shift'].append(shift_new[:, SUBLANE - 1])
        new['ssd_S'].append(jnp.swapaxes(_extract_bd(sc_new), -1, -2))
        new['ssd_conv'].append(chist_new[:, SUBLANE - (C_CONV - 1):][:, :, _IDX['xbc_back']])
        new['gdn_S'].append(_extract_bd(sd_new))
        new['gdn_conv'].append(dhist_new[:, SUBLANE - (D_CONV - 1):])
        new['ffn_conv'].append(fhist_new[:, SUBLANE - (FFN_CONV - 1):][:, :, _IDX['ffn_back']])
    return x, {name: jnp.stack(vals) for name, vals in new.items()}


def kernel(x_prompt, x_sample, cache_mla_ckv, cache_mla_krope, state_rwkv, state_rwkv_shift, state_ssd, state_ssd_conv, state_gdn, state_gdn_conv, state_ffn_conv, meta_tokens, norm1_g, w_in, a_gq, a_wuq, a_gkv, a_wuk, a_wuv, a_gout, b_mu, b_w0, b_w2, b_a0, b_a2, b_g2, b_kk, b_ka, b_rk, b_gnw, b_gnb, c_convw, c_convb, c_dtb, c_alog, c_d, c_gnorm, d_convw, d_alog, d_dtb, d_gnorm, w_out, norm2_g, f_wup, f_convw, f_wdown, final_g):
    P = dict(norm1_g=norm1_g, w_in=w_in, a_gq=a_gq, a_wuq=a_wuq, a_gkv=a_gkv, a_wuk=a_wuk,
             a_wuv=a_wuv, a_gout=a_gout, b_mu=b_mu, b_w0=b_w0, b_w2=b_w2, b_a0=b_a0, b_a2=b_a2,
             b_g2=b_g2, b_kk=b_kk, b_ka=b_ka, b_rk=b_rk, b_gnw=b_gnw, b_gnb=b_gnb,
             c_convw=c_convw, c_convb=c_convb, c_dtb=c_dtb, c_alog=c_alog, c_d=c_d, c_gnorm=c_gnorm,
             d_convw=d_convw, d_alog=d_alog, d_dtb=d_dtb, d_gnorm=d_gnorm,
             w_out=w_out, norm2_g=norm2_g, f_wup=f_wup, f_convw=f_convw, f_wdown=f_wdown)
    P = [_layer_consts(P, li) for li in range(DEPTH)]
    fin = final_g.reshape(1, -1).astype(F32)
    b_p, seq, _ = x_prompt.shape
    lx = N_META + seq
    assert seq % ATT_BLOCK == 0 and N_META <= ATT_BLOCK
    front = ATT_BLOCK - N_META
    l_pad = front + lx
    meta = jnp.broadcast_to(meta_tokens[None].astype(F32), (b_p, N_META, D_MODEL))
    x_ext = jnp.concatenate([jnp.zeros((b_p, front, D_MODEL), F32), meta, x_prompt], axis=1)
    pos_p = jnp.arange(l_pad, dtype=jnp.int32) - (front + N_META)
    tm_ffn = max(t for t in (ROW_TILE, 2 * ROW_TILE, 3 * ROW_TILE) if l_pad % t == 0)
    y_p, ns_p = _trunk(x_ext, pos_p, front, None, P, tm=ROW_TILE, tm_ffn=tm_ffn, c=CHUNK, tq=ATT_BLOCK,
                       causal=True, n_keys_pad=l_pad, scan_rows=SCAN_ROWS)
    y_prompt = _final_norm(y_p, fin, ROW_TILE, (front + N_META) // ROW_TILE, seq)
    b_s, t_s, _ = x_sample.shape
    past = cache_mla_ckv.shape[2]
    assert t_s <= CHUNK and t_s % 16 == 0 and (t_s & (t_s - 1)) == 0
    st_s = dict(ckv=cache_mla_ckv, krope=cache_mla_krope, rwkv_S=state_rwkv, rwkv_shift=state_rwkv_shift,
                ssd_S=state_ssd, ssd_conv=state_ssd_conv, gdn_S=state_gdn, gdn_conv=state_gdn_conv,
                ffn_conv=state_ffn_conv)
    pos_s = past + jnp.arange(t_s, dtype=jnp.int32)
    n_keys_pad = -(-(past + t_s) // LANE) * LANE
    y_s, ns_s = _trunk(x_sample, pos_s, 0, st_s, P, tm=t_s, tm_ffn=t_s, c=t_s, tq=t_s, causal=False,
                       n_keys_pad=n_keys_pad, scan_rows=math.gcd(b_s, SCAN_ROWS_SHORT))
    y_sample = _final_norm(y_s, fin, t_s, 0, t_s)
    keys = ('ckv', 'krope', 'rwkv_S', 'rwkv_shift', 'ssd_S', 'ssd_conv', 'gdn_S', 'gdn_conv', 'ffn_conv')
    return (y_prompt, y_sample) + tuple(ns_p[k] for k in keys) + tuple(ns_s[k] for k in keys)
```

```python
import functools
import math

import numpy as np
import jax
import jax.numpy as jnp
from jax import lax
from jax.experimental import pallas as pl
from jax.experimental.pallas import tpu as pltpu

F32 = jnp.float32
BF16 = jnp.bfloat16

D_MODEL = 1024
DEPTH = 4
CHUNK = 64
N_META = 16
EPS = 1e-6
L2_EPS = 1e-6
GROUP_W = 256
N_HEADS = 4
HEAD_W = 64
A_NOPE = 64
A_ROPE = 32
A_QRANK = 192
A_KVRANK = 128
A_SCALE = (A_NOPE + A_ROPE) ** -0.5
ROPE_BASE = 10000.0
B_GN_EPS = 64e-5
B_COLS = 1024
C_CONV = 4
D_CONV = 4
D_FF = 2816
FFN_CONV = 3
A_COLS = 352
C_COLS = 772
D_COLS = 1032
IN_COLS = A_COLS + B_COLS + C_COLS + D_COLS

LANE = 128
SUBLANE = 8
VMEM_LIMIT = 56 * 1024 * 1024
NEG = -1e30
LOG2E = math.log2(math.e)
V_ROWS = 80
FFN_CW = 256
FFN_DOWN_GROUP = 4
ATT_BLOCK = 256
ROW_TILE = 256
SCAN_TILE = 128
SCAN_ROWS = 4
SCAN_ROWS_SHORT = 8


def _mm(a, b):
    return jnp.dot(a.astype(BF16), b.astype(BF16), preferred_element_type=F32)


def _mm_nt(a, b):
    return lax.dot_general(a.astype(BF16), b.astype(BF16), (((1,), (1,)), ((), ())),
                           preferred_element_type=F32)


def _mm_tn(a, b):
    return lax.dot_general(a.astype(BF16), b.astype(BF16), (((0,), (0,)), ((), ())),
                           preferred_element_type=F32)


def _split2(x):
    hi = x.astype(BF16)
    lo = (x - hi.astype(F32)).astype(BF16)
    return hi, lo


def _mm_x2(x, w):
    hi, lo = _split2(x)
    return (jnp.dot(hi, w, preferred_element_type=F32)
            + jnp.dot(lo, w, preferred_element_type=F32))


def _rms(x, g, n):
    ms = jnp.sum(x * x, axis=-1, keepdims=True) * (1.0 / n)
    return x * lax.rsqrt(ms + EPS) * g


def _softplus(x):
    return jnp.maximum(x, 0.0) + jnp.log1p(jnp.exp(-jnp.abs(x)))


def _sigmoid(x):
    return 1.0 / (1.0 + jnp.exp(-x))


def _silu(x):
    return x * _sigmoid(x)


def _iota(shape, axis):
    return lax.broadcasted_iota(jnp.int32, shape, axis)


def _norm_in(x_ref, g_ref, j, tm, front):
    h = _rms(x_ref[0], g_ref[...], D_MODEL)
    if front > 0:
        rows = j * tm + _iota((tm, 1), 0)
        h = jnp.where(rows >= front, h, 0.0)
    return h.astype(BF16)


def _norm_in_rows(x_ref, g_ref, j, tm, front):
    x = x_ref[...]
    h = _rms(x, g_ref[...], D_MODEL)
    if front > 0:
        pos = j * tm + _iota((1, tm, 1), 1)
        h = jnp.where(pos >= front, h, 0.0)
    return h.reshape(x.shape[0] * tm, D_MODEL).astype(BF16)


def _hsum(x, bd):
    return _mm_x2(x, bd)


TRI_INCL, TRI_STRICT = 0, 1


class _Stk:
    def __init__(self, c, hm_ref, tri_ref, join_ref, tl_ref):
        self.c = c
        self.n = N_HEADS * c
        self.steps = int(math.log2(c))
        assert 1 << self.steps == c
        self.hm_ref, self.tri_ref, self.join_ref, self.tl_ref = hm_ref, tri_ref, join_ref, tl_ref
        self.hm_b = None

    def mask(self, which):
        return self.tri_ref[which]

    def tile(self, x):
        return jnp.concatenate([x] * N_HEADS, axis=0)

    def stack(self, x):
        return self.tile(x) * self.hm_ref[...]

    def unstack(self, xs):
        c = self.c
        return xs[0:c] + xs[c:2 * c] + xs[2 * c:3 * c] + xs[3 * c:4 * c]

    def cumsum(self, x):
        hi, lo = _split2(x)
        tl = self.tl_ref[...]
        return jnp.dot(tl, hi, preferred_element_type=F32) + jnp.dot(tl, lo, preferred_element_type=F32)

    def decay(self, g_cum):
        gcol = jnp.min(self.stack(g_cum), axis=-1, keepdims=True)
        grow = jnp.broadcast_to(gcol, (self.n, LANE)).T[0:1, :]
        return jnp.exp(jnp.minimum(gcol - grow, 0.0)) * self.mask(TRI_INCL)

    def tri_inv_many(self, lowers):
        lowers = [lw.astype(BF16) for lw in lowers]
        minvs = [self.join_ref[self.steps] - lw * self.join_ref[0] for lw in lowers]
        for lb in range(1, self.steps):
            ts = [_mm(m, lw * self.join_ref[lb]) for m, lw in zip(minvs, lowers)]
            minvs = [m - _mm(t, m).astype(BF16) for t, m in zip(ts, minvs)]
        return minvs

    def stack_b(self, x):
        if self.hm_b is None:
            self.hm_b = self.hm_ref[...].astype(BF16)
        return self.tile(x.astype(BF16)) * self.hm_b


def _mla_prep_kernel(x_ref, g1_ref, wa_ref, gq_ref, wq1t_ref, wq2t_ref, gkv_ref, tabq_ref, tabk_ref,
                     *rest, tm, front, tk):
    if tk:
        wk_ref, ek_ref, wvt_ref, ones_ref, qt_out, ckv_out, kr_out, k_out, vt_out = rest
    else:
        qt_out, ckv_out, kr_out = rest
    j = pl.program_id(1)
    h = _norm_in(x_ref, g1_ref, j, tm, front)
    pa = jnp.dot(h, wa_ref[...], preferred_element_type=F32)
    qn = _rms(pa[:, 0:256], gq_ref[...], A_QRANK)
    q1t = _mm_nt(wq1t_ref[...], qn)
    q2t = _mm_nt(wq2t_ref[...], qn)
    tabq = tabq_ref[...]
    cos4 = jnp.concatenate([tabq[0:128]] * N_HEADS, axis=0)
    sin4 = jnp.concatenate([tabq[128:256]] * N_HEADS, axis=0)
    qt_out[0] = ((q1t * cos4 + q2t * sin4) * (A_SCALE * LOG2E)).astype(BF16)
    c = _rms(pa[:, 256:384], gkv_ref[...], A_KVRANK)
    ckv_out[0] = c
    tabk = tabk_ref[...]
    kr = pa[:, 384:512] * tabk[:, 0:128] + pa[:, 512:640] * tabk[:, 128:256]
    kr_out[0] = kr[:, 0:A_ROPE]
    if tk:
        cb = c.astype(BF16)
        k_out[0] = (jnp.dot(cb, wk_ref[...], preferred_element_type=F32)
                    + jnp.dot(kr.astype(BF16), ek_ref[...], preferred_element_type=F32)).astype(BF16)
        vt = (_mm_nt(wvt_ref[...], cb) + ones_ref[...]).astype(BF16)
        for t in range(tm // tk):
            vt_out[0, t] = vt[:, t * tk:(t + 1) * tk]


def _kv_up_kernel(c_ref, kr_ref, wk_ref, ek_ref, wvt_ref, ones_ref, k_out, vt_out):
    c = c_ref[0].astype(BF16)
    kr = kr_ref[0].astype(BF16)
    k = (jnp.dot(c, wk_ref[...], preferred_element_type=F32)
         + jnp.dot(kr, ek_ref[...], preferred_element_type=F32))
    k_out[0] = k.astype(BF16)
    vt_out[0, 0] = (_mm_nt(wvt_ref[...], c) + ones_ref[...]).astype(BF16)


def _flash_kernel(qt_ref, k_ref, vt_ref, gout_ref, o_ref, m_sc, acc_sc, sa_sc, sb_sc,
                  *, tq, tk, nkv, causal, klo, khi):
    i = pl.program_id(1)
    m_sc[...] = jnp.full((N_HEADS * SUBLANE, tq), NEG, F32)
    acc_sc[...] = jnp.zeros((N_HEADS * V_ROWS, tq), F32)
    vrows = [slice(V_ROWS * h, V_ROWS * (h + 1)) for h in range(N_HEADS)]
    heads = [slice(LANE * h, LANE * (h + 1)) for h in range(N_HEADS)]
    s_bufs = (sa_sc, sb_sc)

    def produce(jb, slot):
        start = pl.multiple_of(jb * tk, tk)
        for h, rows in enumerate(heads):
            s_bufs[slot][h * tk:(h + 1) * tk, :] = jnp.dot(
                k_ref[0, pl.ds(start, tk), rows], qt_ref[0, rows, :], preferred_element_type=F32)

    def consume(jb, slot, masked, nxt):
        if nxt is not None:
            produce(*nxt)
        if masked:
            kpos = jb * tk + _iota((tk, 1), 0)
            qpos = i * tq + _iota((1, tq), 1)
            vis = (kpos >= klo) & (kpos < khi)
            if causal:
                vis = vis & ((kpos >> 6) <= (qpos >> 6))
        m_all = m_sc[...]
        acc_all = acc_sc[...]
        m_out, alphas, ps = [], [], []
        for h in range(N_HEADS):
            s = s_bufs[slot][h * tk:(h + 1) * tk, :]
            if masked:
                s = jnp.where(vis, s, NEG)
            m_prev = m_all[SUBLANE * h:SUBLANE * (h + 1)]
            m_new = jnp.maximum(m_prev, jnp.max(s, axis=0, keepdims=True))
            alphas.append(jnp.exp2(m_prev[0:1] - m_new[0:1]))
            ps.append(jnp.exp2(s - m_new[0:1]).astype(BF16))
            m_out.append(m_new)
        acc_out = [alphas[h] * acc_all[rows]
                   + jnp.dot(vt_ref[0, jb, rows, :], ps[h], preferred_element_type=F32)
                   for h, rows in enumerate(vrows)]
        m_sc[...] = jnp.concatenate(m_out, axis=0)
        acc_sc[...] = jnp.concatenate(acc_out, axis=0)

    produce(0, 0)
    if causal:
        @pl.when(i == 0)
        def _():
            consume(0, 0, True, None)

        @pl.when(i > 0)
        def _():
            consume(0, 0, True, (1, 1))
            quads = (i - 1) >> 2

            def pair(jb):
                consume(jb, 1, False, (jb + 1, 0))
                consume(jb + 1, 0, False, (jb + 2, 1))

            def body(t, carry):
                pair(1 + 4 * t)
                pair(3 + 4 * t)
                return carry

            lax.fori_loop(0, quads, body, 0)

            @pl.when(((i - 1) & 2) != 0)
            def _():
                pair(1 + 4 * quads)

            @pl.when((i & 1) == 0)
            def _():
                consume(i - 1, 1, False, (i, 0))
                consume(i, 0, True, None)

            @pl.when((i & 1) == 1)
            def _():
                consume(i, 1, True, None)
    else:
        for jb in range(nkv):
            consume(jb, jb & 1, True, (jb + 1, (jb + 1) & 1) if jb + 1 < nkv else None)
    outs = []
    for h in range(N_HEADS):
        a = acc_sc[V_ROWS * h:V_ROWS * (h + 1), :]
        outs.append(a[0:HEAD_W] / a[HEAD_W:HEAD_W + 1])
    yat = jnp.concatenate(outs, axis=0)
    ms = jnp.sum(yat * yat, axis=0, keepdims=True) * (1.0 / GROUP_W)
    o_ref[0] = yat * lax.rsqrt(ms + EPS) * gout_ref[...]


def _chunk_ids(grp, tm, c):
    ids = [(g, ci) for ci in range(tm // c) for g in range(grp)]
    return ids, {(g, ci): slice(g * tm + ci * c, g * tm + (ci + 1) * c) for g, ci in ids}


def _carry_rows(work, g, new_rows, tm):
    work[g, SUBLANE:SUBLANE + tm, :] = new_rows
    return work[g, tm:tm + SUBLANE, :]


def _rwkv_kernel(x_ref, g1_ref, wb_ref, shift_ref, s0_ref, mu_ref, w0_ref, wl_ref, a0_ref,
                 g2_ref, kk_ref, ka_ref, rk_ref, gnw_ref, gnb_ref, hm_ref, tri_ref, join_ref, tl_ref, bd_ref,
                 y_out, s_out, shift_out, work, st, *, tm, c, front, grp):
    j = pl.program_id(1)

    @pl.when(j == 0)
    def _():
        work[:, 0:SUBLANE, :] = shift_ref[...]
        st[...] = s0_ref[...]

    h = _norm_in_rows(x_ref, g1_ref, j, tm, front)
    cols = jnp.dot(h, wb_ref[...], preferred_element_type=F32)
    shifted = []
    for g in range(grp):
        tail = _carry_rows(work, g, cols[g * tm:(g + 1) * tm], tm)
        shifted.append(work[g, pl.ds(SUBLANE - 1, tm), :])
        work[g, 0:SUBLANE, :] = tail
        shift_out[g] = tail
    shifted = jnp.concatenate(shifted, axis=0)
    xm = cols + (shifted - cols) * mu_ref[...]
    r = xm[:, 0:256]
    k = xm[:, 256:512]
    v = xm[:, 512:768]
    lora = xm[:, 768:896]
    dg = xm[:, 896:1024]
    lora = jnp.where(_iota((grp * tm, LANE), 1) < 64, jnp.tanh(lora), lora)
    ll = _mm(lora, wl_ref[...])
    w_log = -_softplus(-(w0_ref[...] + ll[:, 0:256])) - 0.5
    logd = -jnp.exp(w_log)
    a = _sigmoid(a0_ref[...] + ll[:, 256:512])
    g_gate = _mm(_sigmoid(dg), g2_ref[...])
    bd = bd_ref[...]
    kkr = k * kk_ref[...]
    kk = kkr * lax.rsqrt(_hsum(kkr * kkr, bd) + L2_EPS)
    k2 = k * (1.0 + (a - 1.0) * ka_ref[...])

    sk = _Stk(c, hm_ref, tri_ref, join_ref, tl_ref)
    ids, rows = _chunk_ids(grp, tm, c)
    strict, incl = sk.mask(TRI_STRICT), sk.mask(TRI_INCL)
    gcs = {i: sk.cumsum(logd[rows[i]]) for i in ids}
    pre = {}
    for i in ids:
        gc, ld = gcs[i], logd[rows[i]]
        eg, eng = jnp.exp(gc), jnp.exp(-gc)
        bt = kk[rows[i]] * a[rows[i]] * eng
        kt = k2[rows[i]] * eng
        dc = eg[c - 1:c, :]
        pre[i] = dict(at_s=sk.stack_b(-kk[rows[i]] * jnp.exp(gc - ld)), rt_s=sk.stack_b(r[rows[i]] * eg),
                      v_s=sk.stack_b(v[rows[i]]), bt_t=sk.tile(bt.astype(BF16)),
                      kt_t=sk.tile(kt.astype(BF16)), dc=dc,
                      bd_s=sk.stack_b(bt * dc), kd_s=sk.stack_b(kt * dc))
    lab = {i: _mm_nt(pre[i]['at_s'], pre[i]['bt_t']) * strict for i in ids}
    aak = {i: (_mm_nt(pre[i]['at_s'], pre[i]['kt_t']) * strict).astype(BF16) for i in ids}
    arb = {i: (_mm_nt(pre[i]['rt_s'], pre[i]['bt_t']) * incl).astype(BF16) for i in ids}
    ark = {i: (_mm_nt(pre[i]['rt_s'], pre[i]['kt_t']) * incl).astype(BF16) for i in ids}
    minv = {i: m.astype(BF16) for i, m in zip(ids, sk.tri_inv_many([-lab[i] for i in ids]))}
    a2 = {i: _mm(aak[i], pre[i]['v_s']) for i in ids}
    u0 = {i: _mm(minv[i], a2[i]) for i in ids}
    m1 = {i: _mm(minv[i], pre[i]['at_s']).astype(BF16) for i in ids}
    ork = {i: _mm(ark[i], pre[i]['v_s']) for i in ids}
    skv = {i: _mm_tn(pre[i]['v_s'], pre[i]['kd_s']) for i in ids}

    s = [st[g] for g in range(grp)]
    outs = {}
    for ci in range(tm // c):
        sb = [s[g].astype(BF16) for g in range(grp)]
        us = [u0[(g, ci)] + _mm_nt(m1[(g, ci)], sb[g]) for g in range(grp)]
        oq = [_mm_nt(pre[(g, ci)]['rt_s'], sb[g]) for g in range(grp)]
        for g in range(grp):
            i = (g, ci)
            s[g] = s[g] * pre[i]['dc'] + _mm_tn(us[g], pre[i]['bd_s']) + skv[i]
            outs[i] = sk.unstack(oq[g] + _mm(arb[i], us[g]) + ork[i])
    for g in range(grp):
        st[g] = s[g]
        s_out[g] = s[g]
    o = jnp.concatenate([outs[(g, ci)] for g in range(grp) for ci in range(tm // c)], axis=0)
    mean = _hsum(o, bd) * (1.0 / HEAD_W)
    d = o - mean
    var = _hsum(d * d, bd) * (1.0 / HEAD_W)
    o = d * lax.rsqrt(var + B_GN_EPS) * gnw_ref[...] + gnb_ref[...]
    bonus = _hsum(r * k2 * rk_ref[...], bd) * v
    y_out[...] = ((o + bonus) * g_gate).reshape(grp, tm, GROUP_W)


def _conv4_rows(work, new, wv, hist_out, grp, tm):
    ys = []
    for g in range(grp):
        tail = _carry_rows(work, g, new[g * tm:(g + 1) * tm], tm)
        y = work[g, pl.ds(SUBLANE - 3, tm), :] * wv[0:1, :]
        y = y + work[g, pl.ds(SUBLANE - 2, tm), :] * wv[1:2, :]
        y = y + work[g, pl.ds(SUBLANE - 1, tm), :] * wv[2:3, :]
        ys.append(y + work[g, pl.ds(SUBLANE, tm), :] * wv[3:4, :])
        work[g, 0:SUBLANE, :] = tail
        hist_out[g] = tail
    return jnp.concatenate(ys, axis=0)


def _ssd_kernel(x_ref, g1_ref, wc_ref, hist_ref, s0_ref, cw_ref, cb_ref, dtb_ref, alog_ref,
                dskip_ref, gn_ref, hm_ref, tri_ref, join_ref, tl_ref, bd_ref,
                y_out, s_out, hist_out, work, st, *, tm, c, front, grp):
    j = pl.program_id(1)

    @pl.when(j == 0)
    def _():
        work[:, 0:SUBLANE, :] = hist_ref[...]
        st[...] = s0_ref[...]

    h = _norm_in_rows(x_ref, g1_ref, j, tm, front)
    pc = jnp.dot(h, wc_ref[...], preferred_element_type=F32)
    z = pc[:, 0:256]
    xbc = _silu(_conv4_rows(work, pc[:, 256:1024], cw_ref[...], hist_out, grp, tm) + cb_ref[...])
    xs = xbc[:, 0:256]
    bm = xbc[:, 256:512]
    cm = xbc[:, 512:768]
    dt = _softplus(pc[:, 1024:1280] + dtb_ref[...])
    if front > 0:
        pos = j * tm + _iota((1, tm, 1), 1)
        dt = jnp.where(pos >= front, dt.reshape(grp, tm, GROUP_W), 0.0).reshape(grp * tm, GROUP_W)
    a = dt * (-jnp.exp(alog_ref[...]))
    xdt = xs * dt

    sk = _Stk(c, hm_ref, tri_ref, join_ref, tl_ref)
    ids, rows = _chunk_ids(grp, tm, c)
    acs = {i: sk.cumsum(a[rows[i]]) for i in ids}
    dms = {i: sk.decay(acs[i]) for i in ids}
    xdt_s = {i: sk.stack_b(xdt[rows[i]]) for i in ids}
    amat = {i: _mm_nt(sk.stack_b(cm[rows[i]]), sk.tile(bm[rows[i]].astype(BF16))) * dms[i] for i in ids}
    ydiag = {i: _mm(amat[i], xdt_s[i]) for i in ids}
    sx = {i: _mm_tn(sk.stack_b(bm[rows[i]] * jnp.exp(acs[i][c - 1:c, :] - acs[i])), xdt_s[i]) for i in ids}
    ce_s = {i: sk.stack_b(cm[rows[i]] * jnp.exp(acs[i])) for i in ids}

    s = [st[g] for g in range(grp)]
    outs = {}
    for ci in range(tm // c):
        for g in range(grp):
            i = (g, ci)
            outs[i] = sk.unstack(ydiag[i] + _mm(ce_s[i], s[g]))
            s[g] = s[g] * jnp.exp(acs[i][c - 1:c, :]) + sx[i]
    for g in range(grp):
        st[g] = s[g]
        s_out[g] = s[g]
    y = jnp.concatenate([outs[(g, ci)] for g in range(grp) for ci in range(tm // c)], axis=0)
    y = y + dskip_ref[...] * xs
    y_out[...] = _rms(y * _silu(z), gn_ref[...], GROUP_W).reshape(grp, tm, GROUP_W)


def _gdn_kernel(x_ref, g1_ref, wd_ref, hist_ref, s0_ref, cw_ref, alog_ref, dtb_ref, gn_ref,
                hm_ref, tri_ref, join_ref, tl_ref, bd_ref,
                y_out, s_out, hist_out, work, st, *, tm, c, front, grp):
    j = pl.program_id(1)

    @pl.when(j == 0)
    def _():
        work[:, 0:SUBLANE, :] = hist_ref[...]
        st[...] = s0_ref[...]

    h = _norm_in_rows(x_ref, g1_ref, j, tm, front)
    pd = jnp.dot(h, wd_ref[...], preferred_element_type=F32)
    qkv = _silu(_conv4_rows(work, pd[:, 0:768], cw_ref[...], hist_out, grp, tm))
    z = pd[:, 768:1024]
    beta = _sigmoid(pd[:, 1024:1280])
    g_log = -jnp.exp(alog_ref[...]) * _softplus(pd[:, 1280:1536] + dtb_ref[...])
    bd = bd_ref[...]
    q = qkv[:, 0:256]
    k = qkv[:, 256:512]
    v = qkv[:, 512:768]
    q = q * lax.rsqrt(_hsum(q * q, bd) + L2_EPS) * (HEAD_W ** -0.5)
    k = k * lax.rsqrt(_hsum(k * k, bd) + L2_EPS)

    sk = _Stk(c, hm_ref, tri_ref, join_ref, tl_ref)
    ids, rows = _chunk_ids(grp, tm, c)
    strict = sk.mask(TRI_STRICT)
    gcs = {i: sk.cumsum(g_log[rows[i]]) for i in ids}
    dms = {i: sk.decay(gcs[i]) for i in ids}
    kb = {i: k[rows[i]] * beta[rows[i]] for i in ids}
    k_t = {i: sk.tile(k[rows[i]].astype(BF16)) for i in ids}
    lower = {i: _mm_nt(sk.stack_b(kb[i]), k_t[i]) * dms[i] * strict for i in ids}
    aqk = {i: (_mm_nt(sk.stack_b(q[rows[i]]), k_t[i]) * dms[i]).astype(BF16) for i in ids}
    tinv = {i: t.astype(BF16) for i, t in zip(ids, sk.tri_inv_many([lower[i] for i in ids]))}
    u = {i: _mm(tinv[i], sk.stack_b(v[rows[i]] * beta[rows[i]])) for i in ids}
    w = {i: _mm(tinv[i], sk.stack_b(kb[i] * jnp.exp(gcs[i]))).astype(BF16) for i in ids}
    qe_s = {i: sk.stack_b(q[rows[i]] * jnp.exp(gcs[i])) for i in ids}
    kd_s = {i: sk.stack_b(k[rows[i]] * jnp.exp(gcs[i][c - 1:c, :] - gcs[i])) for i in ids}

    s = [st[g] for g in range(grp)]
    outs = {}
    for ci in range(tm // c):
        sb = [s[g].astype(BF16) for g in range(grp)]
        vn = [u[(g, ci)] - _mm(w[(g, ci)], sb[g]) for g in range(grp)]
        oq = [_mm(qe_s[(g, ci)], sb[g]) for g in range(grp)]
        for g in range(grp):
            i = (g, ci)
            s[g] = s[g] * jnp.exp(gcs[i][c - 1:c, :]) + _mm_tn(kd_s[i], vn[g])
            outs[i] = sk.unstack(oq[g] + _mm(aqk[i], vn[g]))
    for g in range(grp):
        st[g] = s[g]
        s_out[g] = s[g]
    o = jnp.concatenate([outs[(g, ci)] for g in range(grp) for ci in range(tm // c)], axis=0)
    ms = _hsum(o * o, bd) * (1.0 / HEAD_W)
    y_out[...] = (o * lax.rsqrt(ms + EPS) * gn_ref[...] * _silu(z)).reshape(grp, tm, GROUP_W)


def _ffn_kernel(x_ref, ya_ref, yb_ref, yc_ref, yd_ref, wo_ref, g2_ref, wup_ref, cw_ref, wdn_ref,
                hist_ref, xo_ref, hist_out, carry, work_a, work_b, *, tm, front):
    j = pl.program_id(1)

    @pl.when(j == 0)
    def _():
        carry[...] = hist_ref[0]

    x = x_ref[0] + _mm_tn(ya_ref[0], wo_ref[0:GROUP_W, :])
    for idx, y_ref in ((1, yb_ref), (2, yc_ref), (3, yd_ref)):
        x = x + jnp.dot(y_ref[0].astype(BF16), wo_ref[GROUP_W * idx:GROUP_W * (idx + 1), :],
                        preferred_element_type=F32)
    h2 = _rms(x, g2_ref[...], D_MODEL)
    if front > 0:
        rows = j * tm + _iota((tm, 1), 0)
        h2 = jnp.where(rows >= front, h2, 0.0)
    h2 = h2.astype(BF16)
    acc = jnp.zeros((tm, D_MODEL), F32)
    w2 = 2 * FFN_CW
    n_f = D_FF // FFN_CW
    up = lambda f: jnp.dot(h2, wup_ref[:, f * w2:(f + 1) * w2], preferred_element_type=F32)
    u_next = up(0)
    acts, k0 = [], 0
    for f in range(n_f):
        cols = slice(f * w2, (f + 1) * w2)
        u = u_next
        if f + 1 < n_f:
            u_next = up(f + 1)
        work = (work_a, work_b)[f % 2]
        work[0:SUBLANE, :] = carry[:, cols]
        work[SUBLANE:SUBLANE + tm, :] = u
        cw = cw_ref[:, cols]
        y = (work[pl.ds(SUBLANE - 2, tm), :] * cw[0:1, :]
             + work[pl.ds(SUBLANE - 1, tm), :] * cw[1:2, :] + u * cw[2:3, :])
        carry[:, cols] = work[tm:tm + SUBLANE, :]
        acts.append((_silu(y[:, 0:FFN_CW]) * y[:, FFN_CW:w2]).astype(BF16))
        if len(acts) == FFN_DOWN_GROUP or f + 1 == n_f:
            k1 = k0 + FFN_CW * len(acts)
            acc = acc + jnp.dot(jnp.concatenate(acts, axis=1) if len(acts) > 1 else acts[0],
                                wdn_ref[k0:k1, :], preferred_element_type=F32)
            acts, k0 = [], k1
    xo_ref[0] = x + acc
    hist_out[0] = carry[...]


def _final_norm_kernel(x_ref, g_ref, o_ref):
    o_ref[0] = _rms(x_ref[0], g_ref[...], D_MODEL)


def _const_spec(arr):
    nd = arr.ndim
    return pl.BlockSpec(arr.shape, lambda b, j: (0,) * nd, pipeline_mode=pl.Buffered(1))


def _tile_spec(tm, width):
    return pl.BlockSpec((1, tm, width), lambda b, j: (b, j, 0))


def _batch_spec(rows, width):
    return pl.BlockSpec((1, rows, width), lambda b, j: (b, 0, 0))


def _params():
    return pltpu.CompilerParams(dimension_semantics=("arbitrary", "arbitrary"),
                                vmem_limit_bytes=VMEM_LIMIT)


def _cols_spec(rows, tm):
    return pl.BlockSpec((1, rows, tm), lambda b, j: (b, 0, j))


def _mla_prep(x, g1, wa, gq, wq1t, wq2t, gkv, tabq, tabk, tm, front, kv_consts=(), tk=0):
    b, l, _ = x.shape
    consts = (g1, wa, gq, wq1t, wq2t, gkv)
    out_specs = [_cols_spec(512, tm), _tile_spec(tm, A_KVRANK), _tile_spec(tm, A_ROPE)]
    out_shape = [jax.ShapeDtypeStruct((b, 512, l), BF16),
                 jax.ShapeDtypeStruct((b, l, A_KVRANK), F32),
                 jax.ShapeDtypeStruct((b, l, A_ROPE), F32)]
    if tk:
        nb = tm // tk
        out_specs += [_tile_spec(tm, 512),
                      pl.BlockSpec((1, nb, N_HEADS * V_ROWS, tk), lambda bb, j: (bb, j, 0, 0))]
        out_shape += [jax.ShapeDtypeStruct((b, l, 512), BF16),
                      jax.ShapeDtypeStruct((b, l // tk, N_HEADS * V_ROWS, tk), BF16)]
    return pl.pallas_call(
        functools.partial(_mla_prep_kernel, tm=tm, front=front, tk=tk),
        grid=(b, l // tm),
        in_specs=[_tile_spec(tm, D_MODEL)] + [_const_spec(a) for a in consts]
        + [pl.BlockSpec((2 * LANE, tm), lambda bb, j: (0, j)),
           pl.BlockSpec((tm, 2 * LANE), lambda bb, j: (j, 0))] + [_const_spec(a) for a in kv_consts],
        out_specs=out_specs, out_shape=out_shape,
        compiler_params=_params(), name="mla_prep",
    )(x, *consts, tabq, tabk, *kv_consts)


def _kv_up(c_all, kr_all, wk, ek, wvt, ones_col, tm):
    b, n, _ = c_all.shape
    consts = (wk, ek, wvt, ones_col)
    return pl.pallas_call(
        _kv_up_kernel,
        grid=(b, n // tm),
        in_specs=[_tile_spec(tm, A_KVRANK), _tile_spec(tm, A_ROPE)] + [_const_spec(a) for a in consts],
        out_specs=[_tile_spec(tm, 512), pl.BlockSpec((1, 1, N_HEADS * V_ROWS, tm), lambda bb, j: (bb, j, 0, 0))],
        out_shape=[jax.ShapeDtypeStruct((b, n, 512), BF16),
                   jax.ShapeDtypeStruct((b, n // tm, N_HEADS * V_ROWS, tm), BF16)],
        compiler_params=_params(), name="kv_up",
    )(c_all, kr_all, *consts)


def _flash(qt, k, vt, gout_col, tq, tk, causal, klo, khi):
    b, _, l = qt.shape
    n = k.shape[1]
    nkv = n // tk
    return pl.pallas_call(
        functools.partial(_flash_kernel, tq=tq, tk=tk, nkv=nkv, causal=causal, klo=klo, khi=khi),
        grid=(b, l // tq),
        in_specs=[_cols_spec(512, tq),
                  pl.BlockSpec((1, n, 512), lambda bb, j: (bb, 0, 0), pipeline_mode=pl.Buffered(1)),
                  pl.BlockSpec((1, nkv, N_HEADS * V_ROWS, tk), lambda bb, j: (bb, 0, 0, 0), pipeline_mode=pl.Buffered(1)),
                  _const_spec(gout_col)],
        out_specs=_cols_spec(GROUP_W, tq),
        out_shape=jax.ShapeDtypeStruct((b, GROUP_W, l), F32),
        scratch_shapes=[pltpu.VMEM((N_HEADS * SUBLANE, tq), F32), pltpu.VMEM((N_HEADS * V_ROWS, tq), F32),
                        pltpu.VMEM((N_HEADS * tk, tq), F32), pltpu.VMEM((N_HEADS * tk, tq), F32)],
        compiler_params=_params(), name="mla_flash",
    )(qt, k, vt, gout_col)


def _stack_consts(c):
    n = N_HEADS * c
    lc = int(math.log2(c))
    r, l = np.arange(n)[:, None], np.arange(GROUP_W)[None, :]
    hm = ((r >> lc) == (l >> 6)).astype(np.float32)
    rr, cc = np.arange(n)[:, None], np.arange(n)[None, :]
    same = (rr >> lc) == (cc >> lc)
    tri = [same & (cc <= rr), same & (cc < rr)]
    join = [((rr >> (lb + 1)) == (cc >> (lb + 1))) & (((rr >> lb) & 1) == 1) & (((cc >> lb) & 1) == 0)
            for lb in range(lc)] + [rr == cc]
    tl = np.arange(c)[None, :] <= np.arange(c)[:, None]
    hh = np.arange(GROUP_W)
    bd = (hh[:, None] >> 6) == (hh[None, :] >> 6)
    return (jnp.asarray(hm), jnp.asarray(np.stack(tri).astype(np.float32)),
            jnp.asarray(np.stack(join), BF16), jnp.asarray(tl, BF16), jnp.asarray(bd, BF16))


def _scan_call(kernel, name, x, consts_a, hist, s0, consts_b, width_in, tm, c, front, grp):
    b, l, _ = x.shape
    assert b % grp == 0
    consts_b = tuple(consts_b) + _stack_consts(c)
    rows_spec = lambda rows, width: pl.BlockSpec((grp, rows, width), lambda bb, j: (bb, 0, 0))
    return pl.pallas_call(
        functools.partial(kernel, tm=tm, c=c, front=front, grp=grp),
        grid=(b // grp, l // tm),
        in_specs=[pl.BlockSpec((grp, tm, D_MODEL), lambda bb, j: (bb, j, 0))]
        + [_const_spec(a) for a in consts_a]
        + [rows_spec(SUBLANE, width_in), rows_spec(GROUP_W, GROUP_W)]
        + [_const_spec(a) for a in consts_b],
        out_specs=[pl.BlockSpec((grp, tm, GROUP_W), lambda bb, j: (bb, j, 0)),
                   rows_spec(GROUP_W, GROUP_W), rows_spec(SUBLANE, width_in)],
        out_shape=[jax.ShapeDtypeStruct((b, l, GROUP_W), F32),
                   jax.ShapeDtypeStruct((b, GROUP_W, GROUP_W), F32),
                   jax.ShapeDtypeStruct((b, SUBLANE, width_in), F32)],
        scratch_shapes=[pltpu.VMEM((grp, tm + SUBLANE, width_in), F32),
                        pltpu.VMEM((grp, GROUP_W, GROUP_W), F32)],
        compiler_params=_params(), name=name,
    )(x, *consts_a, hist, s0, *consts_b)


def _ffn(x, ya, yb, yc, yd, wo, g2, wup, cw, wdn, hist, tm, front):
    b, l, _ = x.shape
    return pl.pallas_call(
        functools.partial(_ffn_kernel, tm=tm, front=front),
        grid=(b, l // tm),
        in_specs=[_tile_spec(tm, D_MODEL), _cols_spec(GROUP_W, tm)] + [_tile_spec(tm, GROUP_W)] * 3
        + [_const_spec(a) for a in (wo, g2, wup, cw, wdn)] + [_batch_spec(SUBLANE, 2 * D_FF)],
        out_specs=[_tile_spec(tm, D_MODEL), _batch_spec(SUBLANE, 2 * D_FF)],
        out_shape=[jax.ShapeDtypeStruct((b, l, D_MODEL), F32),
                   jax.ShapeDtypeStruct((b, SUBLANE, 2 * D_FF), F32)],
        scratch_shapes=[pltpu.VMEM((SUBLANE, 2 * D_FF), F32),
                        pltpu.VMEM((tm + SUBLANE, 2 * FFN_CW), F32),
                        pltpu.VMEM((tm + SUBLANE, 2 * FFN_CW), F32)],
        compiler_params=_params(), name="out_ffn",
    )(x, ya, yb, yc, yd, wo, g2, wup, cw, wdn, hist)


def _final_norm(x, g, tm, skip_tiles, out_rows):
    b = x.shape[0]
    return pl.pallas_call(
        _final_norm_kernel,
        grid=(b, out_rows // tm),
        in_specs=[pl.BlockSpec((1, tm, D_MODEL), lambda bb, j: (bb, j + skip_tiles, 0)), _const_spec(g)],
        out_specs=_tile_spec(tm, D_MODEL),
        out_shape=jax.ShapeDtypeStruct((b, out_rows, D_MODEL), F32),
        compiler_params=_params(), name="final_norm",
    )(x, g)


def _np_idx():
    z = IN_COLS
    zpad = lambda n: [z] * n
    rep = lambda base: [base + i for i in range(N_HEADS) for _ in range(HEAD_W)]
    grp = lambda base: [base + g * HEAD_W + i for g in (0, 0, 1, 1) for i in range(HEAD_W)]
    a = (list(range(0, 192)) + zpad(64) + list(range(192, 320))
         + list(range(320, 352)) + zpad(96)
         + list(range(336, 352)) + list(range(320, 336)) + zpad(96))
    b0 = A_COLS
    bcols = list(range(b0, b0 + B_COLS))
    c0 = b0 + B_COLS
    ccols = (list(range(c0, c0 + 256)) + list(range(c0 + 256, c0 + 512))
             + grp(c0 + 512) + grp(c0 + 640) + rep(c0 + 768))
    d0 = c0 + C_COLS
    dcols = list(range(d0, d0 + 1024)) + rep(d0 + 1024) + rep(d0 + 1028)
    xbc_exp = list(range(256)) + grp(256) + grp(384)
    xbc_back = (list(range(256)) + list(range(256, 320)) + list(range(384, 448))
                + list(range(512, 576)) + list(range(640, 704)))
    ffn_perm = []
    for f in range(D_FF // FFN_CW):
        ffn_perm += list(range(f * FFN_CW, (f + 1) * FFN_CW))
        ffn_perm += list(range(D_FF + f * FFN_CW, D_FF + (f + 1) * FFN_CW))
    ffn_back = np.argsort(np.array(ffn_perm))
    as_i = lambda v: np.asarray(v, np.int32)
    return dict(a=as_i(a), b=as_i(bcols), c=as_i(ccols), d=as_i(dcols), xbc_exp=as_i(xbc_exp),
                xbc_back=as_i(xbc_back), ffn_perm=as_i(ffn_perm), ffn_back=as_i(ffn_back))


_IDX = _np_idx()


def _rep_heads(v):
    return jnp.repeat(v, HEAD_W, axis=-1)


def _pad_rows(w, rows=SUBLANE):
    return jnp.pad(w, [(0, rows - w.shape[0])] + [(0, 0)] * (w.ndim - 1))


def _layer_consts(P, l):
    row = lambda v: v.reshape(1, -1).astype(F32)
    w_in = jnp.concatenate([P['w_in'][l], jnp.zeros((D_MODEL, 1), F32)], axis=1)
    c = {}
    c['g1'] = row(P['norm1_g'][l])
    c['wa'] = w_in[:, _IDX['a']].astype(BF16)
    c['wb'] = w_in[:, _IDX['b']].astype(BF16)
    c['wc'] = w_in[:, _IDX['c']].astype(BF16)
    c['wd'] = w_in[:, _IDX['d']].astype(BF16)
    c['gq'] = row(jnp.pad(P['a_gq'][l], (0, 64)))
    wuq = P['a_wuq'][l].reshape(A_QRANK, N_HEADS, A_NOPE + A_ROPE)
    rope = wuq[:, :, A_NOPE:]
    swap = jnp.concatenate([rope[..., 16:], rope[..., :16]], axis=-1)
    zeros = lambda n: jnp.zeros((A_QRANK, N_HEADS, n), F32)
    wq1 = jnp.concatenate([wuq, zeros(32)], axis=-1).reshape(A_QRANK, 512)
    wq2 = jnp.concatenate([zeros(64), swap, zeros(32)], axis=-1).reshape(A_QRANK, 512)
    c['wq1t'] = jnp.pad(wq1, ((0, 64), (0, 0))).T.astype(BF16)
    c['wq2t'] = jnp.pad(wq2, ((0, 64), (0, 0))).T.astype(BF16)
    c['gkv'] = row(P['a_gkv'][l])
    wuk = P['a_wuk'][l].reshape(A_KVRANK, N_HEADS, A_NOPE)
    c['wk'] = jnp.concatenate([wuk, jnp.zeros((A_KVRANK, N_HEADS, 64), F32)], axis=-1
                              ).reshape(A_KVRANK, 512).astype(BF16)
    ek = np.zeros((A_ROPE, N_HEADS, LANE), np.float32)
    for hh in range(N_HEADS):
        ek[np.arange(A_ROPE), hh, A_NOPE + np.arange(A_ROPE)] = 1.0
    c['ek'] = jnp.asarray(ek.reshape(A_ROPE, 512), BF16)
    c['ek128'] = jnp.pad(c['ek'], ((0, LANE - A_ROPE), (0, 0)))
    wuv = P['a_wuv'][l].reshape(A_KVRANK, N_HEADS, HEAD_W)
    c['wvt'] = jnp.concatenate([wuv, jnp.zeros((A_KVRANK, N_HEADS, V_ROWS - HEAD_W), F32)], axis=-1
                               ).reshape(A_KVRANK, N_HEADS * V_ROWS).T.astype(BF16)
    ones = np.zeros((N_HEADS, V_ROWS, 1), np.float32)
    ones[:, HEAD_W, 0] = 1.0
    c['ones_col'] = jnp.asarray(ones.reshape(N_HEADS * V_ROWS, 1))
    c['gout_col'] = P['a_gout'][l].reshape(GROUP_W, 1).astype(F32)
    c['mu'] = row(P['b_mu'][l])
    c['w0'] = row(P['b_w0'][l])
    z64 = jnp.zeros((64, GROUP_W), F32)
    c['wl'] = jnp.concatenate([jnp.concatenate([P['b_w2'][l], z64], axis=1),
                               jnp.concatenate([z64, P['b_a2'][l]], axis=1)], axis=0).astype(BF16)
    c['a0'] = row(P['b_a0'][l])
    c['g2b'] = P['b_g2'][l].astype(BF16)
    c['kk'] = row(P['b_kk'][l])
    c['ka'] = row(P['b_ka'][l])
    c['rk'] = row(P['b_rk'][l])
    c['gnw'] = row(P['b_gnw'][l])
    c['gnb'] = row(P['b_gnb'][l])
    c['c_cw'] = _pad_rows(P['c_convw'][l][:, _IDX['xbc_exp']])
    c['c_cb'] = row(P['c_convb'][l][_IDX['xbc_exp']])
    c['c_dtb'] = row(_rep_heads(P['c_dtb'][l]))
    c['c_alog'] = row(_rep_heads(P['c_alog'][l]))
    c['c_d'] = row(_rep_heads(P['c_d'][l]))
    c['c_gn'] = row(P['c_gnorm'][l])
    c['d_cw'] = _pad_rows(P['d_convw'][l])
    c['d_alog'] = row(_rep_heads(P['d_alog'][l]))
    c['d_dtb'] = row(_rep_heads(P['d_dtb'][l]))
    c['d_gn'] = row(jnp.tile(P['d_gnorm'][l], N_HEADS))
    c['wo'] = P['w_out'][l].astype(BF16)
    c['g2'] = row(P['norm2_g'][l])
    c['wup'] = P['f_wup'][l][:, _IDX['ffn_perm']].astype(BF16)
    c['f_cw'] = _pad_rows(P['f_convw'][l][:, _IDX['ffn_perm']])
    c['wdn'] = P['f_wdown'][l].astype(BF16)
    return c


def _embed_bd(s):
    b = s.shape[0]
    eye = jnp.eye(N_HEADS, dtype=s.dtype)
    return jnp.einsum('bhij,hg->bhigj', s, eye).reshape(b, GROUP_W, GROUP_W)


def _extract_bd(s):
    b = s.shape[0]
    s5 = s.reshape(b, N_HEADS, HEAD_W, N_HEADS, HEAD_W)
    return jnp.stack([s5[:, hh, :, hh, :] for hh in range(N_HEADS)], axis=1)


def _hist8(hist):
    return jnp.pad(hist, ((0, 0), (SUBLANE - hist.shape[1], 0), (0, 0)))


def _rope_table(pos):
    half = A_ROPE // 2
    inv = jnp.power(ROPE_BASE, -jnp.arange(half, dtype=F32) / half)
    ang = pos.astype(F32)[:, None] * inv
    cos, sin = jnp.cos(ang), jnp.sin(ang)
    cos2 = jnp.concatenate([cos, cos], axis=-1)
    sin2 = jnp.concatenate([-sin, sin], axis=-1)
    n = pos.shape[0]
    one = jnp.ones((n, A_NOPE), F32)
    z = lambda w: jnp.zeros((n, w), F32)
    tabq = jnp.concatenate([one, cos2, z(32), z(64), sin2, z(32)], axis=-1).T
    tabk = jnp.concatenate([cos2, z(96), sin2, z(96)], axis=-1)
    return tabq, tabk


def _trunk(x, pos, front, st, P, *, tm_scan, tm_ffn, c, tq, causal, n_keys_pad, scan_rows):
    b, l, _ = x.shape
    tabq, tabk = _rope_table(pos)
    new = {name: [] for name in ('ckv', 'krope', 'rwkv_S', 'rwkv_shift', 'ssd_S', 'ssd_conv',
                                 'gdn_S', 'gdn_conv', 'ffn_conv')}
    zeros_bd = jnp.zeros((b, GROUP_W, GROUP_W), F32)
    for li in range(DEPTH):
        c_ = P[li]
        prep = (x, c_['g1'], c_['wa'], c_['gq'], c_['wq1t'], c_['wq2t'], c_['gkv'], tabq, tabk, tm_ffn, front)
        if st is None:
            qt, ckv, krope, kf, vt = _mla_prep(
                *prep, kv_consts=(c_['wk'], c_['ek128'], c_['wvt'], c_['ones_col']), tk=tq)
            tkv, klo, khi = tq, front, l
            s_b = s_c = s_d = zeros_bd
            shift8 = jnp.zeros((b, SUBLANE, B_COLS), F32)
            chist = jnp.zeros((b, SUBLANE, 768), F32)
            dhist = jnp.zeros((b, SUBLANE, 768), F32)
            fhist = jnp.zeros((b, SUBLANE, 2 * D_FF), F32)
        else:
            qt, ckv, krope = _mla_prep(*prep)
            past = st['ckv'].shape[2]
            padk = n_keys_pad - past - l
            c_all = jnp.concatenate([st['ckv'][li], ckv, jnp.zeros((b, padk, A_KVRANK), F32)], axis=1)
            kr_all = jnp.concatenate([st['krope'][li], krope, jnp.zeros((b, padk, A_ROPE), F32)], axis=1)
            tkv, klo, khi = n_keys_pad, 0, past + l
            kf, vt = _kv_up(c_all, kr_all, c_['wk'], c_['ek'], c_['wvt'], c_['ones_col'], tkv)
            s_b = _embed_bd(st['rwkv_S'][li])
            s_c = _embed_bd(jnp.swapaxes(st['ssd_S'][li], -1, -2))
            s_d = _embed_bd(st['gdn_S'][li])
            shift8 = _hist8(st['rwkv_shift'][li][:, None, :])
            chist = _hist8(st['ssd_conv'][li][:, :, _IDX['xbc_exp']])
            dhist = _hist8(st['gdn_conv'][li])
            fhist = _hist8(st['ffn_conv'][li][:, :, _IDX['ffn_perm']])
        ya = _flash(qt, kf, vt, c_['gout_col'], tq, tkv, causal, klo, khi)
        yb, sb_new, shift_new = _scan_call(
            _rwkv_kernel, "rwkv7", x, (c_['g1'], c_['wb']), shift8, s_b,
            (c_['mu'], c_['w0'], c_['wl'], c_['a0'], c_['g2b'], c_['kk'], c_['ka'], c_['rk'],
             c_['gnw'], c_['gnb']), B_COLS, tm_scan, c, front, scan_rows)
        yc, sc_new, chist_new = _scan_call(
            _ssd_kernel, "ssd", x, (c_['g1'], c_['wc']), chist, s_c,
            (c_['c_cw'], c_['c_cb'], c_['c_dtb'], c_['c_alog'], c_['c_d'], c_['c_gn']),
            768, tm_scan, c, front, scan_rows)
        yd, sd_new, dhist_new = _scan_call(
            _gdn_kernel, "gdn", x, (c_['g1'], c_['wd']), dhist, s_d,
            (c_['d_cw'], c_['d_alog'], c_['d_dtb'], c_['d_gn']), 768, tm_scan, c, front, scan_rows)
        x, fhist_new = _ffn(x, ya, yb, yc, yd, c_['wo'], c_['g2'], c_['wup'], c_['f_cw'], c_['wdn'],
                            fhist, tm_ffn, front)
        new['ckv'].append(ckv[:, front:])
        new['krope'].append(krope[:, front:])
        new['rwkv_S'].append(_extract_bd(sb_new))
        new['rwkv_shift'].append(shift_new[:, SUBLANE - 1])
        new['ssd_S'].append(jnp.swapaxes(_extract_bd(sc_new), -1, -2))
        new['ssd_conv'].append(chist_new[:, SUBLANE - (C_CONV - 1):][:, :, _IDX['xbc_back']])
        new['gdn_S'].append(_extract_bd(sd_new))
        new['gdn_conv'].append(dhist_new[:, SUBLANE - (D_CONV - 1):])
        new['ffn_conv'].append(fhist_new[:, SUBLANE - (FFN_CONV - 1):][:, :, _IDX['ffn_back']])
    return x, {name: jnp.stack(vals) for name, vals in new.items()}


def kernel(x_prompt, x_sample, cache_mla_ckv, cache_mla_krope, state_rwkv, state_rwkv_shift, state_ssd, state_ssd_conv, state_gdn, state_gdn_conv, state_ffn_conv, meta_tokens, norm1_g, w_in, a_gq, a_wuq, a_gkv, a_wuk, a_wuv, a_gout, b_mu, b_w0, b_w2, b_a0, b_a2, b_g2, b_kk, b_ka, b_rk, b_gnw, b_gnb, c_convw, c_convb, c_dtb, c_alog, c_d, c_gnorm, d_convw, d_alog, d_dtb, d_gnorm, w_out, norm2_g, f_wup, f_convw, f_wdown, final_g):
    P = dict(norm1_g=norm1_g, w_in=w_in, a_gq=a_gq, a_wuq=a_wuq, a_gkv=a_gkv, a_wuk=a_wuk,
             a_wuv=a_wuv, a_gout=a_gout, b_mu=b_mu, b_w0=b_w0, b_w2=b_w2, b_a0=b_a0, b_a2=b_a2,
             b_g2=b_g2, b_kk=b_kk, b_ka=b_ka, b_rk=b_rk, b_gnw=b_gnw, b_gnb=b_gnb,
             c_convw=c_convw, c_convb=c_convb, c_dtb=c_dtb, c_alog=c_alog, c_d=c_d, c_gnorm=c_gnorm,
             d_convw=d_convw, d_alog=d_alog, d_dtb=d_dtb, d_gnorm=d_gnorm,
             w_out=w_out, norm2_g=norm2_g, f_wup=f_wup, f_convw=f_convw, f_wdown=f_wdown)
    P = [_layer_consts(P, li) for li in range(DEPTH)]
    fin = final_g.reshape(1, -1).astype(F32)
    b_p, seq, _ = x_prompt.shape
    lx = N_META + seq
    assert seq % ATT_BLOCK == 0 and N_META <= ATT_BLOCK
    front = ATT_BLOCK - N_META
    l_pad = front + lx
    meta = jnp.broadcast_to(meta_tokens[None].astype(F32), (b_p, N_META, D_MODEL))
    x_ext = jnp.concatenate([jnp.zeros((b_p, front, D_MODEL), F32), meta, x_prompt], axis=1)
    pos_p = jnp.arange(l_pad, dtype=jnp.int32) - (front + N_META)
    tm_ffn = max(t for t in (ROW_TILE, 2 * ROW_TILE, 3 * ROW_TILE) if l_pad % t == 0)
    y_p, ns_p = _trunk(x_ext, pos_p, front, None, P, tm_scan=SCAN_TILE, tm_ffn=tm_ffn, c=CHUNK, tq=ATT_BLOCK,
                       causal=True, n_keys_pad=l_pad, scan_rows=SCAN_ROWS)
    y_prompt = _final_norm(y_p, fin, ROW_TILE, (front + N_META) // ROW_TILE, seq)
    b_s, t_s, _ = x_sample.shape
    past = cache_mla_ckv.shape[2]
    assert t_s <= CHUNK and t_s % 16 == 0 and (t_s & (t_s - 1)) == 0
    st_s = dict(ckv=cache_mla_ckv, krope=cache_mla_krope, rwkv_S=state_rwkv, rwkv_shift=state_rwkv_shift,
                ssd_S=state_ssd, ssd_conv=state_ssd_conv, gdn_S=state_gdn, gdn_conv=state_gdn_conv,
                ffn_conv=state_ffn_conv)
    pos_s = past + jnp.arange(t_s, dtype=jnp.int32)
    n_keys_pad = -(-(past + t_s) // LANE) * LANE
    y_s, ns_s = _trunk(x_sample, pos_s, 0, st_s, P, tm_scan=t_s, tm_ffn=t_s, c=t_s, tq=t_s, causal=False,
                       n_keys_pad=n_keys_pad, scan_rows=math.gcd(b_s, SCAN_ROWS_SHORT))
    y_sample = _final_norm(y_s, fin, t_s, 0, t_s)
    keys = ('ckv', 'krope', 'rwkv_S', 'rwkv_shift', 'ssd_S', 'ssd_conv', 'gdn_S', 'gdn_conv', 'ffn_conv')
    return (y_prompt, y_sample) + tuple(ns_p[k] for k in keys) + tuple(ns_s[k] for k in keys)
```

```python
import functools
import math

import numpy as np
import jax
import jax.numpy as jnp
from jax import lax
from jax.experimental import pallas as pl
from jax.experimental.pallas import tpu as pltpu

F32 = jnp.float32
BF16 = jnp.bfloat16

D_MODEL = 1024
DEPTH = 4
CHUNK = 64
N_META = 16
EPS = 1e-6
L2_EPS = 1e-6
GROUP_W = 256
N_HEADS = 4
HEAD_W = 64
A_NOPE = 64
A_ROPE = 32
A_QRANK = 192
A_KVRANK = 128
A_SCALE = (A_NOPE + A_ROPE) ** -0.5
ROPE_BASE = 10000.0
B_GN_EPS = 64e-5
B_COLS = 1024
C_CONV = 4
D_CONV = 4
D_FF = 2816
FFN_CONV = 3
A_COLS = 352
C_COLS = 772
D_COLS = 1032
IN_COLS = A_COLS + B_COLS + C_COLS + D_COLS

LANE = 128
SUBLANE = 8
VMEM_LIMIT = 56 * 1024 * 1024
NEG = -1e30
LOG2E = math.log2(math.e)
V_ROWS = 80
FFN_CW = 256
FFN_DOWN_GROUP = 6
ATT_BLOCK = 256
ROW_TILE = 256
SCAN_TILE = 128
SCAN_ROWS = 4
SCAN_ROWS_SHORT = 8


def _mm(a, b):
    return jnp.dot(a.astype(BF16), b.astype(BF16), preferred_element_type=F32)


def _mm_nt(a, b):
    return lax.dot_general(a.astype(BF16), b.astype(BF16), (((1,), (1,)), ((), ())),
                           preferred_element_type=F32)


def _mm_tn(a, b):
    return lax.dot_general(a.astype(BF16), b.astype(BF16), (((0,), (0,)), ((), ())),
                           preferred_element_type=F32)


def _split2(x):
    hi = x.astype(BF16)
    lo = (x - hi.astype(F32)).astype(BF16)
    return hi, lo


def _mm_x2(x, w):
    hi, lo = _split2(x)
    return (jnp.dot(hi, w, preferred_element_type=F32)
            + jnp.dot(lo, w, preferred_element_type=F32))


def _rms(x, g, n):
    ms = jnp.sum(x * x, axis=-1, keepdims=True) * (1.0 / n)
    return x * lax.rsqrt(ms + EPS) * g


def _softplus(x):
    return jnp.maximum(x, 0.0) + jnp.log1p(jnp.exp(-jnp.abs(x)))


def _sigmoid(x):
    return 1.0 / (1.0 + jnp.exp(-x))


def _silu(x):
    return x * _sigmoid(x)


def _iota(shape, axis):
    return lax.broadcasted_iota(jnp.int32, shape, axis)


def _norm_in(x_ref, g_ref, j, tm, front):
    h = _rms(x_ref[0], g_ref[...], D_MODEL)
    if front > 0:
        rows = j * tm + _iota((tm, 1), 0)
        h = jnp.where(rows >= front, h, 0.0)
    return h.astype(BF16)


def _norm_in_rows(x_ref, g_ref, j, tm, front):
    x = x_ref[...]
    h = _rms(x, g_ref[...], D_MODEL)
    if front > 0:
        pos = j * tm + _iota((1, tm, 1), 1)
        h = jnp.where(pos >= front, h, 0.0)
    return h.reshape(x.shape[0] * tm, D_MODEL).astype(BF16)


def _hsum(x, bd):
    return _mm_x2(x, bd)


TRI_INCL, TRI_STRICT = 0, 1


class _Stk:
    def __init__(self, c, hm_ref, tri_ref, join_ref, tl_ref):
        self.c = c
        self.n = N_HEADS * c
        self.steps = int(math.log2(c))
        assert 1 << self.steps == c
        self.hm_ref, self.tri_ref, self.join_ref, self.tl_ref = hm_ref, tri_ref, join_ref, tl_ref
        self.hm_b = None

    def mask(self, which):
        return self.tri_ref[which]

    def tile(self, x):
        return jnp.concatenate([x] * N_HEADS, axis=0)

    def stack(self, x):
        return self.tile(x) * self.hm_ref[...]

    def unstack(self, xs):
        c = self.c
        return xs[0:c] + xs[c:2 * c] + xs[2 * c:3 * c] + xs[3 * c:4 * c]

    def cumsum(self, x):
        hi, lo = _split2(x)
        tl = self.tl_ref[...]
        return jnp.dot(tl, hi, preferred_element_type=F32) + jnp.dot(tl, lo, preferred_element_type=F32)

    def decay(self, g_cum):
        gcol = jnp.min(self.stack(g_cum), axis=-1, keepdims=True)
        grow = jnp.broadcast_to(gcol, (self.n, LANE)).T[0:1, :]
        return jnp.exp(jnp.minimum(gcol - grow, 0.0)) * self.mask(TRI_INCL)

    def tri_inv_many(self, lowers):
        lowers = [lw.astype(BF16) for lw in lowers]
        minvs = [self.join_ref[self.steps] - lw * self.join_ref[0] for lw in lowers]
        for lb in range(1, self.steps):
            ts = [_mm(m, lw * self.join_ref[lb]) for m, lw in zip(minvs, lowers)]
            minvs = [m - _mm(t, m).astype(BF16) for t, m in zip(ts, minvs)]
        return minvs

    def stack_b(self, x):
        if self.hm_b is None:
            self.hm_b = self.hm_ref[...].astype(BF16)
        return self.tile(x.astype(BF16)) * self.hm_b


def _mla_prep_kernel(x_ref, g1_ref, wa_ref, gq_ref, wq1t_ref, wq2t_ref, gkv_ref, tabq_ref, tabk_ref,
                     *rest, tm, front, tk):
    if tk:
        wk_ref, ek_ref, wvt_ref, ones_ref, qt_out, ckv_out, kr_out, k_out, vt_out = rest
    else:
        qt_out, ckv_out, kr_out = rest
    j = pl.program_id(1)
    h = _norm_in(x_ref, g1_ref, j, tm, front)
    pa = jnp.dot(h, wa_ref[...], preferred_element_type=F32)
    qn = _rms(pa[:, 0:256], gq_ref[...], A_QRANK)
    q1t = _mm_nt(wq1t_ref[...], qn)
    q2t = _mm_nt(wq2t_ref[...], qn)
    tabq = tabq_ref[...]
    cos4 = jnp.concatenate([tabq[0:128]] * N_HEADS, axis=0)
    sin4 = jnp.concatenate([tabq[128:256]] * N_HEADS, axis=0)
    qt_out[0] = ((q1t * cos4 + q2t * sin4) * (A_SCALE * LOG2E)).astype(BF16)
    c = _rms(pa[:, 256:384], gkv_ref[...], A_KVRANK)
    ckv_out[0] = c
    tabk = tabk_ref[...]
    kr = pa[:, 384:512] * tabk[:, 0:128] + pa[:, 512:640] * tabk[:, 128:256]
    kr_out[0] = kr[:, 0:A_ROPE]
    if tk:
        cb = c.astype(BF16)
        k_out[0] = (jnp.dot(cb, wk_ref[...], preferred_element_type=F32)
                    + jnp.dot(kr.astype(BF16), ek_ref[...], preferred_element_type=F32)).astype(BF16)
        vt = (_mm_nt(wvt_ref[...], cb) + ones_ref[...]).astype(BF16)
        for t in range(tm // tk):
            vt_out[0, t] = vt[:, t * tk:(t + 1) * tk]


def _kv_up_kernel(c_ref, kr_ref, wk_ref, ek_ref, wvt_ref, ones_ref, k_out, vt_out):
    c = c_ref[0].astype(BF16)
    kr = kr_ref[0].astype(BF16)
    k = (jnp.dot(c, wk_ref[...], preferred_element_type=F32)
         + jnp.dot(kr, ek_ref[...], preferred_element_type=F32))
    k_out[0] = k.astype(BF16)
    vt_out[0, 0] = (_mm_nt(wvt_ref[...], c) + ones_ref[...]).astype(BF16)


def _flash_kernel(qt_ref, k_ref, vt_ref, gout_ref, o_ref, m_sc, acc_sc, sa_sc, sb_sc,
                  *, tq, tk, nkv, causal, klo, khi):
    i = pl.program_id(1)
    m_sc[...] = jnp.full((N_HEADS * SUBLANE, tq), NEG, F32)
    acc_sc[...] = jnp.zeros((N_HEADS * V_ROWS, tq), F32)
    vrows = [slice(V_ROWS * h, V_ROWS * (h + 1)) for h in range(N_HEADS)]
    heads = [slice(LANE * h, LANE * (h + 1)) for h in range(N_HEADS)]
    s_bufs = (sa_sc, sb_sc)

    def produce(jb, slot):
        start = pl.multiple_of(jb * tk, tk)
        for h, rows in enumerate(heads):
            s_bufs[slot][h * tk:(h + 1) * tk, :] = jnp.dot(
                k_ref[0, pl.ds(start, tk), rows], qt_ref[0, rows, :], preferred_element_type=F32)

    def consume(jb, slot, masked, nxt):
        if nxt is not None:
            produce(*nxt)
        if masked:
            kpos = jb * tk + _iota((tk, 1), 0)
            qpos = i * tq + _iota((1, tq), 1)
            vis = (kpos >= klo) & (kpos < khi)
            if causal:
                vis = vis & ((kpos >> 6) <= (qpos >> 6))
        m_all = m_sc[...]
        acc_all = acc_sc[...]
        m_out, alphas, ps = [], [], []
        for h in range(N_HEADS):
            s = s_bufs[slot][h * tk:(h + 1) * tk, :]
            if masked:
                s = jnp.where(vis, s, NEG)
            m_prev = m_all[SUBLANE * h:SUBLANE * (h + 1)]
            m_new = jnp.maximum(m_prev, jnp.max(s, axis=0, keepdims=True))
            alphas.append(jnp.exp2(m_prev[0:1] - m_new[0:1]))
            ps.append(jnp.exp2(s - m_new[0:1]).astype(BF16))
            m_out.append(m_new)
        acc_out = [alphas[h] * acc_all[rows]
                   + jnp.dot(vt_ref[0, jb, rows, :], ps[h], preferred_element_type=F32)
                   for h, rows in enumerate(vrows)]
        m_sc[...] = jnp.concatenate(m_out, axis=0)
        acc_sc[...] = jnp.concatenate(acc_out, axis=0)

    produce(0, 0)
    if causal:
        @pl.when(i == 0)
        def _():
            consume(0, 0, True, None)

        @pl.when(i > 0)
        def _():
            consume(0, 0, True, (1, 1))
            quads = (i - 1) >> 2

            def pair(jb):
                consume(jb, 1, False, (jb + 1, 0))
                consume(jb + 1, 0, False, (jb + 2, 1))

            def body(t, carry):
                pair(1 + 4 * t)
                pair(3 + 4 * t)
                return carry

            lax.fori_loop(0, quads, body, 0)

            @pl.when(((i - 1) & 2) != 0)
            def _():
                pair(1 + 4 * quads)

            @pl.when((i & 1) == 0)
            def _():
                consume(i - 1, 1, False, (i, 0))
                consume(i, 0, True, None)

            @pl.when((i & 1) == 1)
            def _():
                consume(i, 1, True, None)
    else:
        for jb in range(nkv):
            consume(jb, jb & 1, True, (jb + 1, (jb + 1) & 1) if jb + 1 < nkv else None)
    outs = []
    for h in range(N_HEADS):
        a = acc_sc[V_ROWS * h:V_ROWS * (h + 1), :]
        outs.append(a[0:HEAD_W] / a[HEAD_W:HEAD_W + 1])
    yat = jnp.concatenate(outs, axis=0)
    ms = jnp.sum(yat * yat, axis=0, keepdims=True) * (1.0 / GROUP_W)
    o_ref[0] = yat * lax.rsqrt(ms + EPS) * gout_ref[...]


def _chunk_ids(grp, tm, c):
    ids = [(g, ci) for ci in range(tm // c) for g in range(grp)]
    return ids, {(g, ci): slice(g * tm + ci * c, g * tm + (ci + 1) * c) for g, ci in ids}


def _shift_rows(u, tail8, d):
    r = pltpu.roll(u, d, axis=0)
    row8 = _iota((SUBLANE, 1), 0)
    head = r[0:SUBLANE]
    for k in range(d):
        head = jnp.where(row8 == k, tail8[SUBLANE - d + k:SUBLANE - d + k + 1], head)
    return jnp.concatenate([head, r[SUBLANE:]], axis=0)


def _rwkv_kernel(x_ref, g1_ref, wb_ref, shift_ref, s0_ref, mu_ref, w0_ref, wl_ref, a0_ref,
                 g2_ref, kk_ref, ka_ref, rk_ref, gnw_ref, gnb_ref, hm_ref, tri_ref, join_ref, tl_ref, bd_ref,
                 y_out, s_out, shift_out, work, st, *, tm, c, front, grp):
    j = pl.program_id(1)

    @pl.when(j == 0)
    def _():
        work[...] = shift_ref[...]
        st[...] = s0_ref[...]

    h = _norm_in_rows(x_ref, g1_ref, j, tm, front)
    cols = jnp.dot(h, wb_ref[...], preferred_element_type=F32)
    shifted = []
    for g in range(grp):
        u = cols[g * tm:(g + 1) * tm]
        shifted.append(_shift_rows(u, work[g], 1))
        tail = u[tm - SUBLANE:tm]
        work[g] = tail
        shift_out[g] = tail
    shifted = jnp.concatenate(shifted, axis=0)
    xm = cols + (shifted - cols) * mu_ref[...]
    r = xm[:, 0:256]
    k = xm[:, 256:512]
    v = xm[:, 512:768]
    lora = xm[:, 768:896]
    dg = xm[:, 896:1024]
    lora = jnp.where(_iota((grp * tm, LANE), 1) < 64, jnp.tanh(lora), lora)
    ll = _mm(lora, wl_ref[...])
    w_log = -_softplus(-(w0_ref[...] + ll[:, 0:256])) - 0.5
    logd = -jnp.exp(w_log)
    a = _sigmoid(a0_ref[...] + ll[:, 256:512])
    g_gate = _mm(_sigmoid(dg), g2_ref[...])
    bd = bd_ref[...]
    kkr = k * kk_ref[...]
    kk = kkr * lax.rsqrt(_hsum(kkr * kkr, bd) + L2_EPS)
    k2 = k * (1.0 + (a - 1.0) * ka_ref[...])

    sk = _Stk(c, hm_ref, tri_ref, join_ref, tl_ref)
    ids, rows = _chunk_ids(grp, tm, c)
    strict, incl = sk.mask(TRI_STRICT), sk.mask(TRI_INCL)
    gcs = {i: sk.cumsum(logd[rows[i]]) for i in ids}
    pre = {}
    for i in ids:
        gc, ld = gcs[i], logd[rows[i]]
        eg, eng = jnp.exp(gc), jnp.exp(-gc)
        bt = kk[rows[i]] * a[rows[i]] * eng
        kt = k2[rows[i]] * eng
        dc = eg[c - 1:c, :]
        pre[i] = dict(at_s=sk.stack_b(-kk[rows[i]] * jnp.exp(gc - ld)), rt_s=sk.stack_b(r[rows[i]] * eg),
                      v_s=sk.stack_b(v[rows[i]]), bt_t=sk.tile(bt.astype(BF16)),
                      kt_t=sk.tile(kt.astype(BF16)), dc=dc,
                      bd_s=sk.stack_b(bt * dc), kd_s=sk.stack_b(kt * dc))
    lab = {i: _mm_nt(pre[i]['at_s'], pre[i]['bt_t']) * strict for i in ids}
    aak = {i: (_mm_nt(pre[i]['at_s'], pre[i]['kt_t']) * strict).astype(BF16) for i in ids}
    arb = {i: (_mm_nt(pre[i]['rt_s'], pre[i]['bt_t']) * incl).astype(BF16) for i in ids}
    ark = {i: (_mm_nt(pre[i]['rt_s'], pre[i]['kt_t']) * incl).astype(BF16) for i in ids}
    minv = {i: m.astype(BF16) for i, m in zip(ids, sk.tri_inv_many([-lab[i] for i in ids]))}
    a2 = {i: _mm(aak[i], pre[i]['v_s']) for i in ids}
    u0 = {i: _mm(minv[i], a2[i]) for i in ids}
    m1 = {i: _mm(minv[i], pre[i]['at_s']).astype(BF16) for i in ids}
    ork = {i: _mm(ark[i], pre[i]['v_s']) for i in ids}
    skv = {i: _mm_tn(pre[i]['v_s'], pre[i]['kd_s']) for i in ids}

    s = [st[g] for g in range(grp)]
    outs = {}
    for ci in range(tm // c):
        sb = [s[g].astype(BF16) for g in range(grp)]
        us = [u0[(g, ci)] + _mm_nt(m1[(g, ci)], sb[g]) for g in range(grp)]
        oq = [_mm_nt(pre[(g, ci)]['rt_s'], sb[g]) for g in range(grp)]
        for g in range(grp):
            i = (g, ci)
            s[g] = s[g] * pre[i]['dc'] + _mm_tn(us[g], pre[i]['bd_s']) + skv[i]
            outs[i] = sk.unstack(oq[g] + _mm(arb[i], us[g]) + ork[i])
    for g in range(grp):
        st[g] = s[g]
        s_out[g] = s[g]
    o = jnp.concatenate([outs[(g, ci)] for g in range(grp) for ci in range(tm // c)], axis=0)
    mean = _hsum(o, bd) * (1.0 / HEAD_W)
    d = o - mean
    var = _hsum(d * d, bd) * (1.0 / HEAD_W)
    o = d * lax.rsqrt(var + B_GN_EPS) * gnw_ref[...] + gnb_ref[...]
    bonus = _hsum(r * k2 * rk_ref[...], bd) * v
    y_out[...] = ((o + bonus) * g_gate).reshape(grp, tm, GROUP_W)


def _conv4_rows(work, new, wv, hist_out, grp, tm):
    ys = []
    for g in range(grp):
        u = new[g * tm:(g + 1) * tm]
        prev = work[g]
        y = u * wv[3:4, :]
        for d in (1, 2, 3):
            y = y + _shift_rows(u, prev, d) * wv[3 - d:4 - d, :]
        ys.append(y)
        tail = u[tm - SUBLANE:tm]
        work[g] = tail
        hist_out[g] = tail
    return jnp.concatenate(ys, axis=0)


def _ssd_kernel(x_ref, g1_ref, wc_ref, hist_ref, s0_ref, cw_ref, cb_ref, dtb_ref, alog_ref,
                dskip_ref, gn_ref, hm_ref, tri_ref, join_ref, tl_ref, bd_ref,
                y_out, s_out, hist_out, work, st, *, tm, c, front, grp):
    j = pl.program_id(1)

    @pl.when(j == 0)
    def _():
        work[...] = hist_ref[...]
        st[...] = s0_ref[...]

    h = _norm_in_rows(x_ref, g1_ref, j, tm, front)
    pc = jnp.dot(h, wc_ref[...], preferred_element_type=F32)
    z = pc[:, 0:256]
    xbc = _silu(_conv4_rows(work, pc[:, 256:1024], cw_ref[...], hist_out, grp, tm) + cb_ref[...])
    xs = xbc[:, 0:256]
    bm = xbc[:, 256:512]
    cm = xbc[:, 512:768]
    dt = _softplus(pc[:, 1024:1280] + dtb_ref[...])
    if front > 0:
        pos = j * tm + _iota((1, tm, 1), 1)
        dt = jnp.where(pos >= front, dt.reshape(grp, tm, GROUP_W), 0.0).reshape(grp * tm, GROUP_W)
    a = dt * (-jnp.exp(alog_ref[...]))
    xdt = xs * dt

    sk = _Stk(c, hm_ref, tri_ref, join_ref, tl_ref)
    ids, rows = _chunk_ids(grp, tm, c)
    acs = {i: sk.cumsum(a[rows[i]]) for i in ids}
    dms = {i: sk.decay(acs[i]) for i in ids}
    xdt_s = {i: sk.stack_b(xdt[rows[i]]) for i in ids}
    amat = {i: _mm_nt(sk.stack_b(cm[rows[i]]), sk.tile(bm[rows[i]].astype(BF16))) * dms[i] for i in ids}
    ydiag = {i: _mm(amat[i], xdt_s[i]) for i in ids}
    sx = {i: _mm_tn(sk.stack_b(bm[rows[i]] * jnp.exp(acs[i][c - 1:c, :] - acs[i])), xdt_s[i]) for i in ids}
    ce_s = {i: sk.stack_b(cm[rows[i]] * jnp.exp(acs[i])) for i in ids}

    s = [st[g] for g in range(grp)]
    outs = {}
    for ci in range(tm // c):
        for g in range(grp):
            i = (g, ci)
            outs[i] = sk.unstack(ydiag[i] + _mm(ce_s[i], s[g]))
            s[g] = s[g] * jnp.exp(acs[i][c - 1:c, :]) + sx[i]
    for g in range(grp):
        st[g] = s[g]
        s_out[g] = s[g]
    y = jnp.concatenate([outs[(g, ci)] for g in range(grp) for ci in range(tm // c)], axis=0)
    y = y + dskip_ref[...] * xs
    y_out[...] = _rms(y * _silu(z), gn_ref[...], GROUP_W).reshape(grp, tm, GROUP_W)


def _gdn_kernel(x_ref, g1_ref, wd_ref, hist_ref, s0_ref, cw_ref, alog_ref, dtb_ref, gn_ref,
                hm_ref, tri_ref, join_ref, tl_ref, bd_ref,
                y_out, s_out, hist_out, work, st, *, tm, c, front, grp):
    j = pl.program_id(1)

    @pl.when(j == 0)
    def _():
        work[...] = hist_ref[...]
        st[...] = s0_ref[...]

    h = _norm_in_rows(x_ref, g1_ref, j, tm, front)
    pd = jnp.dot(h, wd_ref[...], preferred_element_type=F32)
    qkv = _silu(_conv4_rows(work, pd[:, 0:768], cw_ref[...], hist_out, grp, tm))
    z = pd[:, 768:1024]
    beta = _sigmoid(pd[:, 1024:1280])
    g_log = -jnp.exp(alog_ref[...]) * _softplus(pd[:, 1280:1536] + dtb_ref[...])
    bd = bd_ref[...]
    q = qkv[:, 0:256]
    k = qkv[:, 256:512]
    v = qkv[:, 512:768]
    q = q * lax.rsqrt(_hsum(q * q, bd) + L2_EPS) * (HEAD_W ** -0.5)
    k = k * lax.rsqrt(_hsum(k * k, bd) + L2_EPS)

    sk = _Stk(c, hm_ref, tri_ref, join_ref, tl_ref)
    ids, rows = _chunk_ids(grp, tm, c)
    strict = sk.mask(TRI_STRICT)
    gcs = {i: sk.cumsum(g_log[rows[i]]) for i in ids}
    dms = {i: sk.decay(gcs[i]) for i in ids}
    kb = {i: k[rows[i]] * beta[rows[i]] for i in ids}
    k_t = {i: sk.tile(k[rows[i]].astype(BF16)) for i in ids}
    lower = {i: _mm_nt(sk.stack_b(kb[i]), k_t[i]) * dms[i] * strict for i in ids}
    aqk = {i: (_mm_nt(sk.stack_b(q[rows[i]]), k_t[i]) * dms[i]).astype(BF16) for i in ids}
    tinv = {i: t.astype(BF16) for i, t in zip(ids, sk.tri_inv_many([lower[i] for i in ids]))}
    u = {i: _mm(tinv[i], sk.stack_b(v[rows[i]] * beta[rows[i]])) for i in ids}
    w = {i: _mm(tinv[i], sk.stack_b(kb[i] * jnp.exp(gcs[i]))).astype(BF16) for i in ids}
    qe_s = {i: sk.stack_b(q[rows[i]] * jnp.exp(gcs[i])) for i in ids}
    kd_s = {i: sk.stack_b(k[rows[i]] * jnp.exp(gcs[i][c - 1:c, :] - gcs[i])) for i in ids}

    s = [st[g] for g in range(grp)]
    outs = {}
    for ci in range(tm // c):
        sb = [s[g].astype(BF16) for g in range(grp)]
        vn = [u[(g, ci)] - _mm(w[(g, ci)], sb[g]) for g in range(grp)]
        oq = [_mm(qe_s[(g, ci)], sb[g]) for g in range(grp)]
        for g in range(grp):
            i = (g, ci)
            s[g] = s[g] * jnp.exp(gcs[i][c - 1:c, :]) + _mm_tn(kd_s[i], vn[g])
            outs[i] = sk.unstack(oq[g] + _mm(aqk[i], vn[g]))
    for g in range(grp):
        st[g] = s[g]
        s_out[g] = s[g]
    o = jnp.concatenate([outs[(g, ci)] for g in range(grp) for ci in range(tm // c)], axis=0)
    ms = _hsum(o * o, bd) * (1.0 / HEAD_W)
    y_out[...] = (o * lax.rsqrt(ms + EPS) * gn_ref[...] * _silu(z)).reshape(grp, tm, GROUP_W)


def _ffn_kernel(x_ref, ya_ref, yb_ref, yc_ref, yd_ref, wo_ref, g2_ref, wup_ref, cw_ref, wdn_ref,
                hist_ref, xo_ref, hist_out, carry, *, tm, front):
    j = pl.program_id(1)

    @pl.when(j == 0)
    def _():
        carry[...] = hist_ref[0]

    x = x_ref[0] + _mm_tn(ya_ref[0], wo_ref[0:GROUP_W, :])
    for idx, y_ref in ((1, yb_ref), (2, yc_ref), (3, yd_ref)):
        x = x + jnp.dot(y_ref[0].astype(BF16), wo_ref[GROUP_W * idx:GROUP_W * (idx + 1), :],
                        preferred_element_type=F32)
    h2 = _rms(x, g2_ref[...], D_MODEL)
    if front > 0:
        rows = j * tm + _iota((tm, 1), 0)
        h2 = jnp.where(rows >= front, h2, 0.0)
    h2 = h2.astype(BF16)
    acc = jnp.zeros((tm, D_MODEL), F32)
    w2 = 2 * FFN_CW
    n_f = D_FF // FFN_CW
    up = lambda f: jnp.dot(h2, wup_ref[:, f * w2:(f + 1) * w2], preferred_element_type=F32)
    u_next = up(0)
    acts, k0 = [], 0
    for f in range(n_f):
        cols = slice(f * w2, (f + 1) * w2)
        u = u_next
        if f + 1 < n_f:
            u_next = up(f + 1)
        cr = carry[:, cols]
        cw = cw_ref[:, cols]
        y = _shift_rows(u, cr, 2) * cw[0:1, :] + _shift_rows(u, cr, 1) * cw[1:2, :] + u * cw[2:3, :]
        carry[:, cols] = u[tm - SUBLANE:tm]
        acts.append((_silu(y[:, 0:FFN_CW]) * y[:, FFN_CW:w2]).astype(BF16))
        if len(acts) == FFN_DOWN_GROUP or f + 1 == n_f:
            k1 = k0 + FFN_CW * len(acts)
            acc = acc + jnp.dot(jnp.concatenate(acts, axis=1) if len(acts) > 1 else acts[0],
                                wdn_ref[k0:k1, :], preferred_element_type=F32)
            acts, k0 = [], k1
    xo_ref[0] = x + acc
    hist_out[0] = carry[...]


def _final_norm_kernel(x_ref, g_ref, o_ref):
    o_ref[0] = _rms(x_ref[0], g_ref[...], D_MODEL)


def _const_spec(arr):
    nd = arr.ndim
    return pl.BlockSpec(arr.shape, lambda b, j: (0,) * nd, pipeline_mode=pl.Buffered(1))


def _tile_spec(tm, width):
    return pl.BlockSpec((1, tm, width), lambda b, j: (b, j, 0))


def _batch_spec(rows, width):
    return pl.BlockSpec((1, rows, width), lambda b, j: (b, 0, 0))


def _params():
    return pltpu.CompilerParams(dimension_semantics=("arbitrary", "arbitrary"),
                                vmem_limit_bytes=VMEM_LIMIT)


def _cols_spec(rows, tm):
    return pl.BlockSpec((1, rows, tm), lambda b, j: (b, 0, j))


def _mla_prep(x, g1, wa, gq, wq1t, wq2t, gkv, tabq, tabk, tm, front, kv_consts=(), tk=0):
    b, l, _ = x.shape
    consts = (g1, wa, gq, wq1t, wq2t, gkv)
    out_specs = [_cols_spec(512, tm), _tile_spec(tm, A_KVRANK), _tile_spec(tm, A_ROPE)]
    out_shape = [jax.ShapeDtypeStruct((b, 512, l), BF16),
                 jax.ShapeDtypeStruct((b, l, A_KVRANK), F32),
                 jax.ShapeDtypeStruct((b, l, A_ROPE), F32)]
    if tk:
        nb = tm // tk
        out_specs += [_tile_spec(tm, 512),
                      pl.BlockSpec((1, nb, N_HEADS * V_ROWS, tk), lambda bb, j: (bb, j, 0, 0))]
        out_shape += [jax.ShapeDtypeStruct((b, l, 512), BF16),
                      jax.ShapeDtypeStruct((b, l // tk, N_HEADS * V_ROWS, tk), BF16)]
    return pl.pallas_call(
        functools.partial(_mla_prep_kernel, tm=tm, front=front, tk=tk),
        grid=(b, l // tm),
        in_specs=[_tile_spec(tm, D_MODEL)] + [_const_spec(a) for a in consts]
        + [pl.BlockSpec((2 * LANE, tm), lambda bb, j: (0, j)),
           pl.BlockSpec((tm, 2 * LANE), lambda bb, j: (j, 0))] + [_const_spec(a) for a in kv_consts],
        out_specs=out_specs, out_shape=out_shape,
        compiler_params=_params(), name="mla_prep",
    )(x, *consts, tabq, tabk, *kv_consts)


def _kv_up(c_all, kr_all, wk, ek, wvt, ones_col, tm):
    b, n, _ = c_all.shape
    consts = (wk, ek, wvt, ones_col)
    return pl.pallas_call(
        _kv_up_kernel,
        grid=(b, n // tm),
        in_specs=[_tile_spec(tm, A_KVRANK), _tile_spec(tm, A_ROPE)] + [_const_spec(a) for a in consts],
        out_specs=[_tile_spec(tm, 512), pl.BlockSpec((1, 1, N_HEADS * V_ROWS, tm), lambda bb, j: (bb, j, 0, 0))],
        out_shape=[jax.ShapeDtypeStruct((b, n, 512), BF16),
                   jax.ShapeDtypeStruct((b, n // tm, N_HEADS * V_ROWS, tm), BF16)],
        compiler_params=_params(), name="kv_up",
    )(c_all, kr_all, *consts)


def _flash(qt, k, vt, gout_col, tq, tk, causal, klo, khi):
    b, _, l = qt.shape
    n = k.shape[1]
    nkv = n // tk
    return pl.pallas_call(
        functools.partial(_flash_kernel, tq=tq, tk=tk, nkv=nkv, causal=causal, klo=klo, khi=khi),
        grid=(b, l // tq),
        in_specs=[_cols_spec(512, tq),
                  pl.BlockSpec((1, n, 512), lambda bb, j: (bb, 0, 0), pipeline_mode=pl.Buffered(1)),
                  pl.BlockSpec((1, nkv, N_HEADS * V_ROWS, tk), lambda bb, j: (bb, 0, 0, 0), pipeline_mode=pl.Buffered(1)),
                  _const_spec(gout_col)],
        out_specs=_cols_spec(GROUP_W, tq),
        out_shape=jax.ShapeDtypeStruct((b, GROUP_W, l), F32),
        scratch_shapes=[pltpu.VMEM((N_HEADS * SUBLANE, tq), F32), pltpu.VMEM((N_HEADS * V_ROWS, tq), F32),
                        pltpu.VMEM((N_HEADS * tk, tq), F32), pltpu.VMEM((N_HEADS * tk, tq), F32)],
        compiler_params=_params(), name="mla_flash",
    )(qt, k, vt, gout_col)


def _stack_consts(c):
    n = N_HEADS * c
    lc = int(math.log2(c))
    r, l = np.arange(n)[:, None], np.arange(GROUP_W)[None, :]
    hm = ((r >> lc) == (l >> 6)).astype(np.float32)
    rr, cc = np.arange(n)[:, None], np.arange(n)[None, :]
    same = (rr >> lc) == (cc >> lc)
    tri = [same & (cc <= rr), same & (cc < rr)]
    join = [((rr >> (lb + 1)) == (cc >> (lb + 1))) & (((rr >> lb) & 1) == 1) & (((cc >> lb) & 1) == 0)
            for lb in range(lc)] + [rr == cc]
    tl = np.arange(c)[None, :] <= np.arange(c)[:, None]
    hh = np.arange(GROUP_W)
    bd = (hh[:, None] >> 6) == (hh[None, :] >> 6)
    return (jnp.asarray(hm), jnp.asarray(np.stack(tri).astype(np.float32)),
            jnp.asarray(np.stack(join), BF16), jnp.asarray(tl, BF16), jnp.asarray(bd, BF16))


def _scan_call(kernel, name, x, consts_a, hist, s0, consts_b, width_in, tm, c, front, grp):
    b, l, _ = x.shape
    assert b % grp == 0
    consts_b = tuple(consts_b) + _stack_consts(c)
    rows_spec = lambda rows, width: pl.BlockSpec((grp, rows, width), lambda bb, j: (bb, 0, 0))
    return pl.pallas_call(
        functools.partial(kernel, tm=tm, c=c, front=front, grp=grp),
        grid=(b // grp, l // tm),
        in_specs=[pl.BlockSpec((grp, tm, D_MODEL), lambda bb, j: (bb, j, 0))]
        + [_const_spec(a) for a in consts_a]
        + [rows_spec(SUBLANE, width_in), rows_spec(GROUP_W, GROUP_W)]
        + [_const_spec(a) for a in consts_b],
        out_specs=[pl.BlockSpec((grp, tm, GROUP_W), lambda bb, j: (bb, j, 0)),
                   rows_spec(GROUP_W, GROUP_W), rows_spec(SUBLANE, width_in)],
        out_shape=[jax.ShapeDtypeStruct((b, l, GROUP_W), F32),
                   jax.ShapeDtypeStruct((b, GROUP_W, GROUP_W), F32),
                   jax.ShapeDtypeStruct((b, SUBLANE, width_in), F32)],
        scratch_shapes=[pltpu.VMEM((grp, SUBLANE, width_in), F32),
                        pltpu.VMEM((grp, GROUP_W, GROUP_W), F32)],
        compiler_params=_params(), name=name,
    )(x, *consts_a, hist, s0, *consts_b)


def _ffn(x, ya, yb, yc, yd, wo, g2, wup, cw, wdn, hist, tm, front):
    b, l, _ = x.shape
    return pl.pallas_call(
        functools.partial(_ffn_kernel, tm=tm, front=front),
        grid=(b, l // tm),
        in_specs=[_tile_spec(tm, D_MODEL), _cols_spec(GROUP_W, tm)] + [_tile_spec(tm, GROUP_W)] * 3
        + [_const_spec(a) for a in (wo, g2, wup, cw, wdn)] + [_batch_spec(SUBLANE, 2 * D_FF)],
        out_specs=[_tile_spec(tm, D_MODEL), _batch_spec(SUBLANE, 2 * D_FF)],
        out_shape=[jax.ShapeDtypeStruct((b, l, D_MODEL), F32),
                   jax.ShapeDtypeStruct((b, SUBLANE, 2 * D_FF), F32)],
        scratch_shapes=[pltpu.VMEM((SUBLANE, 2 * D_FF), F32)],
        compiler_params=_params(), name="out_ffn",
    )(x, ya, yb, yc, yd, wo, g2, wup, cw, wdn, hist)


def _final_norm(x, g, tm, skip_tiles, out_rows):
    b = x.shape[0]
    return pl.pallas_call(
        _final_norm_kernel,
        grid=(b, out_rows // tm),
        in_specs=[pl.BlockSpec((1, tm, D_MODEL), lambda bb, j: (bb, j + skip_tiles, 0)), _const_spec(g)],
        out_specs=_tile_spec(tm, D_MODEL),
        out_shape=jax.ShapeDtypeStruct((b, out_rows, D_MODEL), F32),
        compiler_params=_params(), name="final_norm",
    )(x, g)


def _np_idx():
    z = IN_COLS
    zpad = lambda n: [z] * n
    rep = lambda base: [base + i for i in range(N_HEADS) for _ in range(HEAD_W)]
    grp = lambda base: [base + g * HEAD_W + i for g in (0, 0, 1, 1) for i in range(HEAD_W)]
    a = (list(range(0, 192)) + zpad(64) + list(range(192, 320))
         + list(range(320, 352)) + zpad(96)
         + list(range(336, 352)) + list(range(320, 336)) + zpad(96))
    b0 = A_COLS
    bcols = list(range(b0, b0 + B_COLS))
    c0 = b0 + B_COLS
    ccols = (list(range(c0, c0 + 256)) + list(range(c0 + 256, c0 + 512))
             + grp(c0 + 512) + grp(c0 + 640) + rep(c0 + 768))
    d0 = c0 + C_COLS
    dcols = list(range(d0, d0 + 1024)) + rep(d0 + 1024) + rep(d0 + 1028)
    xbc_exp = list(range(256)) + grp(256) + grp(384)
    xbc_back = (list(range(256)) + list(range(256, 320)) + list(range(384, 448))
                + list(range(512, 576)) + list(range(640, 704)))
    ffn_perm = []
    for f in range(D_FF // FFN_CW):
        ffn_perm += list(range(f * FFN_CW, (f + 1) * FFN_CW))
        ffn_perm += list(range(D_FF + f * FFN_CW, D_FF + (f + 1) * FFN_CW))
    ffn_back = np.argsort(np.array(ffn_perm))
    as_i = lambda v: np.asarray(v, np.int32)
    return dict(a=as_i(a), b=as_i(bcols), c=as_i(ccols), d=as_i(dcols), xbc_exp=as_i(xbc_exp),
                xbc_back=as_i(xbc_back), ffn_perm=as_i(ffn_perm), ffn_back=as_i(ffn_back))


_IDX = _np_idx()


def _rep_heads(v):
    return jnp.repeat(v, HEAD_W, axis=-1)


def _pad_rows(w, rows=SUBLANE):
    return jnp.pad(w, [(0, rows - w.shape[0])] + [(0, 0)] * (w.ndim - 1))


def _layer_consts(P, l):
    row = lambda v: v.reshape(1, -1).astype(F32)
    w_in = jnp.concatenate([P['w_in'][l], jnp.zeros((D_MODEL, 1), F32)], axis=1)
    c = {}
    c['g1'] = row(P['norm1_g'][l])
    c['wa'] = w_in[:, _IDX['a']].astype(BF16)
    c['wb'] = w_in[:, _IDX['b']].astype(BF16)
    c['wc'] = w_in[:, _IDX['c']].astype(BF16)
    c['wd'] = w_in[:, _IDX['d']].astype(BF16)
    c['gq'] = row(jnp.pad(P['a_gq'][l], (0, 64)))
    wuq = P['a_wuq'][l].reshape(A_QRANK, N_HEADS, A_NOPE + A_ROPE)
    rope = wuq[:, :, A_NOPE:]
    swap = jnp.concatenate([rope[..., 16:], rope[..., :16]], axis=-1)
    zeros = lambda n: jnp.zeros((A_QRANK, N_HEADS, n), F32)
    wq1 = jnp.concatenate([wuq, zeros(32)], axis=-1).reshape(A_QRANK, 512)
    wq2 = jnp.concatenate([zeros(64), swap, zeros(32)], axis=-1).reshape(A_QRANK, 512)
    c['wq1t'] = jnp.pad(wq1, ((0, 64), (0, 0))).T.astype(BF16)
    c['wq2t'] = jnp.pad(wq2, ((0, 64), (0, 0))).T.astype(BF16)
    c['gkv'] = row(P['a_gkv'][l])
    wuk = P['a_wuk'][l].reshape(A_KVRANK, N_HEADS, A_NOPE)
    c['wk'] = jnp.concatenate([wuk, jnp.zeros((A_KVRANK, N_HEADS, 64), F32)], axis=-1
                              ).reshape(A_KVRANK, 512).astype(BF16)
    ek = np.zeros((A_ROPE, N_HEADS, LANE), np.float32)
    for hh in range(N_HEADS):
        ek[np.arange(A_ROPE), hh, A_NOPE + np.arange(A_ROPE)] = 1.0
    c['ek'] = jnp.asarray(ek.reshape(A_ROPE, 512), BF16)
    c['ek128'] = jnp.pad(c['ek'], ((0, LANE - A_ROPE), (0, 0)))
    wuv = P['a_wuv'][l].reshape(A_KVRANK, N_HEADS, HEAD_W)
    c['wvt'] = jnp.concatenate([wuv, jnp.zeros((A_KVRANK, N_HEADS, V_ROWS - HEAD_W), F32)], axis=-1
                               ).reshape(A_KVRANK, N_HEADS * V_ROWS).T.astype(BF16)
    ones = np.zeros((N_HEADS, V_ROWS, 1), np.float32)
    ones[:, HEAD_W, 0] = 1.0
    c['ones_col'] = jnp.asarray(ones.reshape(N_HEADS * V_ROWS, 1))
    c['gout_col'] = P['a_gout'][l].reshape(GROUP_W, 1).astype(F32)
    c['mu'] = row(P['b_mu'][l])
    c['w0'] = row(P['b_w0'][l])
    z64 = jnp.zeros((64, GROUP_W), F32)
    c['wl'] = jnp.concatenate([jnp.concatenate([P['b_w2'][l], z64], axis=1),
                               jnp.concatenate([z64, P['b_a2'][l]], axis=1)], axis=0).astype(BF16)
    c['a0'] = row(P['b_a0'][l])
    c['g2b'] = P['b_g2'][l].astype(BF16)
    c['kk'] = row(P['b_kk'][l])
    c['ka'] = row(P['b_ka'][l])
    c['rk'] = row(P['b_rk'][l])
    c['gnw'] = row(P['b_gnw'][l])
    c['gnb'] = row(P['b_gnb'][l])
    c['c_cw'] = _pad_rows(P['c_convw'][l][:, _IDX['xbc_exp']])
    c['c_cb'] = row(P['c_convb'][l][_IDX['xbc_exp']])
    c['c_dtb'] = row(_rep_heads(P['c_dtb'][l]))
    c['c_alog'] = row(_rep_heads(P['c_alog'][l]))
    c['c_d'] = row(_rep_heads(P['c_d'][l]))
    c['c_gn'] = row(P['c_gnorm'][l])
    c['d_cw'] = _pad_rows(P['d_convw'][l])
    c['d_alog'] = row(_rep_heads(P['d_alog'][l]))
    c['d_dtb'] = row(_rep_heads(P['d_dtb'][l]))
    c['d_gn'] = row(jnp.tile(P['d_gnorm'][l], N_HEADS))
    c['wo'] = P['w_out'][l].astype(BF16)
    c['g2'] = row(P['norm2_g'][l])
    c['wup'] = P['f_wup'][l][:, _IDX['ffn_perm']].astype(BF16)
    c['f_cw'] = _pad_rows(P['f_convw'][l][:, _IDX['ffn_perm']])
    c['wdn'] = P['f_wdown'][l].astype(BF16)
    return c


def _embed_bd(s):
    b = s.shape[0]
    eye = jnp.eye(N_HEADS, dtype=s.dtype)
    return jnp.einsum('bhij,hg->bhigj', s, eye).reshape(b, GROUP_W, GROUP_W)


def _extract_bd(s):
    b = s.shape[0]
    s5 = s.reshape(b, N_HEADS, HEAD_W, N_HEADS, HEAD_W)
    return jnp.stack([s5[:, hh, :, hh, :] for hh in range(N_HEADS)], axis=1)


def _hist8(hist):
    return jnp.pad(hist, ((0, 0), (SUBLANE - hist.shape[1], 0), (0, 0)))


def _rope_table(pos):
    half = A_ROPE // 2
    inv = jnp.power(ROPE_BASE, -jnp.arange(half, dtype=F32) / half)
    ang = pos.astype(F32)[:, None] * inv
    cos, sin = jnp.cos(ang), jnp.sin(ang)
    cos2 = jnp.concatenate([cos, cos], axis=-1)
    sin2 = jnp.concatenate([-sin, sin], axis=-1)
    n = pos.shape[0]
    one = jnp.ones((n, A_NOPE), F32)
    z = lambda w: jnp.zeros((n, w), F32)
    tabq = jnp.concatenate([one, cos2, z(32), z(64), sin2, z(32)], axis=-1).T
    tabk = jnp.concatenate([cos2, z(96), sin2, z(96)], axis=-1)
    return tabq, tabk


def _trunk(x, pos, front, st, P, *, tm_scan, tm_ffn, c, tq, causal, n_keys_pad, scan_rows):
    b, l, _ = x.shape
    tabq, tabk = _rope_table(pos)
    new = {name: [] for name in ('ckv', 'krope', 'rwkv_S', 'rwkv_shift', 'ssd_S', 'ssd_conv',
                                 'gdn_S', 'gdn_conv', 'ffn_conv')}
    zeros_bd = jnp.zeros((b, GROUP_W, GROUP_W), F32)
    for li in range(DEPTH):
        c_ = P[li]
        prep = (x, c_['g1'], c_['wa'], c_['gq'], c_['wq1t'], c_['wq2t'], c_['gkv'], tabq, tabk, tm_ffn, front)
        if st is None:
            qt, ckv, krope, kf, vt = _mla_prep(
                *prep, kv_consts=(c_['wk'], c_['ek128'], c_['wvt'], c_['ones_col']), tk=tq)
            tkv, klo, khi = tq, front, l
            s_b = s_c = s_d = zeros_bd
            shift8 = jnp.zeros((b, SUBLANE, B_COLS), F32)
            chist = jnp.zeros((b, SUBLANE, 768), F32)
            dhist = jnp.zeros((b, SUBLANE, 768), F32)
            fhist = jnp.zeros((b, SUBLANE, 2 * D_FF), F32)
        else:
            qt, ckv, krope = _mla_prep(*prep)
            past = st['ckv'].shape[2]
            padk = n_keys_pad - past - l
            c_all = jnp.concatenate([st['ckv'][li], ckv, jnp.zeros((b, padk, A_KVRANK), F32)], axis=1)
            kr_all = jnp.concatenate([st['krope'][li], krope, jnp.zeros((b, padk, A_ROPE), F32)], axis=1)
            tkv, klo, khi = n_keys_pad, 0, past + l
            kf, vt = _kv_up(c_all, kr_all, c_['wk'], c_['ek'], c_['wvt'], c_['ones_col'], tkv)
            s_b = _embed_bd(st['rwkv_S'][li])
            s_c = _embed_bd(jnp.swapaxes(st['ssd_S'][li], -1, -2))
            s_d = _embed_bd(st['gdn_S'][li])
            shift8 = _hist8(st['rwkv_shift'][li][:, None, :])
            chist = _hist8(st['ssd_conv'][li][:, :, _IDX['xbc_exp']])
            dhist = _hist8(st['gdn_conv'][li])
            fhist = _hist8(st['ffn_conv'][li][:, :, _IDX['ffn_perm']])
        ya = _flash(qt, kf, vt, c_['gout_col'], tq, tkv, causal, klo, khi)
        yb, sb_new, shift_new = _scan_call(
            _rwkv_kernel, "rwkv7", x, (c_['g1'], c_['wb']), shift8, s_b,
            (c_['mu'], c_['w0'], c_['wl'], c_['a0'], c_['g2b'], c_['kk'], c_['ka'], c_['rk'],
             c_['gnw'], c_['gnb']), B_COLS, tm_scan, c, front, scan_rows)
        yc, sc_new, chist_new = _scan_call(
            _ssd_kernel, "ssd", x, (c_['g1'], c_['wc']), chist, s_c,
            (c_['c_cw'], c_['c_cb'], c_['c_dtb'], c_['c_alog'], c_['c_d'], c_['c_gn']),
            768, tm_scan, c, front, scan_rows)
        yd, sd_new, dhist_new = _scan_call(
            _gdn_kernel, "gdn", x, (c_['g1'], c_['wd']), dhist, s_d,
            (c_['d_cw'], c_['d_alog'], c_['d_dtb'], c_['d_gn']), 768, tm_scan, c, front, scan_rows)
        x, fhist_new = _ffn(x, ya, yb, yc, yd, c_['wo'], c_['g2'], c_['wup'], c_['f_cw'], c_['wdn'],
                            fhist, tm_ffn, front)
        new['ckv'].append(ckv[:, front:])
        new['krope'].append(krope[:, front:])
        new['rwkv_S'].append(_extract_bd(sb_new))
        new['rwkv_shift'].append(shift_new[:, SUBLANE - 1])
        new['ssd_S'].append(jnp.swapaxes(_extract_bd(sc_new), -1, -2))
        new['ssd_conv'].append(chist_new[:, SUBLANE - (C_CONV - 1):][:, :, _IDX['xbc_back']])
        new['gdn_S'].append(_extract_bd(sd_new))
        new['gdn_conv'].append(dhist_new[:, SUBLANE - (D_CONV - 1):])
        new['ffn_conv'].append(fhist_new[:, SUBLANE - (FFN_CONV - 1):][:, :, _IDX['ffn_back']])
    return x, {name: jnp.stack(vals) for name, vals in new.items()}


def kernel(x_prompt, x_sample, cache_mla_ckv, cache_mla_krope, state_rwkv, state_rwkv_shift, state_ssd, state_ssd_conv, state_gdn, state_gdn_conv, state_ffn_conv, meta_tokens, norm1_g, w_in, a_gq, a_wuq, a_gkv, a_wuk, a_wuv, a_gout, b_mu, b_w0, b_w2, b_a0, b_a2, b_g2, b_kk, b_ka, b_rk, b_gnw, b_gnb, c_convw, c_convb, c_dtb, c_alog, c_d, c_gnorm, d_convw, d_alog, d_dtb, d_gnorm, w_out, norm2_g, f_wup, f_convw, f_wdown, final_g):
    P = dict(norm1_g=norm1_g, w_in=w_in, a_gq=a_gq, a_wuq=a_wuq, a_gkv=a_gkv, a_wuk=a_wuk,
             a_wuv=a_wuv, a_gout=a_gout, b_mu=b_mu, b_w0=b_w0, b_w2=b_w2, b_a0=b_a0, b_a2=b_a2,
             b_g2=b_g2, b_kk=b_kk, b_ka=b_ka, b_rk=b_rk, b_gnw=b_gnw, b_gnb=b_gnb,
             c_convw=c_convw, c_convb=c_convb, c_dtb=c_dtb, c_alog=c_alog, c_d=c_d, c_gnorm=c_gnorm,
             d_convw=d_convw, d_alog=d_alog, d_dtb=d_dtb, d_gnorm=d_gnorm,
             w_out=w_out, norm2_g=norm2_g, f_wup=f_wup, f_convw=f_convw, f_wdown=f_wdown)
    P = [_layer_consts(P, li) for li in range(DEPTH)]
    fin = final_g.reshape(1, -1).astype(F32)
    b_p, seq, _ = x_prompt.shape
    lx = N_META + seq
    assert seq % ATT_BLOCK == 0 and N_META <= ATT_BLOCK
    front = ATT_BLOCK - N_META
    l_pad = front + lx
    meta = jnp.broadcast_to(meta_tokens[None].astype(F32), (b_p, N_META, D_MODEL))
    x_ext = jnp.concatenate([jnp.zeros((b_p, front, D_MODEL), F32), meta, x_prompt], axis=1)
    pos_p = jnp.arange(l_pad, dtype=jnp.int32) - (front + N_META)
    tm_ffn = max(t for t in (ROW_TILE, 2 * ROW_TILE, 3 * ROW_TILE) if l_pad % t == 0)
    y_p, ns_p = _trunk(x_ext, pos_p, front, None, P, tm_scan=SCAN_TILE, tm_ffn=tm_ffn, c=CHUNK, tq=ATT_BLOCK,
                       causal=True, n_keys_pad=l_pad, scan_rows=math.gcd(b_p, SCAN_ROWS))
    y_prompt = _final_norm(y_p, fin, ROW_TILE, (front + N_META) // ROW_TILE, seq)
    b_s, t_s, _ = x_sample.shape
    past = cache_mla_ckv.shape[2]
    assert t_s <= CHUNK and t_s % 16 == 0 and (t_s & (t_s - 1)) == 0
    st_s = dict(ckv=cache_mla_ckv, krope=cache_mla_krope, rwkv_S=state_rwkv, rwkv_shift=state_rwkv_shift,
                ssd_S=state_ssd, ssd_conv=state_ssd_conv, gdn_S=state_gdn, gdn_conv=state_gdn_conv,
                ffn_conv=state_ffn_conv)
    pos_s = past + jnp.arange(t_s, dtype=jnp.int32)
    n_keys_pad = -(-(past + t_s) // LANE) * LANE
    y_s, ns_s = _trunk(x_sample, pos_s, 0, st_s, P, tm_scan=t_s, tm_ffn=t_s, c=t_s, tq=t_s, causal=False,
                       n_keys_pad=n_keys_pad, scan_rows=math.gcd(b_s, SCAN_ROWS_SHORT))
    y_sample = _final_norm(y_s, fin, t_s, 0, t_s)
    keys = ('ckv', 'krope', 'rwkv_S', 'rwkv_shift', 'ssd_S', 'ssd_conv', 'gdn_S', 'gdn_conv', 'ffn_conv')
    return (y_prompt, y_sample) + tuple(ns_p[k] for k in keys) + tuple(ns_s[k] for k in keys)
```

```python
import functools
import math

import numpy as np
import jax
import jax.numpy as jnp
from jax import lax
from jax.experimental import pallas as pl
from jax.experimental.pallas import tpu as pltpu

F32 = jnp.float32
BF16 = jnp.bfloat16

D_MODEL = 1024
DEPTH = 4
CHUNK = 64
N_META = 16
EPS = 1e-6
L2_EPS = 1e-6
GROUP_W = 256
N_HEADS = 4
HEAD_W = 64
A_NOPE = 64
A_ROPE = 32
A_QRANK = 192
A_KVRANK = 128
A_SCALE = (A_NOPE + A_ROPE) ** -0.5
ROPE_BASE = 10000.0
B_GN_EPS = 64e-5
B_COLS = 1024
C_CONV = 4
D_CONV = 4
D_FF = 2816
FFN_CONV = 3
A_COLS = 352
C_COLS = 772
D_COLS = 1032
IN_COLS = A_COLS + B_COLS + C_COLS + D_COLS

LANE = 128
SUBLANE = 8
VMEM_LIMIT = 56 * 1024 * 1024
NEG = -1e30
LOG2E = math.log2(math.e)
V_ROWS = 80
FFN_CW = 256
FFN_DOWN_GROUP = 6
ATT_BLOCK = 256
ROW_TILE = 256
SCAN_TILE = 128
SCAN_ROWS = 4
SCAN_ROWS_SHORT = 8


def _mm(a, b):
    return jnp.dot(a.astype(BF16), b.astype(BF16), preferred_element_type=F32)


def _mm_nt(a, b):
    return lax.dot_general(a.astype(BF16), b.astype(BF16), (((1,), (1,)), ((), ())),
                           preferred_element_type=F32)


def _mm_tn(a, b):
    return lax.dot_general(a.astype(BF16), b.astype(BF16), (((0,), (0,)), ((), ())),
                           preferred_element_type=F32)


def _split2(x):
    hi = x.astype(BF16)
    lo = (x - hi.astype(F32)).astype(BF16)
    return hi, lo


def _mm_x2(x, w):
    hi, lo = _split2(x)
    return (jnp.dot(hi, w, preferred_element_type=F32)
            + jnp.dot(lo, w, preferred_element_type=F32))


def _rms(x, g, n):
    ms = jnp.sum(x * x, axis=-1, keepdims=True) * (1.0 / n)
    return x * lax.rsqrt(ms + EPS) * g


def _softplus(x):
    return jnp.maximum(x, 0.0) + jnp.log1p(jnp.exp(-jnp.abs(x)))


def _sigmoid(x):
    return 1.0 / (1.0 + jnp.exp(-x))


def _silu(x):
    return x * _sigmoid(x)


def _iota(shape, axis):
    return lax.broadcasted_iota(jnp.int32, shape, axis)


def _norm_in(x_ref, g_ref, j, tm, front):
    h = _rms(x_ref[0], g_ref[...], D_MODEL)
    if front > 0:
        rows = j * tm + _iota((tm, 1), 0)
        h = jnp.where(rows >= front, h, 0.0)
    return h.astype(BF16)


def _norm_in_rows(x_ref, g_ref, j, tm, front):
    x = x_ref[...]
    h = _rms(x, g_ref[...], D_MODEL)
    if front > 0:
        pos = j * tm + _iota((1, tm, 1), 1)
        h = jnp.where(pos >= front, h, 0.0)
    return h.reshape(x.shape[0] * tm, D_MODEL).astype(BF16)


def _hsum(x, bd):
    return _mm_x2(x, bd)


TRI_INCL, TRI_STRICT = 0, 1


class _Stk:
    def __init__(self, c, hm_ref, tri_ref, join_ref, tl_ref):
        self.c = c
        self.n = N_HEADS * c
        self.steps = int(math.log2(c))
        assert 1 << self.steps == c
        self.hm_ref, self.tri_ref, self.join_ref, self.tl_ref = hm_ref, tri_ref, join_ref, tl_ref
        self.hm_b = None

    def mask(self, which):
        return self.tri_ref[which]

    def tile(self, x):
        return jnp.concatenate([x] * N_HEADS, axis=0)

    def stack(self, x):
        return self.tile(x) * self.hm_ref[...]

    def unstack(self, xs):
        c = self.c
        return xs[0:c] + xs[c:2 * c] + xs[2 * c:3 * c] + xs[3 * c:4 * c]

    def cumsum(self, x):
        hi, lo = _split2(x)
        tl = self.tl_ref[...]
        return jnp.dot(tl, hi, preferred_element_type=F32) + jnp.dot(tl, lo, preferred_element_type=F32)

    def decay(self, g_cum):
        gcol = jnp.min(self.stack(g_cum), axis=-1, keepdims=True)
        grow = jnp.broadcast_to(gcol, (self.n, LANE)).T[0:1, :]
        return jnp.exp(jnp.minimum(gcol - grow, 0.0)) * self.mask(TRI_INCL)

    def tri_inv_many(self, lowers):
        lowers = [lw.astype(BF16) for lw in lowers]
        minvs = [self.join_ref[self.steps] - lw * self.join_ref[0] for lw in lowers]
        for lb in range(1, self.steps):
            ts = [_mm(m, lw * self.join_ref[lb]) for m, lw in zip(minvs, lowers)]
            minvs = [m - _mm(t, m).astype(BF16) for t, m in zip(ts, minvs)]
        return minvs

    def stack_b(self, x):
        if self.hm_b is None:
            self.hm_b = self.hm_ref[...].astype(BF16)
        return self.tile(x.astype(BF16)) * self.hm_b


def _mla_prep_kernel(x_ref, g1_ref, wa_ref, gq_ref, wq1t_ref, wq2t_ref, gkv_ref, tabq_ref, tabk_ref,
                     *rest, tm, front, tk):
    if tk:
        wk_ref, ek_ref, wvt_ref, ones_ref, qt_out, ckv_out, kr_out, k_out, vt_out = rest
    else:
        qt_out, ckv_out, kr_out = rest
    j = pl.program_id(1)
    h = _norm_in(x_ref, g1_ref, j, tm, front)
    pa = jnp.dot(h, wa_ref[...], preferred_element_type=F32)
    qn = _rms(pa[:, 0:256], gq_ref[...], A_QRANK)
    q1t = _mm_nt(wq1t_ref[...], qn)
    q2t = _mm_nt(wq2t_ref[...], qn)
    tabq = tabq_ref[...]
    cos4 = jnp.concatenate([tabq[0:128]] * N_HEADS, axis=0)
    sin4 = jnp.concatenate([tabq[128:256]] * N_HEADS, axis=0)
    qt_out[0] = ((q1t * cos4 + q2t * sin4) * (A_SCALE * LOG2E)).astype(BF16)
    c = _rms(pa[:, 256:384], gkv_ref[...], A_KVRANK)
    ckv_out[0] = c
    tabk = tabk_ref[...]
    kr = pa[:, 384:512] * tabk[:, 0:128] + pa[:, 512:640] * tabk[:, 128:256]
    kr_out[0] = kr[:, 0:A_ROPE]
    if tk:
        cb = c.astype(BF16)
        k_out[0] = (jnp.dot(cb, wk_ref[...], preferred_element_type=F32)
                    + jnp.dot(kr.astype(BF16), ek_ref[...], preferred_element_type=F32)).astype(BF16)
        vt = (_mm_nt(wvt_ref[...], cb) + ones_ref[...]).astype(BF16)
        for t in range(tm // tk):
            vt_out[0, t] = vt[:, t * tk:(t + 1) * tk]


def _kv_up_kernel(c_ref, kr_ref, wk_ref, ek_ref, wvt_ref, ones_ref, k_out, vt_out):
    c = c_ref[0].astype(BF16)
    kr = kr_ref[0].astype(BF16)
    k = (jnp.dot(c, wk_ref[...], preferred_element_type=F32)
         + jnp.dot(kr, ek_ref[...], preferred_element_type=F32))
    k_out[0] = k.astype(BF16)
    vt_out[0, 0] = (_mm_nt(wvt_ref[...], c) + ones_ref[...]).astype(BF16)


def _flash_kernel(qt_ref, k_ref, vt_ref, gout_ref, o_ref, m_sc, acc_sc, sa_sc, sb_sc,
                  *, tq, tk, nkv, causal, klo, khi):
    i = pl.program_id(1)
    m_sc[...] = jnp.full((N_HEADS * SUBLANE, tq), NEG, F32)
    acc_sc[...] = jnp.zeros((N_HEADS * V_ROWS, tq), F32)
    vrows = [slice(V_ROWS * h, V_ROWS * (h + 1)) for h in range(N_HEADS)]
    heads = [slice(LANE * h, LANE * (h + 1)) for h in range(N_HEADS)]
    s_bufs = (sa_sc, sb_sc)

    def produce(jb, slot):
        start = pl.multiple_of(jb * tk, tk)
        for h, rows in enumerate(heads):
            s_bufs[slot][h * tk:(h + 1) * tk, :] = jnp.dot(
                k_ref[0, pl.ds(start, tk), rows], qt_ref[0, rows, :], preferred_element_type=F32)

    def consume(jb, slot, masked, nxt):
        if nxt is not None:
            produce(*nxt)
        if masked:
            kpos = jb * tk + _iota((tk, 1), 0)
            qpos = i * tq + _iota((1, tq), 1)
            vis = (kpos >= klo) & (kpos < khi)
            if causal:
                vis = vis & ((kpos >> 6) <= (qpos >> 6))
        m_all = m_sc[...]
        acc_all = acc_sc[...]
        m_out, alphas, ps = [], [], []
        for h in range(N_HEADS):
            s = s_bufs[slot][h * tk:(h + 1) * tk, :]
            if masked:
                s = jnp.where(vis, s, NEG)
            m_prev = m_all[SUBLANE * h:SUBLANE * (h + 1)]
            m_new = jnp.maximum(m_prev, jnp.max(s, axis=0, keepdims=True))
            alphas.append(jnp.exp2(m_prev[0:1] - m_new[0:1]))
            ps.append(jnp.exp2(s - m_new[0:1]).astype(BF16))
            m_out.append(m_new)
        acc_out = [alphas[h] * acc_all[rows]
                   + jnp.dot(vt_ref[0, jb, rows, :], ps[h], preferred_element_type=F32)
                   for h, rows in enumerate(vrows)]
        m_sc[...] = jnp.concatenate(m_out, axis=0)
        acc_sc[...] = jnp.concatenate(acc_out, axis=0)

    produce(0, 0)
    if causal:
        @pl.when(i == 0)
        def _():
            consume(0, 0, True, None)

        @pl.when(i > 0)
        def _():
            consume(0, 0, True, (1, 1))
            octs = (i - 1) >> 3

            def pair(jb):
                consume(jb, 1, False, (jb + 1, 0))
                consume(jb + 1, 0, False, (jb + 2, 1))

            def body(t, carry):
                for p in range(4):
                    pair(1 + 8 * t + 2 * p)
                return carry

            lax.fori_loop(0, octs, body, 0)
            rest4 = 1 + 8 * octs

            @pl.when(((i - 1) & 4) != 0)
            def _():
                pair(rest4)
                pair(rest4 + 2)

            @pl.when(((i - 1) & 2) != 0)
            def _():
                pair(rest4 + ((i - 1) & 4))

            @pl.when((i & 1) == 0)
            def _():
                consume(i - 1, 1, False, (i, 0))
                consume(i, 0, True, None)

            @pl.when((i & 1) == 1)
            def _():
                consume(i, 1, True, None)
    else:
        for jb in range(nkv):
            consume(jb, jb & 1, True, (jb + 1, (jb + 1) & 1) if jb + 1 < nkv else None)
    outs = []
    for h in range(N_HEADS):
        a = acc_sc[V_ROWS * h:V_ROWS * (h + 1), :]
        outs.append(a[0:HEAD_W] / a[HEAD_W:HEAD_W + 1])
    yat = jnp.concatenate(outs, axis=0)
    ms = jnp.sum(yat * yat, axis=0, keepdims=True) * (1.0 / GROUP_W)
    o_ref[0] = yat * lax.rsqrt(ms + EPS) * gout_ref[...]


def _chunk_ids(grp, tm, c):
    ids = [(g, ci) for ci in range(tm // c) for g in range(grp)]
    return ids, {(g, ci): slice(g * tm + ci * c, g * tm + (ci + 1) * c) for g, ci in ids}


def _shift_rows(u, tail8, d):
    r = pltpu.roll(u, d, axis=0)
    row8 = _iota((SUBLANE, 1), 0)
    head = r[0:SUBLANE]
    for k in range(d):
        head = jnp.where(row8 == k, tail8[SUBLANE - d + k:SUBLANE - d + k + 1], head)
    return jnp.concatenate([head, r[SUBLANE:]], axis=0)


def _rwkv_kernel(x_ref, g1_ref, wb_ref, shift_ref, s0_ref, mu_ref, w0_ref, wl_ref, a0_ref,
                 g2_ref, kk_ref, ka_ref, rk_ref, gnw_ref, gnb_ref, hm_ref, tri_ref, join_ref, tl_ref, bd_ref,
                 y_out, s_out, shift_out, work, st, *, tm, c, front, grp):
    j = pl.program_id(1)

    @pl.when(j == 0)
    def _():
        work[...] = shift_ref[...]
        st[...] = s0_ref[...]

    h = _norm_in_rows(x_ref, g1_ref, j, tm, front)
    cols = jnp.dot(h, wb_ref[...], preferred_element_type=F32)
    shifted = []
    for g in range(grp):
        u = cols[g * tm:(g + 1) * tm]
        shifted.append(_shift_rows(u, work[g], 1))
        tail = u[tm - SUBLANE:tm]
        work[g] = tail
        shift_out[g] = tail
    shifted = jnp.concatenate(shifted, axis=0)
    xm = cols + (shifted - cols) * mu_ref[...]
    r = xm[:, 0:256]
    k = xm[:, 256:512]
    v = xm[:, 512:768]
    lora = xm[:, 768:896]
    dg = xm[:, 896:1024]
    lora = jnp.where(_iota((grp * tm, LANE), 1) < 64, jnp.tanh(lora), lora)
    ll = _mm(lora, wl_ref[...])
    w_log = -_softplus(-(w0_ref[...] + ll[:, 0:256])) - 0.5
    logd = -jnp.exp(w_log)
    a = _sigmoid(a0_ref[...] + ll[:, 256:512])
    g_gate = _mm(_sigmoid(dg), g2_ref[...])
    bd = bd_ref[...]
    kkr = k * kk_ref[...]
    kk = kkr * lax.rsqrt(_hsum(kkr * kkr, bd) + L2_EPS)
    k2 = k * (1.0 + (a - 1.0) * ka_ref[...])

    sk = _Stk(c, hm_ref, tri_ref, join_ref, tl_ref)
    ids, rows = _chunk_ids(grp, tm, c)
    strict, incl = sk.mask(TRI_STRICT), sk.mask(TRI_INCL)
    gcs = {i: sk.cumsum(logd[rows[i]]) for i in ids}
    pre = {}
    for i in ids:
        gc, ld = gcs[i], logd[rows[i]]
        eg, eng = jnp.exp(gc), jnp.exp(-gc)
        bt = kk[rows[i]] * a[rows[i]] * eng
        kt = k2[rows[i]] * eng
        dc = eg[c - 1:c, :]
        pre[i] = dict(at_s=sk.stack_b(-kk[rows[i]] * jnp.exp(gc - ld)), rt_s=sk.stack_b(r[rows[i]] * eg),
                      v_s=sk.stack_b(v[rows[i]]), bt_t=sk.tile(bt.astype(BF16)),
                      kt_t=sk.tile(kt.astype(BF16)), dc=dc,
                      bd_s=sk.stack_b(bt * dc), kd_s=sk.stack_b(kt * dc))
    lab = {i: _mm_nt(pre[i]['at_s'], pre[i]['bt_t']) * strict for i in ids}
    aak = {i: (_mm_nt(pre[i]['at_s'], pre[i]['kt_t']) * strict).astype(BF16) for i in ids}
    arb = {i: (_mm_nt(pre[i]['rt_s'], pre[i]['bt_t']) * incl).astype(BF16) for i in ids}
    ark = {i: (_mm_nt(pre[i]['rt_s'], pre[i]['kt_t']) * incl).astype(BF16) for i in ids}
    minv = {i: m.astype(BF16) for i, m in zip(ids, sk.tri_inv_many([-lab[i] for i in ids]))}
    a2 = {i: _mm(aak[i], pre[i]['v_s']) for i in ids}
    u0 = {i: _mm(minv[i], a2[i]) for i in ids}
    m1 = {i: _mm(minv[i], pre[i]['at_s']).astype(BF16) for i in ids}
    ork = {i: _mm(ark[i], pre[i]['v_s']) for i in ids}
    skv = {i: _mm_tn(pre[i]['v_s'], pre[i]['kd_s']) for i in ids}

    s = [st[g] for g in range(grp)]
    outs = {}
    for ci in range(tm // c):
        sb = [s[g].astype(BF16) for g in range(grp)]
        us = [u0[(g, ci)] + _mm_nt(m1[(g, ci)], sb[g]) for g in range(grp)]
        oq = [_mm_nt(pre[(g, ci)]['rt_s'], sb[g]) for g in range(grp)]
        for g in range(grp):
            i = (g, ci)
            s[g] = s[g] * pre[i]['dc'] + _mm_tn(us[g], pre[i]['bd_s']) + skv[i]
            outs[i] = sk.unstack(oq[g] + _mm(arb[i], us[g]) + ork[i])
    for g in range(grp):
        st[g] = s[g]
        s_out[g] = s[g]
    o = jnp.concatenate([outs[(g, ci)] for g in range(grp) for ci in range(tm // c)], axis=0)
    mean = _hsum(o, bd) * (1.0 / HEAD_W)
    d = o - mean
    var = _hsum(d * d, bd) * (1.0 / HEAD_W)
    o = d * lax.rsqrt(var + B_GN_EPS) * gnw_ref[...] + gnb_ref[...]
    bonus = _hsum(r * k2 * rk_ref[...], bd) * v
    y_out[...] = ((o + bonus) * g_gate).reshape(grp, tm, GROUP_W)


def _conv4_rows(work, new, wv, hist_out, grp, tm):
    ys = []
    for g in range(grp):
        u = new[g * tm:(g + 1) * tm]
        prev = work[g]
        y = u * wv[3:4, :]
        for d in (1, 2, 3):
            y = y + _shift_rows(u, prev, d) * wv[3 - d:4 - d, :]
        ys.append(y)
        tail = u[tm - SUBLANE:tm]
        work[g] = tail
        hist_out[g] = tail
    return jnp.concatenate(ys, axis=0)


def _ssd_kernel(x_ref, g1_ref, wc_ref, hist_ref, s0_ref, cw_ref, cb_ref, dtb_ref, alog_ref,
                dskip_ref, gn_ref, hm_ref, tri_ref, join_ref, tl_ref, bd_ref,
                y_out, s_out, hist_out, work, st, *, tm, c, front, grp):
    j = pl.program_id(1)

    @pl.when(j == 0)
    def _():
        work[...] = hist_ref[...]
        st[...] = s0_ref[...]

    h = _norm_in_rows(x_ref, g1_ref, j, tm, front)
    pc = jnp.dot(h, wc_ref[...], preferred_element_type=F32)
    z = pc[:, 0:256]
    xbc = _silu(_conv4_rows(work, pc[:, 256:1024], cw_ref[...], hist_out, grp, tm) + cb_ref[...])
    xs = xbc[:, 0:256]
    bm = xbc[:, 256:512]
    cm = xbc[:, 512:768]
    dt = _softplus(pc[:, 1024:1280] + dtb_ref[...])
    if front > 0:
        pos = j * tm + _iota((1, tm, 1), 1)
        dt = jnp.where(pos >= front, dt.reshape(grp, tm, GROUP_W), 0.0).reshape(grp * tm, GROUP_W)
    a = dt * (-jnp.exp(alog_ref[...]))
    xdt = xs * dt

    sk = _Stk(c, hm_ref, tri_ref, join_ref, tl_ref)
    ids, rows = _chunk_ids(grp, tm, c)
    acs = {i: sk.cumsum(a[rows[i]]) for i in ids}
    dms = {i: sk.decay(acs[i]) for i in ids}
    xdt_s = {i: sk.stack_b(xdt[rows[i]]) for i in ids}
    amat = {i: _mm_nt(sk.stack_b(cm[rows[i]]), sk.tile(bm[rows[i]].astype(BF16))) * dms[i] for i in ids}
    ydiag = {i: _mm(amat[i], xdt_s[i]) for i in ids}
    sx = {i: _mm_tn(sk.stack_b(bm[rows[i]] * jnp.exp(acs[i][c - 1:c, :] - acs[i])), xdt_s[i]) for i in ids}
    ce_s = {i: sk.stack_b(cm[rows[i]] * jnp.exp(acs[i])) for i in ids}

    s = [st[g] for g in range(grp)]
    outs = {}
    for ci in range(tm // c):
        for g in range(grp):
            i = (g, ci)
            outs[i] = sk.unstack(ydiag[i] + _mm(ce_s[i], s[g]))
            s[g] = s[g] * jnp.exp(acs[i][c - 1:c, :]) + sx[i]
    for g in range(grp):
        st[g] = s[g]
        s_out[g] = s[g]
    y = jnp.concatenate([outs[(g, ci)] for g in range(grp) for ci in range(tm // c)], axis=0)
    y = y + dskip_ref[...] * xs
    y_out[...] = _rms(y * _silu(z), gn_ref[...], GROUP_W).reshape(grp, tm, GROUP_W)


def _gdn_kernel(x_ref, g1_ref, wd_ref, hist_ref, s0_ref, cw_ref, alog_ref, dtb_ref, gn_ref,
                hm_ref, tri_ref, join_ref, tl_ref, bd_ref,
                y_out, s_out, hist_out, work, st, *, tm, c, front, grp):
    j = pl.program_id(1)

    @pl.when(j == 0)
    def _():
        work[...] = hist_ref[...]
        st[...] = s0_ref[...]

    h = _norm_in_rows(x_ref, g1_ref, j, tm, front)
    pd = jnp.dot(h, wd_ref[...], preferred_element_type=F32)
    qkv = _silu(_conv4_rows(work, pd[:, 0:768], cw_ref[...], hist_out, grp, tm))
    z = pd[:, 768:1024]
    beta = _sigmoid(pd[:, 1024:1280])
    g_log = -jnp.exp(alog_ref[...]) * _softplus(pd[:, 1280:1536] + dtb_ref[...])
    bd = bd_ref[...]
    q = qkv[:, 0:256]
    k = qkv[:, 256:512]
    v = qkv[:, 512:768]
    q = q * lax.rsqrt(_hsum(q * q, bd) + L2_EPS) * (HEAD_W ** -0.5)
    k = k * lax.rsqrt(_hsum(k * k, bd) + L2_EPS)

    sk = _Stk(c, hm_ref, tri_ref, join_ref, tl_ref)
    ids, rows = _chunk_ids(grp, tm, c)
    strict = sk.mask(TRI_STRICT)
    gcs = {i: sk.cumsum(g_log[rows[i]]) for i in ids}
    dms = {i: sk.decay(gcs[i]) for i in ids}
    kb = {i: k[rows[i]] * beta[rows[i]] for i in ids}
    k_t = {i: sk.tile(k[rows[i]].astype(BF16)) for i in ids}
    lower = {i: _mm_nt(sk.stack_b(kb[i]), k_t[i]) * dms[i] * strict for i in ids}
    aqk = {i: (_mm_nt(sk.stack_b(q[rows[i]]), k_t[i]) * dms[i]).astype(BF16) for i in ids}
    tinv = {i: t.astype(BF16) for i, t in zip(ids, sk.tri_inv_many([lower[i] for i in ids]))}
    u = {i: _mm(tinv[i], sk.stack_b(v[rows[i]] * beta[rows[i]])) for i in ids}
    w = {i: _mm(tinv[i], sk.stack_b(kb[i] * jnp.exp(gcs[i]))).astype(BF16) for i in ids}
    qe_s = {i: sk.stack_b(q[rows[i]] * jnp.exp(gcs[i])) for i in ids}
    kd_s = {i: sk.stack_b(k[rows[i]] * jnp.exp(gcs[i][c - 1:c, :] - gcs[i])) for i in ids}

    s = [st[g] for g in range(grp)]
    outs = {}
    for ci in range(tm // c):
        sb = [s[g].astype(BF16) for g in range(grp)]
        vn = [u[(g, ci)] - _mm(w[(g, ci)], sb[g]) for g in range(grp)]
        oq = [_mm(qe_s[(g, ci)], sb[g]) for g in range(grp)]
        for g in range(grp):
            i = (g, ci)
            s[g] = s[g] * jnp.exp(gcs[i][c - 1:c, :]) + _mm_tn(kd_s[i], vn[g])
            outs[i] = sk.unstack(oq[g] + _mm(aqk[i], vn[g]))
    for g in range(grp):
        st[g] = s[g]
        s_out[g] = s[g]
    o = jnp.concatenate([outs[(g, ci)] for g in range(grp) for ci in range(tm // c)], axis=0)
    ms = _hsum(o * o, bd) * (1.0 / HEAD_W)
    y_out[...] = (o * lax.rsqrt(ms + EPS) * gn_ref[...] * _silu(z)).reshape(grp, tm, GROUP_W)


def _ffn_kernel(x_ref, ya_ref, yb_ref, yc_ref, yd_ref, wo_ref, g2_ref, wup_ref, cw_ref, wdn_ref,
                hist_ref, xo_ref, hist_out, carry, *, tm, front):
    j = pl.program_id(1)

    @pl.when(j == 0)
    def _():
        carry[...] = hist_ref[0]

    x = x_ref[0] + _mm_tn(ya_ref[0], wo_ref[0:GROUP_W, :])
    for idx, y_ref in ((1, yb_ref), (2, yc_ref), (3, yd_ref)):
        x = x + jnp.dot(y_ref[0].astype(BF16), wo_ref[GROUP_W * idx:GROUP_W * (idx + 1), :],
                        preferred_element_type=F32)
    h2 = _rms(x, g2_ref[...], D_MODEL)
    if front > 0:
        rows = j * tm + _iota((tm, 1), 0)
        h2 = jnp.where(rows >= front, h2, 0.0)
    h2 = h2.astype(BF16)
    acc = jnp.zeros((tm, D_MODEL), F32)
    w2 = 2 * FFN_CW
    n_f = D_FF // FFN_CW
    up = lambda f: jnp.dot(h2, wup_ref[:, f * w2:(f + 1) * w2], preferred_element_type=F32)
    u_next = up(0)
    acts, k0 = [], 0
    for f in range(n_f):
        cols = slice(f * w2, (f + 1) * w2)
        u = u_next
        if f + 1 < n_f:
            u_next = up(f + 1)
        cr = carry[:, cols]
        cw = cw_ref[:, cols]
        y = _shift_rows(u, cr, 2) * cw[0:1, :] + _shift_rows(u, cr, 1) * cw[1:2, :] + u * cw[2:3, :]
        carry[:, cols] = u[tm - SUBLANE:tm]
        acts.append((_silu(y[:, 0:FFN_CW]) * y[:, FFN_CW:w2]).astype(BF16))
        if len(acts) == FFN_DOWN_GROUP or f + 1 == n_f:
            k1 = k0 + FFN_CW * len(acts)
            acc = acc + jnp.dot(jnp.concatenate(acts, axis=1) if len(acts) > 1 else acts[0],
                                wdn_ref[k0:k1, :], preferred_element_type=F32)
            acts, k0 = [], k1
    xo_ref[0] = x + acc
    hist_out[0] = carry[...]


def _final_norm_kernel(x_ref, g_ref, o_ref):
    o_ref[0] = _rms(x_ref[0], g_ref[...], D_MODEL)


def _const_spec(arr):
    nd = arr.ndim
    return pl.BlockSpec(arr.shape, lambda b, j: (0,) * nd, pipeline_mode=pl.Buffered(1))


def _tile_spec(tm, width):
    return pl.BlockSpec((1, tm, width), lambda b, j: (b, j, 0))


def _batch_spec(rows, width):
    return pl.BlockSpec((1, rows, width), lambda b, j: (b, 0, 0))


def _params():
    return pltpu.CompilerParams(dimension_semantics=("arbitrary", "arbitrary"),
                                vmem_limit_bytes=VMEM_LIMIT)


def _cols_spec(rows, tm):
    return pl.BlockSpec((1, rows, tm), lambda b, j: (b, 0, j))


def _mla_prep(x, g1, wa, gq, wq1t, wq2t, gkv, tabq, tabk, tm, front, kv_consts=(), tk=0):
    b, l, _ = x.shape
    consts = (g1, wa, gq, wq1t, wq2t, gkv)
    out_specs = [_cols_spec(512, tm), _tile_spec(tm, A_KVRANK), _tile_spec(tm, A_ROPE)]
    out_shape = [jax.ShapeDtypeStruct((b, 512, l), BF16),
                 jax.ShapeDtypeStruct((b, l, A_KVRANK), F32),
                 jax.ShapeDtypeStruct((b, l, A_ROPE), F32)]
    if tk:
        nb = tm // tk
        out_specs += [_tile_spec(tm, 512),
                      pl.BlockSpec((1, nb, N_HEADS * V_ROWS, tk), lambda bb, j: (bb, j, 0, 0))]
        out_shape += [jax.ShapeDtypeStruct((b, l, 512), BF16),
                      jax.ShapeDtypeStruct((b, l // tk, N_HEADS * V_ROWS, tk), BF16)]
    return pl.pallas_call(
        functools.partial(_mla_prep_kernel, tm=tm, front=front, tk=tk),
        grid=(b, l // tm),
        in_specs=[_tile_spec(tm, D_MODEL)] + [_const_spec(a) for a in consts]
        + [pl.BlockSpec((2 * LANE, tm), lambda bb, j: (0, j)),
           pl.BlockSpec((tm, 2 * LANE), lambda bb, j: (j, 0))] + [_const_spec(a) for a in kv_consts],
        out_specs=out_specs, out_shape=out_shape,
        compiler_params=_params(), name="mla_prep",
    )(x, *consts, tabq, tabk, *kv_consts)


def _kv_up(c_all, kr_all, wk, ek, wvt, ones_col, tm):
    b, n, _ = c_all.shape
    consts = (wk, ek, wvt, ones_col)
    return pl.pallas_call(
        _kv_up_kernel,
        grid=(b, n // tm),
        in_specs=[_tile_spec(tm, A_KVRANK), _tile_spec(tm, A_ROPE)] + [_const_spec(a) for a in consts],
        out_specs=[_tile_spec(tm, 512), pl.BlockSpec((1, 1, N_HEADS * V_ROWS, tm), lambda bb, j: (bb, j, 0, 0))],
        out_shape=[jax.ShapeDtypeStruct((b, n, 512), BF16),
                   jax.ShapeDtypeStruct((b, n // tm, N_HEADS * V_ROWS, tm), BF16)],
        compiler_params=_params(), name="kv_up",
    )(c_all, kr_all, *consts)


def _flash(qt, k, vt, gout_col, tq, tk, causal, klo, khi):
    b, _, l = qt.shape
    n = k.shape[1]
    nkv = n // tk
    return pl.pallas_call(
        functools.partial(_flash_kernel, tq=tq, tk=tk, nkv=nkv, causal=causal, klo=klo, khi=khi),
        grid=(b, l // tq),
        in_specs=[_cols_spec(512, tq),
                  pl.BlockSpec((1, n, 512), lambda bb, j: (bb, 0, 0), pipeline_mode=pl.Buffered(1)),
                  pl.BlockSpec((1, nkv, N_HEADS * V_ROWS, tk), lambda bb, j: (bb, 0, 0, 0), pipeline_mode=pl.Buffered(1)),
                  _const_spec(gout_col)],
        out_specs=_cols_spec(GROUP_W, tq),
        out_shape=jax.ShapeDtypeStruct((b, GROUP_W, l), F32),
        scratch_shapes=[pltpu.VMEM((N_HEADS * SUBLANE, tq), F32), pltpu.VMEM((N_HEADS * V_ROWS, tq), F32),
                        pltpu.VMEM((N_HEADS * tk, tq), F32), pltpu.VMEM((N_HEADS * tk, tq), F32)],
        compiler_params=_params(), name="mla_flash",
    )(qt, k, vt, gout_col)


def _stack_consts(c):
    n = N_HEADS * c
    lc = int(math.log2(c))
    r, l = np.arange(n)[:, None], np.arange(GROUP_W)[None, :]
    hm = ((r >> lc) == (l >> 6)).astype(np.float32)
    rr, cc = np.arange(n)[:, None], np.arange(n)[None, :]
    same = (rr >> lc) == (cc >> lc)
    tri = [same & (cc <= rr), same & (cc < rr)]
    join = [((rr >> (lb + 1)) == (cc >> (lb + 1))) & (((rr >> lb) & 1) == 1) & (((cc >> lb) & 1) == 0)
            for lb in range(lc)] + [rr == cc]
    tl = np.arange(c)[None, :] <= np.arange(c)[:, None]
    hh = np.arange(GROUP_W)
    bd = (hh[:, None] >> 6) == (hh[None, :] >> 6)
    return (jnp.asarray(hm), jnp.asarray(np.stack(tri).astype(np.float32)),
            jnp.asarray(np.stack(join), BF16), jnp.asarray(tl, BF16), jnp.asarray(bd, BF16))


def _scan_call(kernel, name, x, consts_a, hist, s0, consts_b, width_in, tm, c, front, grp):
    b, l, _ = x.shape
    assert b % grp == 0
    consts_b = tuple(consts_b) + _stack_consts(c)
    rows_spec = lambda rows, width: pl.BlockSpec((grp, rows, width), lambda bb, j: (bb, 0, 0))
    return pl.pallas_call(
        functools.partial(kernel, tm=tm, c=c, front=front, grp=grp),
        grid=(b // grp, l // tm),
        in_specs=[pl.BlockSpec((grp, tm, D_MODEL), lambda bb, j: (bb, j, 0))]
        + [_const_spec(a) for a in consts_a]
        + [rows_spec(SUBLANE, width_in), rows_spec(GROUP_W, GROUP_W)]
        + [_const_spec(a) for a in consts_b],
        out_specs=[pl.BlockSpec((grp, tm, GROUP_W), lambda bb, j: (bb, j, 0)),
                   rows_spec(GROUP_W, GROUP_W), rows_spec(SUBLANE, width_in)],
        out_shape=[jax.ShapeDtypeStruct((b, l, GROUP_W), F32),
                   jax.ShapeDtypeStruct((b, GROUP_W, GROUP_W), F32),
                   jax.ShapeDtypeStruct((b, SUBLANE, width_in), F32)],
        scratch_shapes=[pltpu.VMEM((grp, SUBLANE, width_in), F32),
                        pltpu.VMEM((grp, GROUP_W, GROUP_W), F32)],
        compiler_params=_params(), name=name,
    )(x, *consts_a, hist, s0, *consts_b)


def _ffn(x, ya, yb, yc, yd, wo, g2, wup, cw, wdn, hist, tm, front):
    b, l, _ = x.shape
    return pl.pallas_call(
        functools.partial(_ffn_kernel, tm=tm, front=front),
        grid=(b, l // tm),
        in_specs=[_tile_spec(tm, D_MODEL), _cols_spec(GROUP_W, tm)] + [_tile_spec(tm, GROUP_W)] * 3
        + [_const_spec(a) for a in (wo, g2, wup, cw, wdn)] + [_batch_spec(SUBLANE, 2 * D_FF)],
        out_specs=[_tile_spec(tm, D_MODEL), _batch_spec(SUBLANE, 2 * D_FF)],
        out_shape=[jax.ShapeDtypeStruct((b, l, D_MODEL), F32),
                   jax.ShapeDtypeStruct((b, SUBLANE, 2 * D_FF), F32)],
        scratch_shapes=[pltpu.VMEM((SUBLANE, 2 * D_FF), F32)],
        compiler_params=_params(), name="out_ffn",
    )(x, ya, yb, yc, yd, wo, g2, wup, cw, wdn, hist)


def _final_norm(x, g, tm, skip_tiles, out_rows):
    b = x.shape[0]
    return pl.pallas_call(
        _final_norm_kernel,
        grid=(b, out_rows // tm),
        in_specs=[pl.BlockSpec((1, tm, D_MODEL), lambda bb, j: (bb, j + skip_tiles, 0)), _const_spec(g)],
        out_specs=_tile_spec(tm, D_MODEL),
        out_shape=jax.ShapeDtypeStruct((b, out_rows, D_MODEL), F32),
        compiler_params=_params(), name="final_norm",
    )(x, g)


def _np_idx():
    z = IN_COLS
    zpad = lambda n: [z] * n
    rep = lambda base: [base + i for i in range(N_HEADS) for _ in range(HEAD_W)]
    grp = lambda base: [base + g * HEAD_W + i for g in (0, 0, 1, 1) for i in range(HEAD_W)]
    a = (list(range(0, 192)) + zpad(64) + list(range(192, 320))
         + list(range(320, 352)) + zpad(96)
         + list(range(336, 352)) + list(range(320, 336)) + zpad(96))
    b0 = A_COLS
    bcols = list(range(b0, b0 + B_COLS))
    c0 = b0 + B_COLS
    ccols = (list(range(c0, c0 + 256)) + list(range(c0 + 256, c0 + 512))
             + grp(c0 + 512) + grp(c0 + 640) + rep(c0 + 768))
    d0 = c0 + C_COLS
    dcols = list(range(d0, d0 + 1024)) + rep(d0 + 1024) + rep(d0 + 1028)
    xbc_exp = list(range(256)) + grp(256) + grp(384)
    xbc_back = (list(range(256)) + list(range(256, 320)) + list(range(384, 448))
                + list(range(512, 576)) + list(range(640, 704)))
    ffn_perm = []
    for f in range(D_FF // FFN_CW):
        ffn_perm += list(range(f * FFN_CW, (f + 1) * FFN_CW))
        ffn_perm += list(range(D_FF + f * FFN_CW, D_FF + (f + 1) * FFN_CW))
    ffn_back = np.argsort(np.array(ffn_perm))
    as_i = lambda v: np.asarray(v, np.int32)
    return dict(a=as_i(a), b=as_i(bcols), c=as_i(ccols), d=as_i(dcols), xbc_exp=as_i(xbc_exp),
                xbc_back=as_i(xbc_back), ffn_perm=as_i(ffn_perm), ffn_back=as_i(ffn_back))


_IDX = _np_idx()


def _rep_heads(v):
    return jnp.repeat(v, HEAD_W, axis=-1)


def _pad_rows(w, rows=SUBLANE):
    return jnp.pad(w, [(0, rows - w.shape[0])] + [(0, 0)] * (w.ndim - 1))


def _layer_consts(P, l):
    row = lambda v: v.reshape(1, -1).astype(F32)
    w_in = jnp.concatenate([P['w_in'][l], jnp.zeros((D_MODEL, 1), F32)], axis=1)
    c = {}
    c['g1'] = row(P['norm1_g'][l])
    c['wa'] = w_in[:, _IDX['a']].astype(BF16)
    c['wb'] = w_in[:, _IDX['b']].astype(BF16)
    c['wc'] = w_in[:, _IDX['c']].astype(BF16)
    c['wd'] = w_in[:, _IDX['d']].astype(BF16)
    c['gq'] = row(jnp.pad(P['a_gq'][l], (0, 64)))
    wuq = P['a_wuq'][l].reshape(A_QRANK, N_HEADS, A_NOPE + A_ROPE)
    rope = wuq[:, :, A_NOPE:]
    swap = jnp.concatenate([rope[..., 16:], rope[..., :16]], axis=-1)
    zeros = lambda n: jnp.zeros((A_QRANK, N_HEADS, n), F32)
    wq1 = jnp.concatenate([wuq, zeros(32)], axis=-1).reshape(A_QRANK, 512)
    wq2 = jnp.concatenate([zeros(64), swap, zeros(32)], axis=-1).reshape(A_QRANK, 512)
    c['wq1t'] = jnp.pad(wq1, ((0, 64), (0, 0))).T.astype(BF16)
    c['wq2t'] = jnp.pad(wq2, ((0, 64), (0, 0))).T.astype(BF16)
    c['gkv'] = row(P['a_gkv'][l])
    wuk = P['a_wuk'][l].reshape(A_KVRANK, N_HEADS, A_NOPE)
    c['wk'] = jnp.concatenate([wuk, jnp.zeros((A_KVRANK, N_HEADS, 64), F32)], axis=-1
                              ).reshape(A_KVRANK, 512).astype(BF16)
    ek = np.zeros((A_ROPE, N_HEADS, LANE), np.float32)
    for hh in range(N_HEADS):
        ek[np.arange(A_ROPE), hh, A_NOPE + np.arange(A_ROPE)] = 1.0
    c['ek'] = jnp.asarray(ek.reshape(A_ROPE, 512), BF16)
    c['ek128'] = jnp.pad(c['ek'], ((0, LANE - A_ROPE), (0, 0)))
    wuv = P['a_wuv'][l].reshape(A_KVRANK, N_HEADS, HEAD_W)
    c['wvt'] = jnp.concatenate([wuv, jnp.zeros((A_KVRANK, N_HEADS, V_ROWS - HEAD_W), F32)], axis=-1
                               ).reshape(A_KVRANK, N_HEADS * V_ROWS).T.astype(BF16)
    ones = np.zeros((N_HEADS, V_ROWS, 1), np.float32)
    ones[:, HEAD_W, 0] = 1.0
    c['ones_col'] = jnp.asarray(ones.reshape(N_HEADS * V_ROWS, 1))
    c['gout_col'] = P['a_gout'][l].reshape(GROUP_W, 1).astype(F32)
    c['mu'] = row(P['b_mu'][l])
    c['w0'] = row(P['b_w0'][l])
    z64 = jnp.zeros((64, GROUP_W), F32)
    c['wl'] = jnp.concatenate([jnp.concatenate([P['b_w2'][l], z64], axis=1),
                               jnp.concatenate([z64, P['b_a2'][l]], axis=1)], axis=0).astype(BF16)
    c['a0'] = row(P['b_a0'][l])
    c['g2b'] = P['b_g2'][l].astype(BF16)
    c['kk'] = row(P['b_kk'][l])
    c['ka'] = row(P['b_ka'][l])
    c['rk'] = row(P['b_rk'][l])
    c['gnw'] = row(P['b_gnw'][l])
    c['gnb'] = row(P['b_gnb'][l])
    c['c_cw'] = _pad_rows(P['c_convw'][l][:, _IDX['xbc_exp']])
    c['c_cb'] = row(P['c_convb'][l][_IDX['xbc_exp']])
    c['c_dtb'] = row(_rep_heads(P['c_dtb'][l]))
    c['c_alog'] = row(_rep_heads(P['c_alog'][l]))
    c['c_d'] = row(_rep_heads(P['c_d'][l]))
    c['c_gn'] = row(P['c_gnorm'][l])
    c['d_cw'] = _pad_rows(P['d_convw'][l])
    c['d_alog'] = row(_rep_heads(P['d_alog'][l]))
    c['d_dtb'] = row(_rep_heads(P['d_dtb'][l]))
    c['d_gn'] = row(jnp.tile(P['d_gnorm'][l], N_HEADS))
    c['wo'] = P['w_out'][l].astype(BF16)
    c['g2'] = row(P['norm2_g'][l])
    c['wup'] = P['f_wup'][l][:, _IDX['ffn_perm']].astype(BF16)
    c['f_cw'] = _pad_rows(P['f_convw'][l][:, _IDX['ffn_perm']])
    c['wdn'] = P['f_wdown'][l].astype(BF16)
    return c


def _embed_bd(s):
    b = s.shape[0]
    eye = jnp.eye(N_HEADS, dtype=s.dtype)
    return jnp.einsum('bhij,hg->bhigj', s, eye).reshape(b, GROUP_W, GROUP_W)


def _extract_bd(s):
    b = s.shape[0]
    s5 = s.reshape(b, N_HEADS, HEAD_W, N_HEADS, HEAD_W)
    return jnp.stack([s5[:, hh, :, hh, :] for hh in range(N_HEADS)], axis=1)


def _hist8(hist):
    return jnp.pad(hist, ((0, 0), (SUBLANE - hist.shape[1], 0), (0, 0)))


def _rope_table(pos):
    half = A_ROPE // 2
    inv = jnp.power(ROPE_BASE, -jnp.arange(half, dtype=F32) / half)
    ang = pos.astype(F32)[:, None] * inv
    cos, sin = jnp.cos(ang), jnp.sin(ang)
    cos2 = jnp.concatenate([cos, cos], axis=-1)
    sin2 = jnp.concatenate([-sin, sin], axis=-1)
    n = pos.shape[0]
    one = jnp.ones((n, A_NOPE), F32)
    z = lambda w: jnp.zeros((n, w), F32)
    tabq = jnp.concatenate([one, cos2, z(32), z(64), sin2, z(32)], axis=-1).T
    tabk = jnp.concatenate([cos2, z(96), sin2, z(96)], axis=-1)
    return tabq, tabk


def _trunk(x, pos, front, st, P, *, tm_scan, tm_ffn, c, tq, causal, n_keys_pad, scan_rows):
    b, l, _ = x.shape
    tabq, tabk = _rope_table(pos)
    new = {name: [] for name in ('ckv', 'krope', 'rwkv_S', 'rwkv_shift', 'ssd_S', 'ssd_conv',
                                 'gdn_S', 'gdn_conv', 'ffn_conv')}
    zeros_bd = jnp.zeros((b, GROUP_W, GROUP_W), F32)
    for li in range(DEPTH):
        c_ = P[li]
        prep = (x, c_['g1'], c_['wa'], c_['gq'], c_['wq1t'], c_['wq2t'], c_['gkv'], tabq, tabk, tm_ffn, front)
        if st is None:
            qt, ckv, krope, kf, vt = _mla_prep(
                *prep, kv_consts=(c_['wk'], c_['ek128'], c_['wvt'], c_['ones_col']), tk=tq)
            tkv, klo, khi = tq, front, l
            s_b = s_c = s_d = zeros_bd
            shift8 = jnp.zeros((b, SUBLANE, B_COLS), F32)
            chist = jnp.zeros((b, SUBLANE, 768), F32)
            dhist = jnp.zeros((b, SUBLANE, 768), F32)
            fhist = jnp.zeros((b, SUBLANE, 2 * D_FF), F32)
        else:
            qt, ckv, krope = _mla_prep(*prep)
            past = st['ckv'].shape[2]
            padk = n_keys_pad - past - l
            c_all = jnp.concatenate([st['ckv'][li], ckv, jnp.zeros((b, padk, A_KVRANK), F32)], axis=1)
            kr_all = jnp.concatenate([st['krope'][li], krope, jnp.zeros((b, padk, A_ROPE), F32)], axis=1)
            tkv, klo, khi = n_keys_pad, 0, past + l
            kf, vt = _kv_up(c_all, kr_all, c_['wk'], c_['ek'], c_['wvt'], c_['ones_col'], tkv)
            s_b = _embed_bd(st['rwkv_S'][li])
            s_c = _embed_bd(jnp.swapaxes(st['ssd_S'][li], -1, -2))
            s_d = _embed_bd(st['gdn_S'][li])
            shift8 = _hist8(st['rwkv_shift'][li][:, None, :])
            chist = _hist8(st['ssd_conv'][li][:, :, _IDX['xbc_exp']])
            dhist = _hist8(st['gdn_conv'][li])
            fhist = _hist8(st['ffn_conv'][li][:, :, _IDX['ffn_perm']])
        ya = _flash(qt, kf, vt, c_['gout_col'], tq, tkv, causal, klo, khi)
        yb, sb_new, shift_new = _scan_call(
            _rwkv_kernel, "rwkv7", x, (c_['g1'], c_['wb']), shift8, s_b,
            (c_['mu'], c_['w0'], c_['wl'], c_['a0'], c_['g2b'], c_['kk'], c_['ka'], c_['rk'],
             c_['gnw'], c_['gnb']), B_COLS, tm_scan, c, front, scan_rows)
        yc, sc_new, chist_new = _scan_call(
            _ssd_kernel, "ssd", x, (c_['g1'], c_['wc']), chist, s_c,
            (c_['c_cw'], c_['c_cb'], c_['c_dtb'], c_['c_alog'], c_['c_d'], c_['c_gn']),
            768, tm_scan, c, front, scan_rows)
        yd, sd_new, dhist_new = _scan_call(
            _gdn_kernel, "gdn", x, (c_['g1'], c_['wd']), dhist, s_d,
            (c_['d_cw'], c_['d_alog'], c_['d_dtb'], c_['d_gn']), 768, tm_scan, c, front, scan_rows)
        x, fhist_new = _ffn(x, ya, yb, yc, yd, c_['wo'], c_['g2'], c_['wup'], c_['f_cw'], c_['wdn'],
                            fhist, tm_ffn, front)
        new['ckv'].append(ckv[:, front:])
        new['krope'].append(krope[:, front:])
        new['rwkv_S'].append(_extract_bd(sb_new))
        new['rwkv_shift'].append(shift_new[:, SUBLANE - 1])
        new['ssd_S'].append(jnp.swapaxes(_extract_bd(sc_new), -1, -2))
        new['ssd_conv'].append(chist_new[:, SUBLANE - (C_CONV - 1):][:, :, _IDX['xbc_back']])
        new['gdn_S'].append(_extract_bd(sd_new))
        new['gdn_conv'].append(dhist_new[:, SUBLANE - (D_CONV - 1):])
        new['ffn_conv'].append(fhist_new[:, SUBLANE - (FFN_CONV - 1):][:, :, _IDX['ffn_back']])
    return x, {name: jnp.stack(vals) for name, vals in new.items()}


def kernel(x_prompt, x_sample, cache_mla_ckv, cache_mla_krope, state_rwkv, state_rwkv_shift, state_ssd, state_ssd_conv, state_gdn, state_gdn_conv, state_ffn_conv, meta_tokens, norm1_g, w_in, a_gq, a_wuq, a_gkv, a_wuk, a_wuv, a_gout, b_mu, b_w0, b_w2, b_a0, b_a2, b_g2, b_kk, b_ka, b_rk, b_gnw, b_gnb, c_convw, c_convb, c_dtb, c_alog, c_d, c_gnorm, d_convw, d_alog, d_dtb, d_gnorm, w_out, norm2_g, f_wup, f_convw, f_wdown, final_g):
    P = dict(norm1_g=norm1_g, w_in=w_in, a_gq=a_gq, a_wuq=a_wuq, a_gkv=a_gkv, a_wuk=a_wuk,
             a_wuv=a_wuv, a_gout=a_gout, b_mu=b_mu, b_w0=b_w0, b_w2=b_w2, b_a0=b_a0, b_a2=b_a2,
             b_g2=b_g2, b_kk=b_kk, b_ka=b_ka, b_rk=b_rk, b_gnw=b_gnw, b_gnb=b_gnb,
             c_convw=c_convw, c_convb=c_convb, c_dtb=c_dtb, c_alog=c_alog, c_d=c_d, c_gnorm=c_gnorm,
             d_convw=d_convw, d_alog=d_alog, d_dtb=d_dtb, d_gnorm=d_gnorm,
             w_out=w_out, norm2_g=norm2_g, f_wup=f_wup, f_convw=f_convw, f_wdown=f_wdown)
    P = [_layer_consts(P, li) for li in range(DEPTH)]
    fin = final_g.reshape(1, -1).astype(F32)
    b_p, seq, _ = x_prompt.shape
    lx = N_META + seq
    assert seq % ATT_BLOCK == 0 and N_META <= ATT_BLOCK
    front = ATT_BLOCK - N_META
    l_pad = front + lx
    meta = jnp.broadcast_to(meta_tokens[None].astype(F32), (b_p, N_META, D_MODEL))
    x_ext = jnp.concatenate([jnp.zeros((b_p, front, D_MODEL), F32), meta, x_prompt], axis=1)
    pos_p = jnp.arange(l_pad, dtype=jnp.int32) - (front + N_META)
    tm_ffn = max(t for t in (ROW_TILE, 2 * ROW_TILE, 3 * ROW_TILE) if l_pad % t == 0)
    y_p, ns_p = _trunk(x_ext, pos_p, front, None, P, tm_scan=SCAN_TILE, tm_ffn=tm_ffn, c=CHUNK, tq=ATT_BLOCK,
                       causal=True, n_keys_pad=l_pad, scan_rows=math.gcd(b_p, SCAN_ROWS))
    y_prompt = _final_norm(y_p, fin, ROW_TILE, (front + N_META) // ROW_TILE, seq)
    b_s, t_s, _ = x_sample.shape
    past = cache_mla_ckv.shape[2]
    assert t_s <= CHUNK and t_s % 16 == 0 and (t_s & (t_s - 1)) == 0
    st_s = dict(ckv=cache_mla_ckv, krope=cache_mla_krope, rwkv_S=state_rwkv, rwkv_shift=state_rwkv_shift,
                ssd_S=state_ssd, ssd_conv=state_ssd_conv, gdn_S=state_gdn, gdn_conv=state_gdn_conv,
                ffn_conv=state_ffn_conv)
    pos_s = past + jnp.arange(t_s, dtype=jnp.int32)
    n_keys_pad = -(-(past + t_s) // LANE) * LANE
    y_s, ns_s = _trunk(x_sample, pos_s, 0, st_s, P, tm_scan=t_s, tm_ffn=t_s, c=t_s, tq=t_s, causal=False,
                       n_keys_pad=n_keys_pad, scan_rows=math.gcd(b_s, SCAN_ROWS_SHORT))
    y_sample = _final_norm(y_s, fin, t_s, 0, t_s)
    keys = ('ckv', 'krope', 'rwkv_S', 'rwkv_shift', 'ssd_S', 'ssd_conv', 'gdn_S', 'gdn_conv', 'ffn_conv')
    return (y_prompt, y_sample) + tuple(ns_p[k] for k in keys) + tuple(ns_s[k] for k in keys)
```
